```python
import numpy as np
import jax
import jax.numpy as jnp
from jax import lax

D_MODEL = 2048
BATCH = 8
SEQ = 2048
DEPTH = 2
DEC_BATCH = 32
DEC_SEQ = 4
PAST_LEN = 8192
PAGE_SIZE = 128

HEAD_DIM = 128
SCALE = HEAD_DIM ** -0.5
NSA_HEADS = 8
NSA_KV_GROUPS = 2
NSA_HPG = NSA_HEADS // NSA_KV_GROUPS
CMP_BLOCK = 32
CMP_STRIDE = 16
CMP_HIDDEN = HEAD_DIM
SLC_BLOCK = 64
SLC_TOPN = 16
WINDOW = 512
WIN_QBLOCK = 128
FORCED_SCORE = 1.0e4
MOBA_HEADS = 8
MOBA_BLOCK = 256
MOBA_TOPK = 3
Q_CHUNK_NSA = 16
Q_CHUNK_MOBA = 8
CONV_WIDTH = 31
D_CONV = D_MODEL
N_GROUPS = 4
EXPERTS_PER_GROUP = 8
N_EXPERTS = N_GROUPS * EXPERTS_PER_GROUP
TOP_K_IN_GROUP = 2
D_EXPERT = D_MODEL // 8
N_EVEN = (DEPTH + 1) // 2
N_ODD = DEPTH // 2
NSA_Q_W = NSA_HEADS * HEAD_DIM
NSA_KV_W = 2 * NSA_KV_GROUPS * HEAD_DIM
NSA_GATE_W = 3 * NSA_HEADS
MOBA_Q_W = MOBA_HEADS * HEAD_DIM
MOBA_KV_W = 2 * MOBA_HEADS * HEAD_DIM
ATTN_IN = NSA_Q_W + 3 * NSA_KV_W + NSA_GATE_W + MOBA_Q_W + MOBA_KV_W
ATTN_OUT = (NSA_HEADS + MOBA_HEADS) * HEAD_DIM
EPS = 1e-6
NEG = -1e30

kernel_name = 'hybrid_nsa_moba_conformer_hmoe_step'


def rms_norm(x, g):
    xf = x.astype(jnp.float32)
    y = xf * lax.rsqrt(jnp.mean(xf * xf, axis=-1, keepdims=True) + EPS)
    return (y * g.astype(jnp.float32)).astype(x.dtype)


def layer_norm(x, g, b):
    xf = x.astype(jnp.float32)
    xc = xf - jnp.mean(xf, axis=-1, keepdims=True)
    y = xc * lax.rsqrt(jnp.mean(xc * xc, axis=-1, keepdims=True) + EPS)
    return (y * g.astype(jnp.float32) + b.astype(jnp.float32)).astype(x.dtype)


def masked_softmax(s, mask):
    s = jnp.where(mask, s.astype(jnp.float32), NEG)
    e = jnp.where(mask, jnp.exp(s - jnp.max(s, axis=-1, keepdims=True)), 0.0)
    return e / jnp.maximum(jnp.sum(e, axis=-1, keepdims=True), 1e-30)


def adaln_params(c, w, b):
    m = jnp.dot(jax.nn.silu(c), w) + b
    return jnp.split(m[:, None, :], 6, axis=-1)


def modulate(h, shift, scale):
    return h * (1.0 + scale) + shift


def over_query_chunks(fn, chunk, pos, *xs):
    t = pos.shape[0]
    if t <= chunk or t % chunk:
        return fn(pos, *xs)
    n = t // chunk
    split = lambda a: jnp.moveaxis(a.reshape(a.shape[0], n, chunk, *a.shape[2:]), 1, 0)
    out = lax.map(lambda args: fn(*args), (jnp.asarray(pos).reshape(n, chunk),) + tuple(split(a) for a in xs))
    out = jnp.moveaxis(out, 0, 1)
    return out.reshape(out.shape[0], t, *out.shape[3:])


def gather_pages(cache, layer, page_table):
    rows = cache[layer, page_table]
    return rows.reshape(rows.shape[0], -1, *rows.shape[3:])


def nsa_compress(kv, pe, w1, w2):
    n_cmp = (kv.shape[1] - CMP_BLOCK) // CMP_STRIDE + 1
    idx = np.arange(n_cmp)[:, None] * CMP_STRIDE + np.arange(CMP_BLOCK)[None, :]
    blocks = kv[:, idx]
    pre = jnp.einsum('bnpcgd,cpde->bncge', blocks, w1)
    pre = pre + jnp.einsum('cpd,cpde->ce', pe, w1)[:, None, :]
    return jnp.einsum('bncge,ced->bncgd', jax.nn.gelu(pre), w2)


def nsa_compressed_and_selected(q, q_pos, kv_cmp, kv_slc, pe, w1, w2):
    B, T, G, Hg, _ = q.shape
    L = kv_cmp.shape[1]
    cmp = nsa_compress(kv_cmp, pe, w1, w2)
    n_cmp = cmp.shape[1]
    cmp_start = np.arange(n_cmp) * CMP_STRIDE
    cmp_mask = (cmp_start + CMP_BLOCK - 1)[None, :] <= q_pos[:, None]
    s = jnp.einsum('btghd,bngd->btghn', q, cmp[:, :, 0]) * SCALE
    p_cmp = masked_softmax(s, cmp_mask[None, :, None, None, :])
    o_cmp = jnp.einsum('btghn,bngd->btghd', p_cmp.astype(q.dtype), cmp[:, :, 1])
    n_slc = -(-L // SLC_BLOCK)
    slc_start = np.arange(n_slc) * SLC_BLOCK
    shared = (np.minimum(cmp_start[:, None] + CMP_BLOCK, slc_start[None, :] + SLC_BLOCK)
              - np.maximum(cmp_start[:, None], slc_start[None, :]))
    cmp_to_slc = jnp.asarray(np.clip(shared, 0, None) / CMP_STRIDE, jnp.float32)
    imp = jnp.einsum('btghn,nj->btgj', p_cmp, cmp_to_slc)
    blk = np.arange(n_slc)[None, :]
    q_blk = (q_pos // SLC_BLOCK)[:, None]
    forced = (blk == 0) | (blk == q_blk) | (blk == q_blk - 1)
    causal = slc_start[None, :] <= q_pos[:, None]
    score = jnp.where(forced[None, :, None, :], FORCED_SCORE, imp)
    score = jnp.where(causal[None, :, None, :], score, -1.0)
    _, sel = lax.top_k(score, min(SLC_TOPN, n_slc))
    kvb = jnp.pad(kv_slc, ((0, 0), (0, n_slc * SLC_BLOCK - L), (0, 0), (0, 0), (0, 0)))
    kvb = kvb.reshape(B, n_slc, SLC_BLOCK, 2, G, HEAD_DIM)
    bi = jnp.arange(B)[:, None, None, None]
    gi = jnp.arange(G)[None, None, :, None]

    def attend(pos, qc, selc):
        tc = qc.shape[1]
        kvg = kvb[bi, selc, :, :, gi]
        kpos = selc[..., None] * SLC_BLOCK + jnp.arange(SLC_BLOCK)
        n_keys = selc.shape[-1] * SLC_BLOCK
        mask = (kpos <= pos[None, :, None, None, None]).reshape(B, tc, G, 1, n_keys)
        s = jnp.einsum('btghd,btgnkd->btghnk', qc, kvg[..., 0, :]) * SCALE
        pr = masked_softmax(s.reshape(B, tc, G, Hg, n_keys), mask)
        v = kvg[..., 1, :].reshape(B, tc, G, n_keys, HEAD_DIM)
        return jnp.einsum('btghm,btgmd->btghd', pr.astype(qc.dtype), v)

    o_slc = over_query_chunks(attend, Q_CHUNK_NSA, q_pos, q, sel)
    return o_cmp, o_slc


def nsa_window_banded(q, kv):
    B, T, G, Hg, _ = q.shape
    nqb = T // WIN_QBLOCK
    span = WINDOW + WIN_QBLOCK
    kvp = jnp.pad(kv, ((0, 0), (WINDOW, 0), (0, 0), (0, 0), (0, 0)))
    idx = np.arange(nqb)[:, None] * WIN_QBLOCK + np.arange(span)[None, :]
    kvb = kvp[:, idx]
    qb = q.reshape(B, nqb, WIN_QBLOCK, G, Hg, HEAD_DIM)
    qpos = np.arange(T).reshape(nqb, WIN_QBLOCK)[:, :, None]
    kpos = (idx - WINDOW)[:, None, :]
    mask = (kpos <= qpos) & (kpos > qpos - WINDOW) & (kpos >= 0)
    s = jnp.einsum('bnqghd,bnkgd->bnqghk', qb, kvb[:, :, :, 0]) * SCALE
    pr = masked_softmax(s, mask[None, :, :, None, None, :])
    o = jnp.einsum('bnqghk,bnkgd->bnqghd', pr.astype(q.dtype), kvb[:, :, :, 1])
    return o.reshape(B, T, G, Hg, HEAD_DIM)


def nsa_window_dense(q, kv, q_pos, k_pos):
    mask = (k_pos[None, :] <= q_pos[:, None]) & (k_pos[None, :] > q_pos[:, None] - WINDOW)
    s = jnp.einsum('btghd,bkgd->btghk', q, kv[:, :, 0]) * SCALE
    pr = masked_softmax(s, mask[None, :, None, None, :])
    return jnp.einsum('btghk,bkgd->btghd', pr.astype(q.dtype), kv[:, :, 1])


def moba_attention(q, q_pos, kv):
    B, T, H, _ = q.shape
    L = kv.shape[1]
    nb = -(-L // MOBA_BLOCK)
    kvb = jnp.pad(kv, ((0, 0), (0, nb * MOBA_BLOCK - L), (0, 0), (0, 0), (0, 0)))
    kvb = kvb.reshape(B, nb, MOBA_BLOCK, 2, H, HEAD_DIM)
    k_mean = jnp.mean(kvb[:, :, :, 0].astype(jnp.float32), axis=2)
    gate = jnp.einsum('bthd,bnhd->bthn', q.astype(jnp.float32), k_mean)
    q_blk = q_pos // MOBA_BLOCK
    fully_past = np.arange(nb)[None, :] < q_blk[:, None]
    gate = jnp.where(fully_past[None, :, None, :], gate, NEG)
    _, top = lax.top_k(gate, min(MOBA_TOPK, nb))
    own = jnp.broadcast_to(jnp.asarray(q_blk, jnp.int32)[None, :, None, None], (B, T, H, 1))
    sel = jnp.concatenate([top, own], axis=-1)
    sel_ok = jnp.concatenate([top < own, jnp.ones((B, T, H, 1), bool)], axis=-1)
    bi = jnp.arange(B)[:, None, None, None]
    hi = jnp.arange(H)[None, None, :, None]

    def attend(pos, qc, selc, okc):
        tc = qc.shape[1]
        kvg = kvb[bi, selc, :, :, hi]
        kpos = selc[..., None] * MOBA_BLOCK + jnp.arange(MOBA_BLOCK)
        n_keys = selc.shape[-1] * MOBA_BLOCK
        mask = (okc[..., None] & (kpos <= pos[None, :, None, None, None])).reshape(B, tc, H, n_keys)
        s = jnp.einsum('bthd,bthnkd->bthnk', qc, kvg[..., 0, :]) * SCALE
        pr = masked_softmax(s.reshape(B, tc, H, n_keys), mask)
        v = kvg[..., 1, :].reshape(B, tc, H, n_keys, HEAD_DIM)
        return jnp.einsum('bthm,bthmd->bthd', pr.astype(qc.dtype), v)

    return over_query_chunks(attend, Q_CHUNK_MOBA, q_pos, q, sel, sel_ok)


def attn_mixer(h, e, p, past):
    B, T, _ = h.shape
    z = jnp.dot(h, p['attn_w_in'][e])
    cuts = [int(v) for v in np.cumsum([NSA_Q_W, NSA_KV_W, NSA_KV_W, NSA_KV_W, NSA_GATE_W, MOBA_Q_W])]
    q_n, kv_c, kv_s, kv_w, gate, q_m, kv_m = jnp.split(z, cuts, axis=-1)
    q_n = q_n.reshape(B, T, NSA_KV_GROUPS, NSA_HPG, HEAD_DIM)
    kv_c = kv_c.reshape(B, T, 2, NSA_KV_GROUPS, HEAD_DIM)
    kv_s = kv_s.reshape(B, T, 2, NSA_KV_GROUPS, HEAD_DIM)
    kv_w = kv_w.reshape(B, T, 2, NSA_KV_GROUPS, HEAD_DIM)
    gate = jax.nn.sigmoid(gate.astype(jnp.float32)).astype(h.dtype).reshape(B, T, NSA_KV_GROUPS, NSA_HPG, 3)
    q_m = q_m.reshape(B, T, MOBA_HEADS, HEAD_DIM)
    kv_m = kv_m.reshape(B, T, 2, MOBA_HEADS, HEAD_DIM)
    if past is None:
        q_pos = np.arange(T, dtype=np.int32)
        full_c, full_s, full_m = kv_c, kv_s, kv_m
        o_win = nsa_window_banded(q_n, kv_w)
        new_win = kv_w[:, T - min(WINDOW, T):]
    else:
        q_pos = (PAST_LEN + np.arange(T)).astype(np.int32)
        pt = past['page_table']
        full_c = jnp.concatenate([gather_pages(past['nsa_cmp'], e, pt), kv_c], axis=1)
        full_s = jnp.concatenate([gather_pages(past['nsa_slc'], e, pt), kv_s], axis=1)
        full_m = jnp.concatenate([gather_pages(past['moba'], e, pt), kv_m], axis=1)
        win = jnp.concatenate([past['nsa_win'][e], kv_w], axis=1)
        w_buf = past['nsa_win'].shape[2]
        k_pos = PAST_LEN - w_buf + np.arange(w_buf + T)
        o_win = nsa_window_dense(q_n, win, q_pos, k_pos)
        new_win = win[:, T:]
    o_cmp, o_slc = nsa_compressed_and_selected(q_n, q_pos, full_c, full_s, p['nsa_cmp_pe'][e],
                                               p['nsa_cmp_w1'][e], p['nsa_cmp_w2'][e])
    o_moba = moba_attention(q_m, q_pos, full_m)
    o_nsa = gate[..., 0:1] * o_cmp + gate[..., 1:2] * o_slc + gate[..., 2:3] * o_win
    o = jnp.concatenate([o_nsa.reshape(B, T, NSA_Q_W), o_moba.reshape(B, T, MOBA_Q_W)], axis=-1)
    return jnp.dot(o, p['attn_w_out'][e]), (kv_c, kv_s, new_win, kv_m)


def conv_mixer(h, o, p, past):
    B = h.shape[0]
    a = jnp.dot(h, p['conv_w_pw1'][o]) + p['conv_b_pw1'][o]
    u = a[..., :D_CONV] * jax.nn.sigmoid(a[..., D_CONV:])
    buf = jnp.zeros((B, CONV_WIDTH - 1, D_CONV), u.dtype) if past is None else past['conv'][o].astype(u.dtype)
    full = jnp.concatenate([buf, u], axis=1)
    y = lax.conv_general_dilated(full, p['conv_w_dw'][o][:, None, :], window_strides=(1,), padding='VALID',
                                 dimension_numbers=('NWC', 'WIO', 'NWC'), feature_group_count=D_CONV)
    y = jax.nn.silu(layer_norm(y + p['conv_b_dw'][o], p['conv_ln_g'][o], p['conv_ln_b'][o]))
    return jnp.dot(y, p['conv_w_pw2'][o]) + p['conv_b_pw2'][o], full[:, -(CONV_WIDTH - 1):]


def hier_moe(h, wg, bg, we, be, w_gate, w_up, w_down):
    B, T, D = h.shape
    hf = h.reshape(B * T, D)
    n = hf.shape[0]
    rows = jnp.arange(n)
    lg = (jnp.dot(hf, wg) + bg).astype(jnp.float32)
    grp = jnp.argmax(lg, axis=-1)
    p_grp = jax.nn.softmax(lg, axis=-1)[rows, grp][:, None]
    le = (jnp.dot(hf, we) + be).astype(jnp.float32).reshape(n, N_GROUPS, EXPERTS_PER_GROUP)[rows, grp]
    top_l, top_i = lax.top_k(le, TOP_K_IN_GROUP)
    w = p_grp * jax.nn.softmax(top_l, axis=-1)
    eid = grp[:, None] * EXPERTS_PER_GROUP + top_i
    combine = jnp.einsum('nk,nke->ne', w, jax.nn.one_hot(eid, N_EXPERTS, dtype=jnp.float32)).astype(h.dtype)
    a = jnp.einsum('nd,edf->nef', hf, w_gate)
    u = jnp.einsum('nd,edf->nef', hf, w_up)
    hid = jax.nn.silu(a) * u * combine[:, :, None]
    return jnp.einsum('nef,efd->nd', hid, w_down).reshape(B, T, D)


def trunk(x, c, p, past):
    st = {'cmp': [], 'slc': [], 'win': [], 'moba': [], 'conv': []}
    for layer in range(DEPTH):
        sh1, sc1, g1, sh2, sc2, g2 = adaln_params(c, p['ada_w'][layer], p['ada_b'][layer])
        h = modulate(rms_norm(x, p['norm_mix_g'][layer]), sh1, sc1)
        if layer % 2 == 0:
            out, (n_c, n_s, n_w, n_m) = attn_mixer(h, layer // 2, p, past)
            st['cmp'].append(n_c)
            st['slc'].append(n_s)
            st['win'].append(n_w)
            st['moba'].append(n_m)
        else:
            out, n_conv = conv_mixer(h, layer // 2, p, past)
            st['conv'].append(n_conv)
        x = x + g1 * out
        h = modulate(rms_norm(x, p['norm_ffn_g'][layer]), sh2, sc2)
        x = x + g2 * hier_moe(h, p['moe_wg'][layer], p['moe_bg'][layer], p['moe_we'][layer], p['moe_be'][layer],
                              p['moe_w_gate'][layer], p['moe_w_up'][layer], p['moe_w_down'][layer])
    y = rms_norm(x, p['final_norm_g'])
    return y, tuple(jnp.stack(st[k]) for k in ('cmp', 'slc', 'win', 'moba', 'conv'))


def setup_inputs(seed: int = 0) -> dict:
    key = jax.random.key(seed)
    keys = iter(jax.random.split(key, 48))
    nrm = lambda shape, s: s * jax.random.normal(next(keys), shape, jnp.float32)
    n_pages = PAST_LEN // PAGE_SIZE
    n_used = DEC_BATCH * n_pages
    n_pool = n_used + max(1, n_used // 4)
    w_buf = min(WINDOW, PAST_LEN)
    D = D_MODEL
    page_table = jax.random.permutation(next(keys), n_pool)[:n_used].reshape(DEC_BATCH, n_pages).astype(jnp.int32)
    return {
        'x_prompt': nrm((BATCH, SEQ, D), 1.0),
        'x_sample': nrm((DEC_BATCH, DEC_SEQ, D), 1.0),
        'cache_nsa_cmp_kv': nrm((N_EVEN, n_pool, PAGE_SIZE, 2, NSA_KV_GROUPS, HEAD_DIM), 1.0),
        'cache_nsa_slc_kv': nrm((N_EVEN, n_pool, PAGE_SIZE, 2, NSA_KV_GROUPS, HEAD_DIM), 1.0),
        'state_nsa_win_kv': nrm((N_EVEN, DEC_BATCH, w_buf, 2, NSA_KV_GROUPS, HEAD_DIM), 1.0),
        'cache_moba_kv': nrm((N_EVEN, n_pool, PAGE_SIZE, 2, MOBA_HEADS, HEAD_DIM), 1.0),
        'state_conv': nrm((N_ODD, DEC_BATCH, CONV_WIDTH - 1, D_CONV), 0.5),
        'page_table': page_table,
        'c_prompt': nrm((BATCH, D), 1.0),
        'c_sample': nrm((DEC_BATCH, D), 1.0),
        'norm_mix_g': 1.0 + nrm((DEPTH, D), 0.02),
        'norm_ffn_g': 1.0 + nrm((DEPTH, D), 0.02),
        'ada_w': nrm((DEPTH, D, 6 * D), 0.5 * D ** -0.5),
        'ada_b': nrm((DEPTH, 6 * D), 0.02),
        'attn_w_in': nrm((N_EVEN, D, ATTN_IN), D ** -0.5),
        'attn_w_out': nrm((N_EVEN, ATTN_OUT, D), ATTN_OUT ** -0.5),
        'nsa_cmp_pe': nrm((N_EVEN, 2, CMP_BLOCK, HEAD_DIM), 0.1),
        'nsa_cmp_w1': nrm((N_EVEN, 2, CMP_BLOCK, HEAD_DIM, CMP_HIDDEN), (CMP_BLOCK * HEAD_DIM) ** -0.5),
        'nsa_cmp_w2': nrm((N_EVEN, 2, CMP_HIDDEN, HEAD_DIM), CMP_HIDDEN ** -0.5),
        'conv_w_pw1': nrm((N_ODD, D, 2 * D_CONV), D ** -0.5),
        'conv_b_pw1': nrm((N_ODD, 2 * D_CONV), 0.02),
        'conv_w_dw': nrm((N_ODD, CONV_WIDTH, D_CONV), CONV_WIDTH ** -0.5),
        'conv_b_dw': nrm((N_ODD, D_CONV), 0.02),
        'conv_ln_g': 1.0 + nrm((N_ODD, D_CONV), 0.02),
        'conv_ln_b': nrm((N_ODD, D_CONV), 0.02),
        'conv_w_pw2': nrm((N_ODD, D_CONV, D), D_CONV ** -0.5),
        'conv_b_pw2': nrm((N_ODD, D), 0.02),
        'moe_wg': nrm((DEPTH, D, N_GROUPS), D ** -0.5),
        'moe_bg': nrm((DEPTH, N_GROUPS), 0.01),
        'moe_we': nrm((DEPTH, D, N_EXPERTS), D ** -0.5),
        'moe_be': nrm((DEPTH, N_EXPERTS), 0.01),
        'moe_w_gate': nrm((DEPTH, N_EXPERTS, D, D_EXPERT), D ** -0.5),
        'moe_w_up': nrm((DEPTH, N_EXPERTS, D, D_EXPERT), D ** -0.5),
        'moe_w_down': nrm((DEPTH, N_EXPERTS, D_EXPERT, D), D_EXPERT ** -0.5),
        'final_norm_g': 1.0 + nrm((D,), 0.02),
    }


def reference(x_prompt, x_sample, cache_nsa_cmp_kv, cache_nsa_slc_kv, state_nsa_win_kv, cache_moba_kv, state_conv,
              page_table, c_prompt, c_sample, norm_mix_g, norm_ffn_g, ada_w, ada_b, attn_w_in, attn_w_out,
              nsa_cmp_pe, nsa_cmp_w1, nsa_cmp_w2, conv_w_pw1, conv_b_pw1, conv_w_dw, conv_b_dw, conv_ln_g, conv_ln_b,
              conv_w_pw2, conv_b_pw2, moe_wg, moe_bg, moe_we, moe_be, moe_w_gate, moe_w_up, moe_w_down, final_norm_g):
    p = dict(norm_mix_g=norm_mix_g, norm_ffn_g=norm_ffn_g, ada_w=ada_w, ada_b=ada_b, attn_w_in=attn_w_in,
             attn_w_out=attn_w_out, nsa_cmp_pe=nsa_cmp_pe, nsa_cmp_w1=nsa_cmp_w1, nsa_cmp_w2=nsa_cmp_w2,
             conv_w_pw1=conv_w_pw1, conv_b_pw1=conv_b_pw1, conv_w_dw=conv_w_dw, conv_b_dw=conv_b_dw,
             conv_ln_g=conv_ln_g, conv_ln_b=conv_ln_b, conv_w_pw2=conv_w_pw2, conv_b_pw2=conv_b_pw2,
             moe_wg=moe_wg, moe_bg=moe_bg, moe_we=moe_we, moe_be=moe_be, moe_w_gate=moe_w_gate,
             moe_w_up=moe_w_up, moe_w_down=moe_w_down, final_norm_g=final_norm_g)
    y_prompt, (cmp_p, slc_p, win_p, moba_p, conv_p) = trunk(x_prompt, c_prompt, p, None)
    past = dict(page_table=page_table, nsa_cmp=cache_nsa_cmp_kv, nsa_slc=cache_nsa_slc_kv,
                nsa_win=state_nsa_win_kv, moba=cache_moba_kv, conv=state_conv)
    y_sample, (cmp_s, slc_s, win_s, moba_s, conv_s) = trunk(x_sample, c_sample, p, past)
    return (y_prompt, y_sample, cmp_p, cmp_s, slc_p, slc_s, win_p, win_s, moba_p, moba_s, conv_p, conv_s)
```

```python
import functools

import numpy as np
import jax
import jax.numpy as jnp
from jax import lax
from jax.experimental import pallas as pl
from jax.experimental.pallas import tpu as pltpu

F32 = jnp.float32
BF16 = jnp.bfloat16
I32 = jnp.int32

HD = 128
LANES = 128
SCALE = HD ** -0.5
NSA_G = 2
NSA_HPG = 4
CMP_BLOCK = 32
CMP_STRIDE = 16
SLC_BLOCK = 64
SLC_TOPN = 16
WINDOW = 512
FORCED = 1.0e4
MOBA_H = 8
MOBA_BLOCK = 256
MOBA_TOPK = 3
CONV_W = 31
N_GROUPS = 4
EPG = 8
N_EXP = N_GROUPS * EPG
PAGE = 128
EPS = 1e-6
NEG = -1e30
TE = 256
VMEM_LIMIT = 56 * 1024 * 1024


def _cp(sem, vmem=VMEM_LIMIT):
    return pltpu.CompilerParams(dimension_semantics=sem, vmem_limit_bytes=vmem)


def _nt(a, b):
    return lax.dot_general(a, b, (((1,), (1,)), ((), ())), preferred_element_type=F32)


def _mm(a, b):
    return jnp.dot(a, b, preferred_element_type=F32)


def _split(x):
    hi = x.astype(BF16)
    lo = (x - hi.astype(F32)).astype(BF16)
    return hi, lo


def _sigmoid(x):
    return 1.0 / (1.0 + jnp.exp(-x))


def _rms_mod(x, g, shift, scale):
    y = x * lax.rsqrt(jnp.mean(x * x, axis=-1, keepdims=True) + EPS)
    return (y * g) * (1.0 + scale) + shift


def _masked_softmax(s, valid):
    sm = jnp.where(valid, s, NEG)
    mx = jnp.max(sm, axis=-1, keepdims=True)
    e = jnp.where(valid, jnp.exp(sm - mx), 0.0)
    return e / jnp.maximum(jnp.sum(e, axis=-1, keepdims=True), 1e-30)


def _ada_kernel(c_ref, w_ref, b_ref, o_ref):
    c = c_ref[...]
    s = c * _sigmoid(c)
    shi, slo = _split(s)
    whi, wlo = _split(w_ref[0])
    o_ref[0] = _mm(shi, whi) + _mm(shi, wlo) + _mm(slo, whi) + b_ref[0]


def ada_params(c_all, ada_w, ada_b):
    depth, d, n6 = ada_w.shape
    r = c_all.shape[0]
    tn = 1024 if n6 % 1024 == 0 else 512
    return pl.pallas_call(
        _ada_kernel,
        out_shape=jax.ShapeDtypeStruct((depth, r, n6), F32),
        grid=(depth, n6 // tn),
        in_specs=[
            pl.BlockSpec((r, d), lambda l, j: (0, 0)),
            pl.BlockSpec((1, d, tn), lambda l, j: (l, 0, j)),
            pl.BlockSpec((1, 1, tn), lambda l, j: (l, 0, j)),
        ],
        out_specs=pl.BlockSpec((1, r, tn), lambda l, j: (l, 0, j)),
        compiler_params=_cp(("arbitrary", "arbitrary")),
        name="ada_params",
    )(c_all, ada_w, ada_b.reshape(depth, 1, n6))


_TN = 512
_SEGS = ((0, 2), (2, 1), (3, 1), (4, 1), (5, 2), (7, 4))
_N_MAIN_TILES = 11


def _inproj_kernel(x_ref, g_ref, sh_ref, sc_ref, w_ref, wg_ref,
                   qn_ref, kvc_ref, kvs_ref, kvw_ref, qm_ref, kvm_ref, gate_ref, h_scr):
    j = pl.program_id(1)

    @pl.when(j == 0)
    def _():
        h = _rms_mod(x_ref[...], g_ref[...], sh_ref[0], sc_ref[0]).astype(BF16)
        h_scr[...] = h
        gate_ref[...] = _sigmoid(_mm(h, wg_ref[...]))

    z = _mm(h_scr[...], w_ref[...])
    outs = (qn_ref, kvc_ref, kvs_ref, kvw_ref, qm_ref, kvm_ref)
    for ref, (start, n) in zip(outs, _SEGS):
        @pl.when((j >= start) & (j < start + n))
        def _(ref=ref):
            ref[...] = z


def attn_in_proj(x2d, g, shift, scale, w_main, w_gate, tm, tiles_per_mod):
    m, d = x2d.shape
    r = shift.shape[1]

    def seg_spec(start, n):
        return pl.BlockSpec((tm, _TN), lambda i, j: (i, jnp.clip(j - start, 0, n - 1)))

    out_shape = [jax.ShapeDtypeStruct((m, n * _TN), F32) for (_, n) in _SEGS]
    out_shape.append(jax.ShapeDtypeStruct((m, LANES), F32))
    out_specs = [seg_spec(s, n) for (s, n) in _SEGS]
    out_specs.append(pl.BlockSpec((tm, LANES), lambda i, j: (i, 0)))
    mod_spec = pl.BlockSpec((1, r, d), lambda i, j: (i // tiles_per_mod, 0, 0))
    return pl.pallas_call(
        _inproj_kernel,
        out_shape=out_shape,
        grid=(m // tm, _N_MAIN_TILES),
        in_specs=[
            pl.BlockSpec((tm, d), lambda i, j: (i, 0)),
            pl.BlockSpec((1, d), lambda i, j: (0, 0)),
            mod_spec, mod_spec,
            pl.BlockSpec((d, _TN), lambda i, j: (0, j)),
            pl.BlockSpec((d, LANES), lambda i, j: (0, 0)),
        ],
        out_specs=out_specs,
        scratch_shapes=[pltpu.VMEM((tm, d), BF16)],
        compiler_params=_cp(("arbitrary", "arbitrary")),
        name="attn_in_proj",
    )(x2d, g, shift, scale, w_main, w_gate)


def _gelu_tanh(x):
    return 0.5 * x * (1.0 + jnp.tanh(0.7978845608028654 * (x + 0.044715 * x * x * x)))


def _compress_groups(x, c, g, wab_ref):
    col = (c * NSA_G + g) * HD
    xs = jnp.concatenate([x[:, r * 4 * HD + col: r * 4 * HD + col + HD] for r in range(CMP_STRIDE)],
                         axis=1).astype(BF16)
    return _mm(xs, wab_ref[c])


def _compress_finish(ab, c, pe_ref, wab_ref, w2_ref):
    rows = ab.shape[0]
    pe2 = _mm(pe_ref[c], wab_ref[c])
    pe_term = pe2[0:1, :HD] + pe2[1:2, HD:]
    pre = ab[:, :HD] + pltpu.roll(ab[:, HD:], rows - 1, 0) + pe_term
    return _mm(_gelu_tanh(pre).astype(BF16), w2_ref[c])


def _compress_prompt_kernel(x_ref, wab_ref, pe_ref, w2_ref, ck_ref, cv_ref):
    x = x_ref[0]
    for c, ref in ((0, ck_ref), (1, cv_ref)):
        for g in range(NSA_G):
            ab = _compress_groups(x, c, g, wab_ref)
            ref[0, g] = _compress_finish(ab, c, pe_ref, wab_ref, w2_ref).astype(BF16)


def compress_prompt(kvc, b, t, wab, pe2, w2):
    ng = t // CMP_STRIDE
    x = kvc.reshape(b, ng, CMP_STRIDE * 4 * HD)
    out = jax.ShapeDtypeStruct((b, NSA_G, ng, HD), BF16)
    ospec = pl.BlockSpec((1, NSA_G, ng, HD), lambda i: (i, 0, 0, 0))
    return pl.pallas_call(
        _compress_prompt_kernel,
        out_shape=[out, out],
        grid=(b,),
        in_specs=[
            pl.BlockSpec((1, ng, CMP_STRIDE * 4 * HD), lambda i: (i, 0, 0)),
            pl.BlockSpec(wab.shape, lambda i: (0, 0, 0)),
            pl.BlockSpec(pe2.shape, lambda i: (0, 0, 0)),
            pl.BlockSpec(w2.shape, lambda i: (0, 0, 0)),
        ],
        out_specs=[ospec, ospec],
        compiler_params=_cp(("arbitrary",)),
        name="nsa_compress_prompt",
    )(x, wab, pe2, w2)


def _select_blocks(score, lane, n_blocks, topn):
    cnt = jnp.zeros(score.shape, F32)
    for i in range(n_blocks):
        ci = score[:, i:i + 1]
        beats = (ci > score) | ((ci == score) & (lane > i))
        cnt = cnt + beats.astype(F32)
    return (cnt < topn) & (lane < n_blocks)


def _nsa_prompt_kernel(q_ref, ck_ref, cv_ref, ks_ref, vs_ref, kw_ref, vw_ref, gate_ref, c2s_ref, ee_ref,
                       o_ref, selx_ref, m_ref, l_ref, acc_ref, *, tq, tk, t_len, n_cmp, n_slc):
    g = pl.program_id(1)
    q0 = pl.program_id(2) * tq
    q = q_ref[...] * SCALE
    q4 = jnp.concatenate([q[:, h * HD:(h + 1) * HD] for h in range(NSA_HPG)], axis=0).astype(BF16)
    trow = q0 + lax.broadcasted_iota(I32, (tq, 1), 0)
    t4 = jnp.concatenate([trow] * NSA_HPG, axis=0)
    lane = lax.broadcasted_iota(I32, (1, LANES), 1)

    s = _nt(q4, ck_ref[0, 0])
    valid = (lane * CMP_STRIDE + (CMP_BLOCK - 1) <= t4) & (lane < n_cmp)
    p = _masked_softmax(s, valid)
    o_cmp = _mm(p.astype(BF16), cv_ref[0, 0])

    psum = p[0:tq] + p[tq:2 * tq] + p[2 * tq:3 * tq] + p[3 * tq:4 * tq]
    phi, plo = _split(psum)
    imp = _mm(phi, c2s_ref[...]) + _mm(plo, c2s_ref[...])
    qb = trow >> 6
    forced = (lane == 0) | (lane == qb) | (lane == qb - 1)
    causal = lane * SLC_BLOCK <= trow
    score = jnp.where(forced, FORCED, imp)
    score = jnp.where(causal, score, -1.0)
    score = jnp.where(lane < n_slc, score, -2.0)
    sel = _select_blocks(score, lane, n_slc, min(SLC_TOPN, n_slc))
    selx_ref[...] = _mm(sel.astype(BF16), ee_ref[...])

    def slc_tile(kt, first):
        k = ks_ref[kt * tk:(kt + 1) * tk, :].astype(BF16)
        v = vs_ref[kt * tk:(kt + 1) * tk, :].astype(BF16)
        sk = _nt(q4, k)
        mk = selx_ref[:, kt * tk:(kt + 1) * tk]
        mk4 = jnp.concatenate([mk] * NSA_HPG, axis=0)
        kpos = kt * tk + lax.broadcasted_iota(I32, (1, tk), 1)
        ok = (mk4 > 0.5) & (kpos <= t4)
        sk = jnp.where(ok, sk, NEG)
        mx = jnp.max(sk, axis=-1, keepdims=True)
        if first:
            pk = jnp.exp(sk - mx)
            m_ref[...] = mx
            l_ref[...] = jnp.sum(pk, axis=-1, keepdims=True)
            acc_ref[...] = _mm(pk.astype(BF16), v)
        else:
            m_old = m_ref[...]
            m_new = jnp.maximum(m_old, mx)
            alpha = jnp.exp(m_old - m_new)
            pk = jnp.exp(sk - m_new)
            m_ref[...] = m_new
            l_ref[...] = alpha * l_ref[...] + jnp.sum(pk, axis=-1, keepdims=True)
            acc_ref[...] = alpha * acc_ref[...] + _mm(pk.astype(BF16), v)

    slc_tile(0, True)
    for kt in range(1, t_len // tk):
        pl.when(kt * tk <= q0 + tq - 1)(functools.partial(slc_tile, kt, False))
    o_slc = acc_ref[...] / l_ref[...]

    span = WINDOW + tq
    start = pl.multiple_of(jnp.maximum(q0 - WINDOW, 0), LANES)
    kw = kw_ref[pl.ds(start, span), :].astype(BF16)
    vw = vw_ref[pl.ds(start, span), :].astype(BF16)
    sw = _nt(q4, kw)
    kpos = start + lax.broadcasted_iota(I32, (1, span), 1)
    okw = (kpos <= t4) & (kpos > t4 - WINDOW)
    o_win = _mm(_masked_softmax(sw, okw).astype(BF16), vw)

    gt = gate_ref[...]

    def gcol(kk):
        cols = []
        for h in range(NSA_HPG):
            c0 = gt[:, h * 3 + kk:h * 3 + kk + 1]
            c1 = gt[:, (NSA_HPG + h) * 3 + kk:(NSA_HPG + h) * 3 + kk + 1]
            cols.append(jnp.where(g == 0, c0, c1))
        return jnp.concatenate(cols, axis=0)

    o = gcol(0) * o_cmp + gcol(1) * o_slc + gcol(2) * o_win
    for h in range(NSA_HPG):
        o_ref[:, h * HD:(h + 1) * HD] = o[h * tq:(h + 1) * tq].astype(BF16)


def _cmp_to_slc(n_cmp, n_slc):
    cs = np.arange(n_cmp) * CMP_STRIDE
    ss = np.arange(n_slc) * SLC_BLOCK
    shared = (np.minimum(cs[:, None] + CMP_BLOCK, ss[None, :] + SLC_BLOCK)
              - np.maximum(cs[:, None], ss[None, :]))
    return np.clip(shared, 0, None) / CMP_STRIDE


def nsa_prompt(qn, ck, cv, kvs, kvw, gate, b, t):
    tq, tk = 128, 512
    nq = t // tq
    n_cmp = (t - CMP_BLOCK) // CMP_STRIDE + 1
    n_slc = -(-t // SLC_BLOCK)
    c2s = np.zeros((LANES, LANES), np.float32)
    c2s[:n_cmp, :n_slc] = _cmp_to_slc(n_cmp, n_slc)
    ee = (np.arange(LANES)[:, None] == (np.arange(t)[None, :] // SLC_BLOCK)).astype(np.float32)
    kern = functools.partial(_nsa_prompt_kernel, tq=tq, tk=tk, t_len=t, n_cmp=n_cmp, n_slc=n_slc)
    r = NSA_HPG * tq
    return pl.pallas_call(
        kern,
        out_shape=jax.ShapeDtypeStruct((b * t, NSA_G * NSA_HPG * HD), BF16),
        grid=(b, NSA_G, nq),
        in_specs=[
            pl.BlockSpec((tq, NSA_HPG * HD), lambda i, g, q: (i * nq + q, g)),
            pl.BlockSpec((1, 1, ck.shape[2], HD), lambda i, g, q: (i, g, 0, 0)),
            pl.BlockSpec((1, 1, cv.shape[2], HD), lambda i, g, q: (i, g, 0, 0)),
            pl.BlockSpec((t, HD), lambda i, g, q: (i, g)),
            pl.BlockSpec((t, HD), lambda i, g, q: (i, NSA_G + g)),
            pl.BlockSpec((t, HD), lambda i, g, q: (i, g)),
            pl.BlockSpec((t, HD), lambda i, g, q: (i, NSA_G + g)),
            pl.BlockSpec((tq, LANES), lambda i, g, q: (i * nq + q, 0)),
            pl.BlockSpec((LANES, LANES), lambda i, g, q: (0, 0)),
            pl.BlockSpec((LANES, t), lambda i, g, q: (0, 0)),
        ],
        out_specs=pl.BlockSpec((tq, NSA_HPG * HD), lambda i, g, q: (i * nq + q, g)),
        scratch_shapes=[pltpu.VMEM((tq, t), F32), pltpu.VMEM((r, 1), F32), pltpu.VMEM((r, 1), F32),
                        pltpu.VMEM((r, HD), F32)],
        compiler_params=_cp(("arbitrary", "arbitrary", "arbitrary")),
        name="nsa_prompt",
    )(qn, ck, cv, kvs, kvs, kvw, kvw, gate, jnp.asarray(c2s, BF16), jnp.asarray(ee, BF16))


def _moba_prompt_kernel(q_ref, k_ref, v_ref, o_ref, km_ref, *, t_len):
    qi = pl.program_id(2)
    nb = t_len // MOBA_BLOCK
    tq = MOBA_BLOCK

    @pl.when(qi == 0)
    def _():
        km_ref[...] = jnp.zeros(km_ref.shape, F32)
        for n in range(nb):
            blk = k_ref[n * MOBA_BLOCK:(n + 1) * MOBA_BLOCK, :]
            km_ref[n:n + 1, :] = jnp.sum(blk, axis=0, keepdims=True) * (1.0 / MOBA_BLOCK)

    qf = q_ref[...]
    qhi, qlo = _split(qf)
    khi, klo = _split(km_ref[...])
    gate = _nt(qhi, khi) + _nt(qhi, klo) + _nt(qlo, khi)
    lane = lax.broadcasted_iota(I32, (1, LANES), 1)
    gate = jnp.where(lane < qi, gate, NEG)
    sel = _select_blocks(gate, lane, nb, min(MOBA_TOPK, nb)) & (lane < qi)
    self32 = sel.astype(F32)

    qs = (qf * SCALE).astype(BF16)
    row = lax.broadcasted_iota(I32, (tq, 1), 0)
    col = lax.broadcasted_iota(I32, (1, tq), 1)
    d0 = pl.multiple_of(qi * MOBA_BLOCK, MOBA_BLOCK)
    kd = k_ref[pl.ds(d0, MOBA_BLOCK), :].astype(BF16)
    vd = v_ref[pl.ds(d0, MOBA_BLOCK), :].astype(BF16)
    s = jnp.where(col <= row, _nt(qs, kd), NEG)
    m0 = jnp.max(s, axis=-1, keepdims=True)
    p0 = jnp.exp(s - m0)
    l0 = jnp.sum(p0, axis=-1, keepdims=True)
    a0 = _mm(p0.astype(BF16), vd)

    def body(kt, carry):
        m_old, l_old, acc = carry
        k0 = pl.multiple_of(kt * MOBA_BLOCK, MOBA_BLOCK)
        k = k_ref[pl.ds(k0, MOBA_BLOCK), :].astype(BF16)
        v = v_ref[pl.ds(k0, MOBA_BLOCK), :].astype(BF16)
        on = jnp.sum(jnp.where(lane == kt, self32, 0.0), axis=-1, keepdims=True) > 0.5
        sk = jnp.where(on, _nt(qs, k), NEG)
        m_new = jnp.maximum(m_old, jnp.max(sk, axis=-1, keepdims=True))
        alpha = jnp.exp(m_old - m_new)
        pk = jnp.exp(sk - m_new)
        return (m_new, alpha * l_old + jnp.sum(pk, axis=-1, keepdims=True),
                alpha * acc + _mm(pk.astype(BF16), v))

    _, l_fin, acc = lax.fori_loop(0, qi, body, (m0, l0, a0))
    o_ref[...] = (acc / l_fin).astype(BF16)


def moba_prompt(qm, kvm, b, t):
    nq = t // MOBA_BLOCK
    kern = functools.partial(_moba_prompt_kernel, t_len=t)
    return pl.pallas_call(
        kern,
        out_shape=jax.ShapeDtypeStruct((b * t, MOBA_H * HD), BF16),
        grid=(b, MOBA_H, nq),
        in_specs=[
            pl.BlockSpec((MOBA_BLOCK, HD), lambda i, h, q: (i * nq + q, h)),
            pl.BlockSpec((t, HD), lambda i, h, q: (i, h)),
            pl.BlockSpec((t, HD), lambda i, h, q: (i, MOBA_H + h)),
        ],
        out_specs=pl.BlockSpec((MOBA_BLOCK, HD), lambda i, h, q: (i * nq + q, h)),
        scratch_shapes=[pltpu.VMEM((LANES, HD), F32)],
        compiler_params=_cp(("arbitrary", "arbitrary", "arbitrary")),
        name="moba_prompt",
    )(qm, kvm, kvm)


def _outproj_kernel(on_ref, om_ref, w1_ref, w2_ref, x_ref, g_ref, o_ref):
    acc = _mm(on_ref[...], w1_ref[...]) + _mm(om_ref[...], w2_ref[...])
    o_ref[...] = x_ref[...] + g_ref[0] * acc


def attn_out_proj(o_nsa, o_moba, w_out, x2d, gate, tm, tiles_per_mod):
    m, d = x2d.shape
    r = gate.shape[1]
    kn = o_nsa.shape[1]
    return pl.pallas_call(
        _outproj_kernel,
        out_shape=jax.ShapeDtypeStruct((m, d), F32),
        grid=(m // tm,),
        in_specs=[
            pl.BlockSpec((tm, kn), lambda i: (i, 0)),
            pl.BlockSpec((tm, kn), lambda i: (i, 0)),
            pl.BlockSpec((kn, d), lambda i: (0, 0)),
            pl.BlockSpec((kn, d), lambda i: (1, 0)),
            pl.BlockSpec((tm, d), lambda i: (i, 0)),
            pl.BlockSpec((1, r, d), lambda i: (i // tiles_per_mod, 0, 0)),
        ],
        out_specs=pl.BlockSpec((tm, d), lambda i: (i, 0)),
        compiler_params=_cp(("arbitrary",)),
        name="attn_out_proj",
    )(o_nsa, o_moba, w_out, w_out, x2d, gate)


_GROUP_LANE0 = 64


def _router_kernel(x_ref, g_ref, sh_ref, sc_ref, wh_ref, wl_ref, b_ref, h_ref, meta_ref, cnt_ref, carry_ref,
                   *, tm):
    i = pl.program_id(0)

    @pl.when(i == 0)
    def _():
        carry_ref[...] = jnp.zeros(carry_ref.shape, F32)

    h = _rms_mod(x_ref[...], g_ref[...], sh_ref[0], sc_ref[0])
    h_ref[...] = h
    hhi, hlo = _split(h)
    lg = _mm(hhi, wh_ref[...]) + _mm(hhi, wl_ref[...]) + _mm(hlo, wh_ref[...]) + b_ref[...]
    lane = lax.broadcasted_iota(I32, (1, LANES), 1)
    lanef = lane.astype(F32)
    big = 1.0e9

    isg = (lane >= _GROUP_LANE0) & (lane < _GROUP_LANE0 + N_GROUPS)
    mxg = jnp.max(jnp.where(isg, lg, NEG), axis=-1, keepdims=True)
    grp = jnp.min(jnp.where(isg & (lg == mxg), lanef - _GROUP_LANE0, big), axis=-1, keepdims=True)
    pg = 1.0 / jnp.sum(jnp.where(isg, jnp.exp(lg - mxg), 0.0), axis=-1, keepdims=True)

    ing = (lane < N_EXP) & ((lane >> 3).astype(F32) == grp)
    l1 = jnp.max(jnp.where(ing, lg, NEG), axis=-1, keepdims=True)
    i1 = jnp.min(jnp.where(ing & (lg == l1), lanef, big), axis=-1, keepdims=True)
    ing2 = ing & (lanef != i1)
    l2 = jnp.max(jnp.where(ing2, lg, NEG), axis=-1, keepdims=True)
    i2 = jnp.min(jnp.where(ing2 & (lg == l2), lanef, big), axis=-1, keepdims=True)
    e21 = jnp.exp(l2 - l1)
    w1 = pg / (1.0 + e21)
    w2 = pg * e21 / (1.0 + e21)

    oh1 = lanef == i1
    oh2 = lanef == i2
    oh = (oh1 | oh2).astype(F32)
    r_i = lax.broadcasted_iota(I32, (tm, tm), 0)
    c_i = lax.broadcasted_iota(I32, (tm, tm), 1)
    lower = (c_i < r_i).astype(BF16)
    pref = _mm(lower, oh.astype(BF16)) + carry_ref[0:1, :]
    r1 = jnp.sum(jnp.where(oh1, pref, 0.0), axis=-1, keepdims=True)
    r2 = jnp.sum(jnp.where(oh2, pref, 0.0), axis=-1, keepdims=True)
    carry_ref[0:1, :] = carry_ref[0:1, :] + jnp.sum(oh, axis=0, keepdims=True)
    cnt_ref[...] = carry_ref[...]

    meta = jnp.where(lane == 0, i1, 0.0) + jnp.where(lane == 1, i2, 0.0) + jnp.where(lane == 2, r1, 0.0) \
        + jnp.where(lane == 3, r2, 0.0) + jnp.where(lane == 4, w1, 0.0) + jnp.where(lane == 5, w2, 0.0)
    meta_ref[...] = meta


def moe_router(x2d, g, shift, scale, wr_hi, wr_lo, br, tm, tiles_per_mod):
    m, d = x2d.shape
    r = shift.shape[1]
    mod_spec = pl.BlockSpec((1, r, d), lambda i: (i // tiles_per_mod, 0, 0))
    kern = functools.partial(_router_kernel, tm=tm)
    return pl.pallas_call(
        kern,
        out_shape=[jax.ShapeDtypeStruct((m, d), F32), jax.ShapeDtypeStruct((m, LANES), F32),
                   jax.ShapeDtypeStruct((8, LANES), F32)],
        grid=(m // tm,),
        in_specs=[
            pl.BlockSpec((tm, d), lambda i: (i, 0)),
            pl.BlockSpec((1, d), lambda i: (0, 0)),
            mod_spec, mod_spec,
            pl.BlockSpec((d, LANES), lambda i: (0, 0)),
            pl.BlockSpec((d, LANES), lambda i: (0, 0)),
            pl.BlockSpec((1, LANES), lambda i: (0, 0)),
        ],
        out_specs=[pl.BlockSpec((tm, d), lambda i: (i, 0)), pl.BlockSpec((tm, LANES), lambda i: (i, 0)),
                   pl.BlockSpec((8, LANES), lambda i: (0, 0))],
        scratch_shapes=[pltpu.VMEM((8, LANES), F32)],
        compiler_params=_cp(("arbitrary",)),
        name="moe_router",
    )(x2d, g, shift, scale, wr_hi, wr_lo, br)


def _scatter_kernel(seg_ref, has_ref, nu_ref, dest_ref, h_ref, xs_ref, zbuf, sem, zsem, *, tm, n_tiles):
    i = pl.program_id(0)

    def zero_copy(row0):
        return pltpu.make_async_copy(zbuf, xs_ref.at[pl.ds(pl.multiple_of(row0, TE), TE), :], zsem)

    @pl.when(i == 0)
    def _():
        zbuf[...] = jnp.zeros(zbuf.shape, F32)
        for e in range(N_EXP):
            pl.when(has_ref[e] > 0)(lambda e=e: zero_copy(seg_ref[e]).start())

        def tail_start(tl, c):
            zero_copy(tl * TE).start()
            return c

        def tail_wait(tl, c):
            zero_copy(tl * TE).wait()
            return c

        lax.fori_loop(nu_ref[0], n_tiles, tail_start, 0)
        for e in range(N_EXP):
            pl.when(has_ref[e] > 0)(lambda e=e: zero_copy(seg_ref[e]).wait())
        lax.fori_loop(nu_ref[0], n_tiles, tail_wait, 0)

    def row_copy(r, d):
        return pltpu.make_async_copy(h_ref.at[pl.ds(r, 1), :], xs_ref.at[pl.ds(d, 1), :], sem)

    def start(r, c):
        row_copy(r, dest_ref[2 * r]).start()
        row_copy(r, dest_ref[2 * r + 1]).start()
        return c

    def wait(r, c):
        row_copy(r, dest_ref[2 * r]).wait()
        row_copy(r, dest_ref[2 * r + 1]).wait()
        return c

    lax.fori_loop(0, tm, start, 0)
    lax.fori_loop(0, tm, wait, 0)


def moe_scatter(h, dest_flat, seg_last, seg_has, n_used, n_rows, tm):
    m, d = h.shape
    kern = functools.partial(_scatter_kernel, tm=tm, n_tiles=n_rows // TE)
    gs = pltpu.PrefetchScalarGridSpec(
        num_scalar_prefetch=3,
        grid=(m // tm,),
        in_specs=[
            pl.BlockSpec((2 * tm,), lambda i, *_: (i,), memory_space=pltpu.SMEM),
            pl.BlockSpec((tm, d), lambda i, *_: (i, 0)),
        ],
        out_specs=pl.BlockSpec(memory_space=pl.ANY),
        scratch_shapes=[pltpu.VMEM((TE, d), F32), pltpu.SemaphoreType.DMA(()), pltpu.SemaphoreType.DMA(())],
    )
    return pl.pallas_call(
        kern,
        out_shape=jax.ShapeDtypeStruct((n_rows, d), F32),
        grid_spec=gs,
        compiler_params=_cp(("arbitrary",)),
        name="moe_scatter",
    )(seg_last, seg_has, n_used, dest_flat, h)


def _expert_kernel(te_ref, nu_ref, xs_ref, wg_ref, wu_ref, wd_ref, ys_ref):
    i = pl.program_id(0)

    @pl.when(i < nu_ref[0])
    def _():
        xb = xs_ref[...].astype(BF16)
        a = _mm(xb, wg_ref[0, 0].astype(BF16))
        u = _mm(xb, wu_ref[0, 0].astype(BF16))
        hid = (a * _sigmoid(a) * u).astype(BF16)
        ys_ref[...] = _mm(hid, wd_ref[0, 0].astype(BF16))

    @pl.when(i >= nu_ref[0])
    def _():
        ys_ref[...] = jnp.zeros(ys_ref.shape, F32)


def moe_experts(xs, tile_expert, n_used, w_gate, w_up, w_down, layer):
    n_rows, d = xs.shape
    f = w_gate.shape[-1]
    nt = n_rows // TE
    gs = pltpu.PrefetchScalarGridSpec(
        num_scalar_prefetch=2,
        grid=(nt,),
        in_specs=[
            pl.BlockSpec((TE, d), lambda i, te, nu: (jnp.minimum(i, nu[0] - 1), 0)),
            pl.BlockSpec((1, 1, d, f), lambda i, te, nu: (layer, te[i], 0, 0)),
            pl.BlockSpec((1, 1, d, f), lambda i, te, nu: (layer, te[i], 0, 0)),
            pl.BlockSpec((1, 1, f, d), lambda i, te, nu: (layer, te[i], 0, 0)),
        ],
        out_specs=pl.BlockSpec((TE, d), lambda i, te, nu: (i, 0)),
    )
    return pl.pallas_call(
        _expert_kernel,
        out_shape=jax.ShapeDtypeStruct((n_rows, d), F32),
        grid_spec=gs,
        compiler_params=_cp(("arbitrary",)),
        name="moe_experts",
    )(tile_expert, n_used, xs, w_gate, w_up, w_down)


def _combine_kernel(dest_ref, x_ref, g_ref, meta_ref, fg_ref, ys_ref, o_ref, rows, sem, *, tm, final):
    def row_copy(r, k):
        return pltpu.make_async_copy(ys_ref.at[pl.ds(dest_ref[2 * r + k], 1), :],
                                     rows.at[k, pl.ds(r, 1), :], sem)

    def start(r, c):
        row_copy(r, 0).start()
        row_copy(r, 1).start()
        return c

    def wait(r, c):
        row_copy(r, 0).wait()
        row_copy(r, 1).wait()
        return c

    lax.fori_loop(0, tm, start, 0)
    lax.fori_loop(0, tm, wait, 0)
    meta = meta_ref[...]
    y = meta[:, 4:5] * rows[0] + meta[:, 5:6] * rows[1]
    x = x_ref[...] + g_ref[0] * y
    if final:
        x = (x * lax.rsqrt(jnp.mean(x * x, axis=-1, keepdims=True) + EPS)) * fg_ref[...]
    o_ref[...] = x


def moe_combine(x2d, gate, meta, final_g, ys, dest_flat, tm, tiles_per_mod, final):
    m, d = x2d.shape
    r = gate.shape[1]
    kern = functools.partial(_combine_kernel, tm=tm, final=final)
    return pl.pallas_call(
        kern,
        out_shape=jax.ShapeDtypeStruct((m, d), F32),
        grid=(m // tm,),
        in_specs=[
            pl.BlockSpec((2 * tm,), lambda i: (i,), memory_space=pltpu.SMEM),
            pl.BlockSpec((tm, d), lambda i: (i, 0)),
            pl.BlockSpec((1, r, d), lambda i: (i // tiles_per_mod, 0, 0)),
            pl.BlockSpec((tm, LANES), lambda i: (i, 0)),
            pl.BlockSpec((1, d), lambda i: (0, 0)),
            pl.BlockSpec(memory_space=pl.ANY),
        ],
        out_specs=pl.BlockSpec((tm, d), lambda i: (i, 0)),
        scratch_shapes=[pltpu.VMEM((2, tm, d), F32), pltpu.SemaphoreType.DMA(())],
        compiler_params=_cp(("arbitrary",)),
        name="moe_combine",
    )(dest_flat, x2d, gate, meta, final_g, ys)


def hier_moe_block(x2d, g_ffn, shift, scale, gate, p_layer, layer, tm, tiles_per_mod, final_g, final):
    m, d = x2d.shape
    wr_hi, wr_lo, br, w_gate, w_up, w_down = p_layer
    h, meta, cnt = moe_router(x2d, g_ffn, shift, scale, wr_hi, wr_lo, br, tm, tiles_per_mod)
    counts = cnt[0, :N_EXP].astype(I32)
    padded = ((counts + TE - 1) // TE) * TE
    ends = jnp.cumsum(padded)
    offs = ends - padded
    eid = meta[:, 0:2].astype(I32)
    dest = (offs[eid] + meta[:, 2:4].astype(I32)).reshape(-1)
    n_tiles = (2 * m) // TE + N_EXP
    n_used = (ends[-1] // TE).astype(I32).reshape(1)
    tile_start = jnp.arange(n_tiles, dtype=I32) * TE
    tile_expert = jnp.minimum(jnp.searchsorted(ends, jnp.minimum(tile_start, ends[-1] - 1), side="right"),
                              N_EXP - 1).astype(I32)
    seg_last = jnp.maximum(ends - TE, 0).astype(I32)
    seg_has = (counts > 0).astype(I32)
    xs = moe_scatter(h, dest, seg_last, seg_has, n_used, n_tiles * TE, tm)
    ys = moe_experts(xs, tile_expert, n_used, w_gate, w_up, w_down, layer)
    return moe_combine(x2d, gate, meta, final_g, ys, dest, tm, tiles_per_mod, final)


def _pw1_kernel(x_ref, g_ref, sh_ref, sc_ref, wa_ref, wb_ref, ba_ref, bb_ref, u_ref, h_scr):
    @pl.when(pl.program_id(1) == 0)
    def _():
        h_scr[...] = _rms_mod(x_ref[...], g_ref[...], sh_ref[0], sc_ref[0]).astype(BF16)

    h = h_scr[...]
    a = _mm(h, wa_ref[...]) + ba_ref[...]
    b = _mm(h, wb_ref[...]) + bb_ref[...]
    u_ref[...] = a * _sigmoid(b)


def conv_pw1_glu(x2d, g, shift, scale, w_pw1, b_pw1, tm, tiles_per_mod):
    m, d = x2d.shape
    r = shift.shape[1]
    dc = w_pw1.shape[1] // 2
    tn = 512 if dc % 512 == 0 else 256
    nj = dc // tn
    mod_spec = pl.BlockSpec((1, r, d), lambda i, j: (i // tiles_per_mod, 0, 0))
    return pl.pallas_call(
        _pw1_kernel,
        out_shape=jax.ShapeDtypeStruct((m, dc), F32),
        grid=(m // tm, nj),
        in_specs=[
            pl.BlockSpec((tm, d), lambda i, j: (i, 0)),
            pl.BlockSpec((1, d), lambda i, j: (0, 0)),
            mod_spec, mod_spec,
            pl.BlockSpec((d, tn), lambda i, j: (0, j)),
            pl.BlockSpec((d, tn), lambda i, j: (0, nj + j)),
            pl.BlockSpec((1, tn), lambda i, j: (0, j)),
            pl.BlockSpec((1, tn), lambda i, j: (0, nj + j)),
        ],
        out_specs=pl.BlockSpec((tm, tn), lambda i, j: (i, j)),
        scratch_shapes=[pltpu.VMEM((tm, d), BF16)],
        compiler_params=_cp(("arbitrary", "arbitrary")),
        name="conv_pw1_glu",
    )(x2d, g, shift, scale, w_pw1, w_pw1, b_pw1, b_pw1)


_TAIL = 32


def _ln_silu(y, g, b):
    yc = y - jnp.mean(y, axis=-1, keepdims=True)
    z = yc * lax.rsqrt(jnp.mean(yc * yc, axis=-1, keepdims=True) + EPS)
    z = z * g + b
    return z * _sigmoid(z)


def _conv_prompt_kernel(u_ref, wdw_ref, bdw_ref, lg_ref, lb_ref, w2_ref, b2_ref, x_ref, g_ref, o_ref, buf,
                        *, tt):
    ti = pl.program_id(1)

    @pl.when(ti == 0)
    def _():
        buf[0:_TAIL, :] = jnp.zeros((_TAIL, buf.shape[1]), F32)

    buf[_TAIL:_TAIL + tt, :] = u_ref[...]
    off = _TAIL - (CONV_W - 1)
    acc = wdw_ref[0:1, :] * buf[off:off + tt, :]
    for k in range(1, CONV_W):
        acc = acc + wdw_ref[k:k + 1, :] * buf[off + k:off + k + tt, :]
    tail = buf[tt:tt + _TAIL, :]
    buf[0:_TAIL, :] = tail
    z = _ln_silu(acc + bdw_ref[...], lg_ref[...], lb_ref[...]).astype(BF16)
    out = _mm(z, w2_ref[...]) + b2_ref[...]
    o_ref[...] = x_ref[...] + g_ref[0] * out


def conv_prompt(u, w_dw, b_dw, ln_g, ln_b, w_pw2, b_pw2, x2d, gate, b, t):
    m, d = x2d.shape
    dc = u.shape[1]
    tt = 256
    nt = t // tt
    kern = functools.partial(_conv_prompt_kernel, tt=tt)
    vec = lambda n: pl.BlockSpec((1, n), lambda i, j: (0, 0))
    return pl.pallas_call(
        kern,
        out_shape=jax.ShapeDtypeStruct((m, d), F32),
        grid=(b, nt),
        in_specs=[
            pl.BlockSpec((tt, dc), lambda i, j: (i * nt + j, 0)),
            pl.BlockSpec((_TAIL, dc), lambda i, j: (0, 0)),
            vec(dc), vec(dc), vec(dc),
            pl.BlockSpec((dc, d), lambda i, j: (0, 0)),
            vec(d),
            pl.BlockSpec((tt, d), lambda i, j: (i * nt + j, 0)),
            pl.BlockSpec((1, 1, d), lambda i, j: (i, 0, 0)),
        ],
        out_specs=pl.BlockSpec((tt, d), lambda i, j: (i * nt + j, 0)),
        scratch_shapes=[pltpu.VMEM((_TAIL + tt, dc), F32)],
        compiler_params=_cp(("arbitrary", "arbitrary")),
        name="conv_prompt",
    )(u, w_dw, b_dw, ln_g, ln_b, w_pw2, b_pw2, x2d, gate)


def _prep_params(p):
    d = p["attn_w_in"].shape[1]
    nsa_q = NSA_G * NSA_HPG * HD
    nsa_kv = 2 * NSA_G * HD
    gate_w = 3 * NSA_G * NSA_HPG
    c_gate = nsa_q + 3 * nsa_kv
    w_in = p["attn_w_in"][0]
    w_main = jnp.concatenate([w_in[:, :c_gate], w_in[:, c_gate + gate_w:]], axis=1).astype(BF16)
    w_gate = jnp.pad(w_in[:, c_gate:c_gate + gate_w], ((0, 0), (0, LANES - gate_w))).astype(BF16)
    w1 = p["nsa_cmp_w1"][0]
    wab = jnp.concatenate([w1[:, :CMP_STRIDE].reshape(2, CMP_STRIDE * HD, HD),
                           w1[:, CMP_STRIDE:].reshape(2, CMP_STRIDE * HD, HD)], axis=2).astype(BF16)
    pe = p["nsa_cmp_pe"][0].reshape(2, 2, CMP_STRIDE * HD)
    pe2 = jnp.pad(pe, ((0, 0), (0, 14), (0, 0))).astype(BF16)
    w2 = p["nsa_cmp_w2"][0].astype(BF16)
    moe = []
    for layer in range(p["moe_wg"].shape[0]):
        wr = jnp.zeros((d, LANES), F32)
        wr = wr.at[:, :N_EXP].set(p["moe_we"][layer]).at[:, _GROUP_LANE0:_GROUP_LANE0 + N_GROUPS].set(
            p["moe_wg"][layer])
        br = jnp.zeros((1, LANES), F32)
        br = br.at[0, :N_EXP].set(p["moe_be"][layer]).at[0, _GROUP_LANE0:_GROUP_LANE0 + N_GROUPS].set(
            p["moe_bg"][layer])
        hi, lo = _split(wr)
        moe.append((hi, lo, br, p["moe_w_gate"], p["moe_w_up"], p["moe_w_down"]))
    w_dw = jnp.pad(p["conv_w_dw"][0], ((0, _TAIL - CONV_W), (0, 0)))
    return dict(
        w_main=w_main, w_gate=w_gate, wab=wab, pe2=pe2, w2=w2,
        w_out=p["attn_w_out"][0].astype(BF16), moe=moe,
        w_pw1=p["conv_w_pw1"][0].astype(BF16), b_pw1=p["conv_b_pw1"][0][None, :],
        w_dw=w_dw, b_dw=p["conv_b_dw"][0][None, :], ln_g=p["conv_ln_g"][0][None, :],
        ln_b=p["conv_ln_b"][0][None, :], w_pw2=p["conv_w_pw2"][0].astype(BF16),
        b_pw2=p["conv_b_pw2"][0][None, :],
        norm_mix_g=p["norm_mix_g"], norm_ffn_g=p["norm_ffn_g"], final_g=p["final_norm_g"][None, :],
    )


def _trunk_prompt(x, mods, pp):
    b, t, d = x.shape
    m = b * t
    x2d = x.reshape(m, d)
    sh1, sc1, g1, sh2, sc2, g2 = mods[0]
    tm = 512
    qn, kvc, kvs, kvw, qm, kvm, gate = attn_in_proj(
        x2d, pp["norm_mix_g"][0:1], sh1, sc1, pp["w_main"], pp["w_gate"], tm, t // tm)
    ck, cv = compress_prompt(kvc, b, t, pp["wab"], pp["pe2"], pp["w2"])
    o_nsa = nsa_prompt(qn, ck, cv, kvs, kvw, gate, b, t)
    o_moba = moba_prompt(qm, kvm, b, t)
    tm2 = 256
    x1 = attn_out_proj(o_nsa, o_moba, pp["w_out"], x2d, g1, tm2, t // tm2)
    x2 = hier_moe_block(x1, pp["norm_ffn_g"][0:1], sh2, sc2, g2, pp["moe"][0], 0, tm2, t // tm2,
                        pp["final_g"], False)
    sh1, sc1, g1, sh2, sc2, g2 = mods[1]
    u = conv_pw1_glu(x2, pp["norm_mix_g"][1:2], sh1, sc1, pp["w_pw1"], pp["b_pw1"], tm, t // tm)
    x3 = conv_prompt(u, pp["w_dw"], pp["b_dw"], pp["ln_g"], pp["ln_b"], pp["w_pw2"], pp["b_pw2"], x2, g1, b, t)
    y = hier_moe_block(x3, pp["norm_ffn_g"][1:2], sh2, sc2, g2, pp["moe"][1], 1, tm2, t // tm2,
                       pp["final_g"], True)
    wlen = min(WINDOW, t)
    state = (
        kvc.reshape(1, b, t, 2, NSA_G, HD), kvs.reshape(1, b, t, 2, NSA_G, HD),
        kvw.reshape(b, t, 2, NSA_G, HD)[None, :, t - wlen:], kvm.reshape(1, b, t, 2, MOBA_H, HD),
        u.reshape(b, t, -1)[None, :, t - (CONV_W - 1):],
    )
    return y.reshape(b, t, d), state


TPAD = 8


def _page_specs(npg, n_pages, block, col_block):
    nd = len(block)

    def spec(i):
        def imap(b, j, pt):
            return (pt[b * n_pages + j * npg + i],) + (0,) * (nd - 2) + (col_block,)
        return pl.BlockSpec(block, imap)

    return [spec(i) for i in range(npg)]


def _compress_sample_kernel(pt_ref, *refs, npg, nsteps):
    pages = refs[:npg]
    wab_ref, pe_ref, w2_ref, ck_ref, cv_ref, ab_scr = refs[npg:]
    j = pl.program_id(1)
    rows = npg * (PAGE // CMP_STRIDE)
    r0 = pl.multiple_of(j * rows, rows)
    for c in range(2):
        for g in range(NSA_G):
            col = (c * NSA_G + g) * HD
            xs = jnp.concatenate(
                [jnp.concatenate([pg[0, :, r * 4 * HD + col:r * 4 * HD + col + HD] for r in range(CMP_STRIDE)],
                                 axis=1) for pg in pages], axis=0).astype(BF16)
            ab_scr[c * NSA_G + g, pl.ds(r0, rows), :] = _mm(xs, wab_ref[c])

    @pl.when(j == nsteps - 1)
    def _():
        for c, ref in ((0, ck_ref), (1, cv_ref)):
            for g in range(NSA_G):
                ref[0, g] = _compress_finish(ab_scr[c * NSA_G + g], c, pe_ref, wab_ref, w2_ref).astype(BF16)


def compress_sample(pt, cache, bs, n_pages, wab, pe2, w2):
    npg = 16
    nsteps = n_pages // npg
    gpp = PAGE // CMP_STRIDE
    ng = n_pages * gpp
    x = cache.reshape(cache.shape[1], gpp, CMP_STRIDE * 4 * HD)
    kern = functools.partial(_compress_sample_kernel, npg=npg, nsteps=nsteps)
    out = jax.ShapeDtypeStruct((bs, NSA_G, ng, HD), BF16)
    ospec = pl.BlockSpec((1, NSA_G, ng, HD), lambda b, j, pt: (b, 0, 0, 0))
    const = lambda a: pl.BlockSpec(a.shape, lambda b, j, pt: (0,) * a.ndim)
    gs = pltpu.PrefetchScalarGridSpec(
        num_scalar_prefetch=1,
        grid=(bs, nsteps),
        in_specs=_page_specs(npg, n_pages, (1, gpp, CMP_STRIDE * 4 * HD), 0) + [const(wab), const(pe2), const(w2)],
        out_specs=[ospec, ospec],
        scratch_shapes=[pltpu.VMEM((2 * NSA_G, ng, 2 * HD), F32)],
    )
    return pl.pallas_call(
        kern, out_shape=[out, out], grid_spec=gs,
        compiler_params=_cp(("arbitrary", "arbitrary")), name="nsa_compress_sample",
    )(pt, *([x] * npg), wab, pe2, w2)


def _nsa_sample_kernel(pt_ref, q_ref, ck_ref, cv_ref, ksn_ref, kwn_ref, win_ref, gate_ref, c2s_ref, *refs,
                       npg, nsteps, past, ts, n_cmp, n_slc):
    pages = refs[:npg]
    o_ref, selx_scr, m_scr, l_scr, acc_scr, part_scr = refs[npg:]
    j = pl.program_id(1)
    qr = NSA_HPG * TPAD
    keys = npg * PAGE
    wbuf = win_ref.shape[1]
    t_row = past + (lax.broadcasted_iota(I32, (qr, 1), 0) & (TPAD - 1))
    lane = lax.broadcasted_iota(I32, (1, LANES), 1)

    @pl.when(j == 0)
    def _():
        t8 = past + lax.broadcasted_iota(I32, (TPAD, 1), 0)
        for g in range(NSA_G):
            q = (q_ref[0, g] * SCALE).astype(BF16)
            gt = gate_ref[0, g]
            ncg = ck_ref.shape[2]
            lane_c = lax.broadcasted_iota(I32, (1, ncg), 1)
            valid = (lane_c * CMP_STRIDE + (CMP_BLOCK - 1) <= t_row) & (lane_c < n_cmp)
            p = _masked_softmax(_nt(q, ck_ref[0, g]), valid)
            o_cmp = _mm(p.astype(BF16), cv_ref[0, g])
            psum = p[0:TPAD] + p[TPAD:2 * TPAD] + p[2 * TPAD:3 * TPAD] + p[3 * TPAD:4 * TPAD]
            phi, plo = _split(psum)
            imp = _mm(phi, c2s_ref[...]) + _mm(plo, c2s_ref[...])
            nl = c2s_ref.shape[1]
            lane_s = lax.broadcasted_iota(I32, (1, nl), 1)
            qb = t8 >> 6
            forced = (lane_s == 0) | (lane_s == qb) | (lane_s == qb - 1)
            score = jnp.where(forced, FORCED, imp)
            score = jnp.where(lane_s * SLC_BLOCK <= t8, score, -1.0)
            score = jnp.where(lane_s < n_slc, score, -2.0)
            sel = _select_blocks(score, lane_s, n_slc, min(SLC_TOPN, n_slc)).astype(BF16)
            bps = keys // SLC_BLOCK
            for jj in range(nsteps):
                blk = lax.broadcasted_iota(I32, (nl, keys), 0)
                key = lax.broadcasted_iota(I32, (nl, keys), 1)
                ee = (blk == jj * bps + (key >> 6)).astype(BF16)
                selx_scr[g, jj] = _mm(sel, ee)
            own = jnp.sum(jnp.where(lane_s == (past >> 6), sel.astype(F32), 0.0), axis=-1, keepdims=True)
            own4 = jnp.concatenate([own] * NSA_HPG, axis=0) > 0.5
            zpad = jnp.zeros((LANES - TPAD, HD), F32)
            kw = jnp.concatenate([win_ref[0, :, g * HD:(g + 1) * HD], kwn_ref[0, :, g * HD:(g + 1) * HD], zpad],
                                 axis=0).astype(BF16)
            vw = jnp.concatenate([win_ref[0, :, (NSA_G + g) * HD:(NSA_G + g + 1) * HD],
                                  kwn_ref[0, :, (NSA_G + g) * HD:(NSA_G + g + 1) * HD], zpad], axis=0).astype(BF16)
            idx = lax.broadcasted_iota(I32, (1, wbuf + LANES), 1)
            kpos = past - wbuf + idx
            okw = (idx < wbuf + ts) & (kpos <= t_row) & (kpos > t_row - WINDOW)
            o_win = _mm(_masked_softmax(_nt(q, kw), okw).astype(BF16), vw)
            part_scr[g] = gt[:, 0:1] * o_cmp + gt[:, 2:3] * o_win
            kn = jnp.concatenate([ksn_ref[0, :, g * HD:(g + 1) * HD], zpad], axis=0).astype(BF16)
            vn = jnp.concatenate([ksn_ref[0, :, (NSA_G + g) * HD:(NSA_G + g + 1) * HD], zpad], axis=0).astype(BF16)
            ok0 = (past + lane <= t_row) & (lane < ts) & own4
            s0 = jnp.where(ok0, _nt(q, kn), NEG)
            m0 = jnp.max(s0, axis=-1, keepdims=True)
            p0 = jnp.where(ok0, jnp.exp(s0 - m0), 0.0)
            m_scr[g] = m0
            l_scr[g] = jnp.sum(p0, axis=-1, keepdims=True)
            acc_scr[g] = _mm(p0.astype(BF16), vn)

    for g in range(NSA_G):
        q = (q_ref[0, g] * SCALE).astype(BF16)
        k = jnp.concatenate([pg[0, :, g * HD:(g + 1) * HD] for pg in pages], axis=0).astype(BF16)
        v = jnp.concatenate([pg[0, :, (NSA_G + g) * HD:(NSA_G + g + 1) * HD] for pg in pages],
                            axis=0).astype(BF16)
        mk = selx_scr[g, j]
        ok = jnp.concatenate([mk] * NSA_HPG, axis=0) > 0.5
        s = jnp.where(ok, _nt(q, k), NEG)
        m_old = m_scr[g]
        m_new = jnp.maximum(m_old, jnp.max(s, axis=-1, keepdims=True))
        alpha = jnp.exp(m_old - m_new)
        pk = jnp.where(ok, jnp.exp(s - m_new), 0.0)
        m_scr[g] = m_new
        l_scr[g] = alpha * l_scr[g] + jnp.sum(pk, axis=-1, keepdims=True)
        acc_scr[g] = alpha * acc_scr[g] + _mm(pk.astype(BF16), v)

    @pl.when(j == nsteps - 1)
    def _():
        for g in range(NSA_G):
            o_slc = acc_scr[g] / jnp.maximum(l_scr[g], 1e-30)
            o_ref[0, g] = (part_scr[g] + gate_ref[0, g][:, 1:2] * o_slc).astype(BF16)


def nsa_sample(pt, q_g, ck, cv, ksn, kwn, win, gate_g, cache, bs, n_pages, ts):
    npg = 8
    nsteps = n_pages // npg
    past = n_pages * PAGE
    n_cmp = (past + ts - CMP_BLOCK) // CMP_STRIDE + 1
    n_slc = -(-(past + ts) // SLC_BLOCK)
    ncg = ck.shape[2]
    nl = -(-n_slc // LANES) * LANES
    c2s = np.zeros((ncg, nl), np.float32)
    c2s[:n_cmp, :n_slc] = _cmp_to_slc(n_cmp, n_slc)
    c2s = jnp.asarray(c2s, BF16)
    qr = NSA_HPG * TPAD
    keys = npg * PAGE
    x = cache.reshape(cache.shape[1], PAGE, 4 * HD)
    kern = functools.partial(_nsa_sample_kernel, npg=npg, nsteps=nsteps, past=past, ts=ts, n_cmp=n_cmp,
                             n_slc=n_slc)

    def per_b(a):
        return pl.BlockSpec((1,) + a.shape[1:], lambda b, j, pt: (b,) + (0,) * (a.ndim - 1))

    gs = pltpu.PrefetchScalarGridSpec(
        num_scalar_prefetch=1,
        grid=(bs, nsteps),
        in_specs=[per_b(q_g), per_b(ck), per_b(cv), per_b(ksn), per_b(kwn), per_b(win), per_b(gate_g),
                  pl.BlockSpec(c2s.shape, lambda b, j, pt: (0, 0))]
        + _page_specs(npg, n_pages, (1, PAGE, 4 * HD), 0),
        out_specs=pl.BlockSpec((1, NSA_G, qr, HD), lambda b, j, pt: (b, 0, 0, 0)),
        scratch_shapes=[pltpu.VMEM((NSA_G, nsteps, TPAD, keys), F32), pltpu.VMEM((NSA_G, qr, 1), F32),
                        pltpu.VMEM((NSA_G, qr, 1), F32), pltpu.VMEM((NSA_G, qr, HD), F32),
                        pltpu.VMEM((NSA_G, qr, HD), F32)],
    )
    return pl.pallas_call(
        kern, out_shape=jax.ShapeDtypeStruct((bs, NSA_G, qr, HD), BF16), grid_spec=gs,
        compiler_params=_cp(("arbitrary", "arbitrary")), name="nsa_sample",
    )(pt, q_g, ck, cv, ksn, kwn, win, gate_g, c2s, *([x] * npg))


def _moba_gate_kernel(pt_ref, q_ref, *refs, npg, nsteps, nb_past):
    pages = refs[:npg]
    sel_ref, km_scr = refs[npg:]
    j = pl.program_id(1)
    ppb = MOBA_BLOCK // PAGE
    bps = npg // ppb

    @pl.when(j == 0)
    def _():
        km_scr[...] = jnp.zeros(km_scr.shape, F32)

    for i in range(bps):
        ssum = jnp.sum(pages[ppb * i][0], axis=0, keepdims=True)
        for pp_ in range(1, ppb):
            ssum = ssum + jnp.sum(pages[ppb * i + pp_][0], axis=0, keepdims=True)
        km_scr[pl.ds(j * bps + i, 1), :] = ssum * (1.0 / MOBA_BLOCK)

    @pl.when(j == nsteps - 1)
    def _():
        lane = lax.broadcasted_iota(I32, (1, LANES), 1)
        for h in range(MOBA_H):
            qhi, qlo = _split(q_ref[0, h])
            khi, klo = _split(km_scr[:, h * HD:(h + 1) * HD])
            gate = _nt(qhi, khi) + _nt(qhi, klo) + _nt(qlo, khi)
            gate = jnp.where(lane < nb_past, gate, NEG)
            sel = _select_blocks(gate, lane, nb_past + 1, min(MOBA_TOPK, nb_past + 1)) & (lane < nb_past)
            sel_ref[0, h] = sel.astype(F32)


def moba_gate_sample(pt, q_h, cache, bs, n_pages):
    npg = 8
    nsteps = n_pages // npg
    hk = MOBA_H * HD
    x = cache.reshape(cache.shape[1], PAGE, 2 * hk)
    kern = functools.partial(_moba_gate_kernel, npg=npg, nsteps=nsteps, nb_past=n_pages * PAGE // MOBA_BLOCK)
    gs = pltpu.PrefetchScalarGridSpec(
        num_scalar_prefetch=1,
        grid=(bs, nsteps),
        in_specs=[pl.BlockSpec((1, MOBA_H, TPAD, HD), lambda b, j, pt: (b, 0, 0, 0))]
        + _page_specs(npg, n_pages, (1, PAGE, hk), 0),
        out_specs=pl.BlockSpec((1, MOBA_H, TPAD, LANES), lambda b, j, pt: (b, 0, 0, 0)),
        scratch_shapes=[pltpu.VMEM((LANES, hk), F32)],
    )
    return pl.pallas_call(
        kern, out_shape=jax.ShapeDtypeStruct((bs, MOBA_H, TPAD, LANES), F32), grid_spec=gs,
        compiler_params=_cp(("arbitrary", "arbitrary")), name="moba_gate_sample",
    )(pt, q_h, *([x] * npg))


def _moba_sample_kernel(pt_ref, qbd_ref, sel_ref, kvn_ref, *refs, npg, nsteps, ts):
    pages = refs[:npg]
    o_ref, m_scr, l_scr, acc_scr = refs[npg:]
    j = pl.program_id(1)
    hk = MOBA_H * HD
    qr = MOBA_H * TPAD
    qbd = qbd_ref[0]
    lane = lax.broadcasted_iota(I32, (1, LANES), 1)

    def diag(o_all):
        return jnp.concatenate([o_all[h * TPAD:(h + 1) * TPAD, h * HD:(h + 1) * HD] for h in range(MOBA_H)],
                               axis=0)

    @pl.when(j == 0)
    def _():
        zpad = jnp.zeros((LANES - TPAD, hk), F32)
        kn = jnp.concatenate([kvn_ref[0, :, :hk], zpad], axis=0).astype(BF16)
        vn = jnp.concatenate([kvn_ref[0, :, hk:], zpad], axis=0).astype(BF16)
        t8 = lax.broadcasted_iota(I32, (qr, 1), 0) & (TPAD - 1)
        ok0 = (lane <= t8) & (lane < ts)
        s0 = jnp.where(ok0, _nt(qbd, kn), NEG)
        m0 = jnp.max(s0, axis=-1, keepdims=True)
        p0 = jnp.where(ok0, jnp.exp(s0 - m0), 0.0)
        m_scr[...] = m0
        l_scr[...] = jnp.sum(p0, axis=-1, keepdims=True)
        acc_scr[...] = diag(_mm(p0.astype(BF16), vn))

    k = jnp.concatenate([pg[0, :, :hk] for pg in pages], axis=0).astype(BF16)
    v = jnp.concatenate([pg[0, :, hk:] for pg in pages], axis=0).astype(BF16)
    s = _nt(qbd, k)
    sel = sel_ref[0]
    bps = npg * PAGE // MOBA_BLOCK
    cols = []
    for bb in range(bps):
        on = jnp.sum(jnp.where(lane == j * bps + bb, sel, 0.0), axis=-1, keepdims=True)
        cols.append(jnp.broadcast_to(on, (qr, MOBA_BLOCK)))
    ok = jnp.concatenate(cols, axis=1) > 0.5
    s = jnp.where(ok, s, NEG)
    m_old = m_scr[...]
    m_new = jnp.maximum(m_old, jnp.max(s, axis=-1, keepdims=True))
    alpha = jnp.exp(m_old - m_new)
    pk = jnp.where(ok, jnp.exp(s - m_new), 0.0)
    m_scr[...] = m_new
    l_scr[...] = alpha * l_scr[...] + jnp.sum(pk, axis=-1, keepdims=True)
    acc_scr[...] = alpha * acc_scr[...] + diag(_mm(pk.astype(BF16), v))

    @pl.when(j == nsteps - 1)
    def _():
        o_ref[0] = (acc_scr[...] / l_scr[...]).astype(BF16)


def moba_sample(pt, q_bd, sel, kvn, cache, bs, n_pages, ts):
    npg = 8
    nsteps = n_pages // npg
    hk = MOBA_H * HD
    qr = MOBA_H * TPAD
    x = cache.reshape(cache.shape[1], PAGE, 2 * hk)
    kern = functools.partial(_moba_sample_kernel, npg=npg, nsteps=nsteps, ts=ts)

    def per_b(a):
        return pl.BlockSpec((1,) + a.shape[1:], lambda b, j, pt: (b,) + (0,) * (a.ndim - 1))

    gs = pltpu.PrefetchScalarGridSpec(
        num_scalar_prefetch=1,
        grid=(bs, nsteps),
        in_specs=[per_b(q_bd), per_b(sel), per_b(kvn)] + _page_specs(npg, n_pages, (1, PAGE, 2 * hk), 0),
        out_specs=pl.BlockSpec((1, qr, HD), lambda b, j, pt: (b, 0, 0)),
        scratch_shapes=[pltpu.VMEM((qr, 1), F32), pltpu.VMEM((qr, 1), F32), pltpu.VMEM((qr, HD), F32)],
    )
    return pl.pallas_call(
        kern, out_shape=jax.ShapeDtypeStruct((bs, qr, HD), BF16), grid_spec=gs,
        compiler_params=_cp(("arbitrary", "arbitrary")), name="moba_sample",
    )(pt, q_bd, sel, kvn, *([x] * npg))


def _conv_sample_kernel(st_ref, u_ref, wdw_ref, bdw_ref, lg_ref, lb_ref, w2_ref, b2_ref, x_ref, g_ref, o_ref,
                        *, ts):
    nst = CONV_W - 1
    bs = st_ref.shape[1]
    zs = []
    for t in range(ts):
        acc = None
        for k in range(CONV_W):
            r = t + k
            row = st_ref[r] if r < nst else u_ref[r - nst]
            term = wdw_ref[k:k + 1, :] * row
            acc = term if acc is None else acc + term
        zs.append(_ln_silu(acc + bdw_ref[...], lg_ref[...], lb_ref[...]))
    z = jnp.concatenate(zs, axis=0).astype(BF16)
    out = _mm(z, w2_ref[...]) + b2_ref[...]
    for t in range(ts):
        o_ref[t] = x_ref[t] + g_ref[t] * out[t * bs:(t + 1) * bs]


def conv_sample(st_t, u_t, w_dw, b_dw, ln_g, ln_b, w_pw2, b_pw2, x_t, g_t):
    ts = u_t.shape[0]
    kern = functools.partial(_conv_sample_kernel, ts=ts)
    full = lambda a: pl.BlockSpec(a.shape, lambda i: (0,) * a.ndim)
    args = (st_t, u_t, w_dw, b_dw, ln_g, ln_b, w_pw2, b_pw2, x_t, g_t)
    return pl.pallas_call(
        kern, out_shape=jax.ShapeDtypeStruct(x_t.shape, F32), grid=(1,),
        in_specs=[full(a) for a in args], out_specs=full(x_t),
        compiler_params=_cp(("arbitrary",)), name="conv_sample",
    )(*args)


def _trunk_sample(x, mods, pp, past):
    bs, ts, d = x.shape
    m = bs * ts
    n_pages = past["page_table"].shape[1]
    plen = n_pages * PAGE
    assert ts <= TPAD and ts < CMP_STRIDE and plen % MOBA_BLOCK == 0 and m % 8 == 0
    pt = past["page_table"].reshape(-1).astype(I32)
    x2d = x.reshape(m, d)
    sh1, sc1, g1, sh2, sc2, g2 = mods[0]
    qn, kvc, kvs, kvw, qm, kvm, gate = attn_in_proj(
        x2d, pp["norm_mix_g"][0:1], sh1, sc1, pp["w_main"], pp["w_gate"], m, 1)

    def pad_t(a, axis):
        w = [(0, 0)] * a.ndim
        w[axis] = (0, TPAD - ts)
        return jnp.pad(a, w)

    ck, cv = compress_sample(pt, past["nsa_cmp"], bs, n_pages, pp["wab"], pp["pe2"], pp["w2"])
    q_g = pad_t(qn.reshape(bs, ts, NSA_G, NSA_HPG, HD).transpose(0, 2, 3, 1, 4), 3)
    q_g = q_g.reshape(bs, NSA_G, NSA_HPG * TPAD, HD)
    gate_g = pad_t(gate[:, :3 * NSA_G * NSA_HPG].reshape(bs, ts, NSA_G, NSA_HPG, 3).transpose(0, 2, 3, 1, 4), 3)
    gate_g = jnp.pad(gate_g.reshape(bs, NSA_G, NSA_HPG * TPAD, 3), ((0, 0), (0, 0), (0, 0), (0, LANES - 3)))
    ksn = pad_t(kvs.reshape(bs, ts, 4 * HD), 1)
    kwn = pad_t(kvw.reshape(bs, ts, 4 * HD), 1)
    win = past["nsa_win"].reshape(bs, -1, 4 * HD)
    o_g = nsa_sample(pt, q_g, ck, cv, ksn, kwn, win, gate_g, past["nsa_slc"], bs, n_pages, ts)
    o_nsa = o_g.reshape(bs, NSA_G, NSA_HPG, TPAD, HD)[:, :, :, :ts].transpose(0, 3, 1, 2, 4).reshape(m, -1)

    q_h = pad_t(qm.reshape(bs, ts, MOBA_H, HD).transpose(0, 2, 1, 3), 2)
    sel = moba_gate_sample(pt, q_h, past["moba"], bs, n_pages)
    eye = jnp.eye(MOBA_H, dtype=F32)
    q_bd = ((q_h * SCALE)[:, :, :, None, :] * eye[None, :, None, :, None]).astype(BF16)
    q_bd = q_bd.reshape(bs, MOBA_H * TPAD, MOBA_H * HD)
    kvn = pad_t(kvm.reshape(bs, ts, 2 * MOBA_H * HD), 1)
    o_m = moba_sample(pt, q_bd, sel.reshape(bs, MOBA_H * TPAD, LANES), kvn, past["moba"], bs, n_pages, ts)
    o_moba = o_m.reshape(bs, MOBA_H, TPAD, HD)[:, :, :ts].transpose(0, 2, 1, 3).reshape(m, -1)

    x1 = attn_out_proj(o_nsa, o_moba, pp["w_out"], x2d, g1, m, 1)
    x2 = hier_moe_block(x1, pp["norm_ffn_g"][0:1], sh2, sc2, g2, pp["moe"][0], 0, m, 1, pp["final_g"], False)
    sh1, sc1, g1, sh2, sc2, g2 = mods[1]
    u = conv_pw1_glu(x2, pp["norm_mix_g"][1:2], sh1, sc1, pp["w_pw1"], pp["b_pw1"], m, 1)
    tb = lambda a: a.reshape(bs, ts, -1).transpose(1, 0, 2)
    st = past["conv"][0]
    x3_t = conv_sample(st.transpose(1, 0, 2), tb(u), pp["w_dw"], pp["b_dw"], pp["ln_g"], pp["ln_b"],
                       pp["w_pw2"], pp["b_pw2"], tb(x2), tb(g1[0]))
    x3 = x3_t.transpose(1, 0, 2).reshape(m, d)
    y = hier_moe_block(x3, pp["norm_ffn_g"][1:2], sh2, sc2, g2, pp["moe"][1], 1, m, 1, pp["final_g"], True)
    state = (
        kvc.reshape(1, bs, ts, 2, NSA_G, HD), kvs.reshape(1, bs, ts, 2, NSA_G, HD),
        jnp.concatenate([past["nsa_win"][0][:, ts:], kvw.reshape(bs, ts, 2, NSA_G, HD)], axis=1)[None],
        kvm.reshape(1, bs, ts, 2, MOBA_H, HD),
        jnp.concatenate([st[:, ts:], u.reshape(bs, ts, -1)], axis=1)[None],
    )
    return y.reshape(bs, ts, d), state


def _mods_from(m_all, rows, expand):
    out = []
    for layer in range(m_all.shape[0]):
        parts = jnp.split(m_all[layer, rows], 6, axis=-1)
        if expand:
            parts = [jnp.repeat(a, expand, axis=0)[None] for a in parts]
        else:
            parts = [a[:, None, :] for a in parts]
        out.append(parts)
    return out


def kernel(x_prompt, x_sample, cache_nsa_cmp_kv, cache_nsa_slc_kv, state_nsa_win_kv, cache_moba_kv, state_conv, page_table, c_prompt, c_sample, norm_mix_g, norm_ffn_g, ada_w, ada_b, attn_w_in, attn_w_out, nsa_cmp_pe, nsa_cmp_w1, nsa_cmp_w2, conv_w_pw1, conv_b_pw1, conv_w_dw, conv_b_dw, conv_ln_g, conv_ln_b, conv_w_pw2, conv_b_pw2, moe_wg, moe_bg, moe_we, moe_be, moe_w_gate, moe_w_up, moe_w_down, final_norm_g):
    p = dict(norm_mix_g=norm_mix_g, norm_ffn_g=norm_ffn_g, ada_w=ada_w, ada_b=ada_b, attn_w_in=attn_w_in,
             attn_w_out=attn_w_out, nsa_cmp_pe=nsa_cmp_pe, nsa_cmp_w1=nsa_cmp_w1, nsa_cmp_w2=nsa_cmp_w2,
             conv_w_pw1=conv_w_pw1, conv_b_pw1=conv_b_pw1, conv_w_dw=conv_w_dw, conv_b_dw=conv_b_dw,
             conv_ln_g=conv_ln_g, conv_ln_b=conv_ln_b, conv_w_pw2=conv_w_pw2, conv_b_pw2=conv_b_pw2,
             moe_wg=moe_wg, moe_bg=moe_bg, moe_we=moe_we, moe_be=moe_be, moe_w_gate=moe_w_gate,
             moe_w_up=moe_w_up, moe_w_down=moe_w_down, final_norm_g=final_norm_g)
    pp = _prep_params(p)
    bp = x_prompt.shape[0]
    bs, ts, d = x_sample.shape
    c_all = jnp.concatenate([c_prompt, c_sample], axis=0)
    pad = (-c_all.shape[0]) % 16
    c_all = jnp.pad(c_all, ((0, pad), (0, 0)))
    m_all = ada_params(c_all, ada_w, ada_b)
    mods_p = _mods_from(m_all, slice(0, bp), 0)
    y_p, (cmp_p, slc_p, win_p, moba_p, conv_p) = _trunk_prompt(x_prompt, mods_p, pp)
    mods_s = _mods_from(m_all, slice(bp, bp + bs), ts)
    past = dict(page_table=page_table, nsa_cmp=cache_nsa_cmp_kv, nsa_slc=cache_nsa_slc_kv,
                nsa_win=state_nsa_win_kv, moba=cache_moba_kv, conv=state_conv)
    y_s, (cmp_s, slc_s, win_s, moba_s, conv_s) = _trunk_sample(x_sample, mods_s, pp, past)
    return (y_p, y_s, cmp_p, cmp_s, slc_p, slc_s, win_p, win_s, moba_p, moba_s, conv_p, conv_s)
```

```python
import functools

import numpy as np
import jax
import jax.numpy as jnp
from jax import lax
from jax.experimental import pallas as pl
from jax.experimental.pallas import tpu as pltpu

F32 = jnp.float32
BF16 = jnp.bfloat16
I32 = jnp.int32

HD = 128
LANES = 128
SCALE = HD ** -0.5
NSA_G = 2
NSA_HPG = 4
CMP_BLOCK = 32
CMP_STRIDE = 16
SLC_BLOCK = 64
SLC_TOPN = 16
WINDOW = 512
FORCED = 1.0e4
MOBA_H = 8
MOBA_BLOCK = 256
MOBA_TOPK = 3
CONV_W = 31
N_GROUPS = 4
EPG = 8
N_EXP = N_GROUPS * EPG
PAGE = 128
EPS = 1e-6
NEG = -1e30
TE = 256
VMEM_LIMIT = 56 * 1024 * 1024


def _cp(sem, vmem=VMEM_LIMIT):
    return pltpu.CompilerParams(dimension_semantics=sem, vmem_limit_bytes=vmem)


def _nt(a, b):
    return lax.dot_general(a, b, (((1,), (1,)), ((), ())), preferred_element_type=F32)


def _mm(a, b):
    return jnp.dot(a, b, preferred_element_type=F32)


def _split(x):
    hi = x.astype(BF16)
    lo = (x - hi.astype(F32)).astype(BF16)
    return hi, lo


def _sigmoid(x):
    return 1.0 / (1.0 + jnp.exp(-x))


def _rms_mod(x, g, shift, scale):
    y = x * lax.rsqrt(jnp.mean(x * x, axis=-1, keepdims=True) + EPS)
    return (y * g) * (1.0 + scale) + shift


def _masked_softmax(s, valid):
    sm = jnp.where(valid, s, NEG)
    mx = jnp.max(sm, axis=-1, keepdims=True)
    e = jnp.where(valid, jnp.exp(sm - mx), 0.0)
    return e / jnp.maximum(jnp.sum(e, axis=-1, keepdims=True), 1e-30)


def _ada_kernel(c_ref, w_ref, b_ref, o_ref):
    c = c_ref[...]
    s = c * _sigmoid(c)
    shi, slo = _split(s)
    whi, wlo = _split(w_ref[0])
    o_ref[0] = _mm(shi, whi) + _mm(shi, wlo) + _mm(slo, whi) + b_ref[0]


def ada_params(c_all, ada_w, ada_b):
    depth, d, n6 = ada_w.shape
    r = c_all.shape[0]
    tn = 1024 if n6 % 1024 == 0 else 512
    return pl.pallas_call(
        _ada_kernel,
        out_shape=jax.ShapeDtypeStruct((depth, r, n6), F32),
        grid=(depth, n6 // tn),
        in_specs=[
            pl.BlockSpec((r, d), lambda l, j: (0, 0)),
            pl.BlockSpec((1, d, tn), lambda l, j: (l, 0, j)),
            pl.BlockSpec((1, 1, tn), lambda l, j: (l, 0, j)),
        ],
        out_specs=pl.BlockSpec((1, r, tn), lambda l, j: (l, 0, j)),
        compiler_params=_cp(("arbitrary", "arbitrary")),
        name="ada_params",
    )(c_all, ada_w, ada_b.reshape(depth, 1, n6))


_TN = 512
_SEGS = ((0, 2), (2, 1), (3, 1), (4, 1), (5, 2), (7, 4))
_N_MAIN_TILES = 11


def _inproj_kernel(x_ref, g_ref, sh_ref, sc_ref, w_ref, wg_ref,
                   qn_ref, kvc_ref, kvs_ref, kvw_ref, qm_ref, kvm_ref, gate_ref, kvmb_ref, h_scr):
    j = pl.program_id(1)

    @pl.when(j == 0)
    def _():
        h = _rms_mod(x_ref[...], g_ref[...], sh_ref[0], sc_ref[0]).astype(BF16)
        h_scr[...] = h
        gate_ref[...] = _sigmoid(_mm(h, wg_ref[...]))

    z = _mm(h_scr[...], w_ref[...])
    outs = (qn_ref, kvc_ref, kvs_ref, kvw_ref, qm_ref, kvm_ref)
    for ref, (start, n) in zip(outs, _SEGS):
        @pl.when((j >= start) & (j < start + n))
        def _(ref=ref):
            ref[...] = z
            if ref is kvm_ref:
                kvmb_ref[...] = z.astype(BF16)


def attn_in_proj(x2d, g, shift, scale, w_main, w_gate, tm, tiles_per_mod):
    m, d = x2d.shape
    r = shift.shape[1]

    def seg_spec(start, n):
        return pl.BlockSpec((tm, _TN), lambda i, j: (i, jnp.clip(j - start, 0, n - 1)))

    out_shape = [jax.ShapeDtypeStruct((m, n * _TN), F32) for (_, n) in _SEGS]
    out_shape.append(jax.ShapeDtypeStruct((m, LANES), F32))
    out_specs = [seg_spec(s, n) for (s, n) in _SEGS]
    out_specs.append(pl.BlockSpec((tm, LANES), lambda i, j: (i, 0)))
    out_shape.append(jax.ShapeDtypeStruct((m, _SEGS[-1][1] * _TN), BF16))
    out_specs.append(seg_spec(*_SEGS[-1]))
    mod_spec = pl.BlockSpec((1, r, d), lambda i, j: (i // tiles_per_mod, 0, 0))
    return pl.pallas_call(
        _inproj_kernel,
        out_shape=out_shape,
        grid=(m // tm, _N_MAIN_TILES),
        in_specs=[
            pl.BlockSpec((tm, d), lambda i, j: (i, 0)),
            pl.BlockSpec((1, d), lambda i, j: (0, 0)),
            mod_spec, mod_spec,
            pl.BlockSpec((d, _TN), lambda i, j: (0, j)),
            pl.BlockSpec((d, LANES), lambda i, j: (0, 0)),
        ],
        out_specs=out_specs,
        scratch_shapes=[pltpu.VMEM((tm, d), BF16)],
        compiler_params=_cp(("arbitrary", "arbitrary")),
        name="attn_in_proj",
    )(x2d, g, shift, scale, w_main, w_gate)


def _gelu_tanh(x):
    return 0.5 * x * (1.0 + jnp.tanh(0.7978845608028654 * (x + 0.044715 * x * x * x)))


def _compress_groups(x, c, g, wab_ref):
    col = (c * NSA_G + g) * HD
    xs = jnp.concatenate([x[:, r * 4 * HD + col: r * 4 * HD + col + HD] for r in range(CMP_STRIDE)],
                         axis=1).astype(BF16)
    return _mm(xs, wab_ref[c])


def _compress_finish(ab, c, pe_ref, wab_ref, w2_ref):
    rows = ab.shape[0]
    pe2 = _mm(pe_ref[c], wab_ref[c])
    pe_term = pe2[0:1, :HD] + pe2[1:2, HD:]
    pre = ab[:, :HD] + pltpu.roll(ab[:, HD:], rows - 1, 0) + pe_term
    return _mm(_gelu_tanh(pre).astype(BF16), w2_ref[c])


def _compress_prompt_kernel(x_ref, wab_ref, pe_ref, w2_ref, ck_ref, cv_ref):
    x = x_ref[0]
    for c, ref in ((0, ck_ref), (1, cv_ref)):
        for g in range(NSA_G):
            ab = _compress_groups(x, c, g, wab_ref)
            ref[0, g] = _compress_finish(ab, c, pe_ref, wab_ref, w2_ref).astype(BF16)


def compress_prompt(kvc, b, t, wab, pe2, w2):
    ng = t // CMP_STRIDE
    x = kvc.reshape(b, ng, CMP_STRIDE * 4 * HD)
    out = jax.ShapeDtypeStruct((b, NSA_G, ng, HD), BF16)
    ospec = pl.BlockSpec((1, NSA_G, ng, HD), lambda i: (i, 0, 0, 0))
    return pl.pallas_call(
        _compress_prompt_kernel,
        out_shape=[out, out],
        grid=(b,),
        in_specs=[
            pl.BlockSpec((1, ng, CMP_STRIDE * 4 * HD), lambda i: (i, 0, 0)),
            pl.BlockSpec(wab.shape, lambda i: (0, 0, 0)),
            pl.BlockSpec(pe2.shape, lambda i: (0, 0, 0)),
            pl.BlockSpec(w2.shape, lambda i: (0, 0, 0)),
        ],
        out_specs=[ospec, ospec],
        compiler_params=_cp(("arbitrary",)),
        name="nsa_compress_prompt",
    )(x, wab, pe2, w2)


def _select_blocks(score, lane, n_blocks, topn):
    cnt = jnp.zeros(score.shape, F32)
    for i in range(n_blocks):
        ci = score[:, i:i + 1]
        beats = (ci > score) | ((ci == score) & (lane > i))
        cnt = cnt + beats.astype(F32)
    return (cnt < topn) & (lane < n_blocks)


def _nsa_prompt_kernel(q_ref, ck_ref, cv_ref, ks_ref, vs_ref, kw_ref, vw_ref, gate_ref, c2s_ref, ee_ref,
                       o_ref, selx_ref, m_ref, l_ref, acc_ref, *, tq, tk, t_len, n_cmp, n_slc):
    g = pl.program_id(1)
    q0 = pl.program_id(2) * tq
    q = q_ref[...] * SCALE
    q4 = jnp.concatenate([q[:, h * HD:(h + 1) * HD] for h in range(NSA_HPG)], axis=0).astype(BF16)
    trow = q0 + lax.broadcasted_iota(I32, (tq, 1), 0)
    t4 = jnp.concatenate([trow] * NSA_HPG, axis=0)
    lane = lax.broadcasted_iota(I32, (1, LANES), 1)

    s = _nt(q4, ck_ref[0, 0])
    valid = (lane * CMP_STRIDE + (CMP_BLOCK - 1) <= t4) & (lane < n_cmp)
    p = _masked_softmax(s, valid)
    o_cmp = _mm(p.astype(BF16), cv_ref[0, 0])

    psum = p[0:tq] + p[tq:2 * tq] + p[2 * tq:3 * tq] + p[3 * tq:4 * tq]
    phi, plo = _split(psum)
    imp = _mm(phi, c2s_ref[...]) + _mm(plo, c2s_ref[...])
    qb = trow >> 6
    forced = (lane == 0) | (lane == qb) | (lane == qb - 1)
    causal = lane * SLC_BLOCK <= trow
    score = jnp.where(forced, FORCED, imp)
    score = jnp.where(causal, score, -1.0)
    score = jnp.where(lane < n_slc, score, -2.0)
    sel = _select_blocks(score, lane, n_slc, min(SLC_TOPN, n_slc))
    selx_ref[...] = _mm(sel.astype(BF16), ee_ref[...])

    def slc_tile(kt, first):
        k = ks_ref[kt * tk:(kt + 1) * tk, :].astype(BF16)
        v = vs_ref[kt * tk:(kt + 1) * tk, :].astype(BF16)
        sk = _nt(q4, k)
        mk = selx_ref[:, kt * tk:(kt + 1) * tk]
        mk4 = jnp.concatenate([mk] * NSA_HPG, axis=0)
        kpos = kt * tk + lax.broadcasted_iota(I32, (1, tk), 1)
        ok = (mk4 > 0.5) & (kpos <= t4)
        sk = jnp.where(ok, sk, NEG)
        mx = jnp.max(sk, axis=-1, keepdims=True)
        if first:
            pk = jnp.exp(sk - mx)
            m_ref[...] = mx
            l_ref[...] = jnp.sum(pk, axis=-1, keepdims=True)
            acc_ref[...] = _mm(pk.astype(BF16), v)
        else:
            m_old = m_ref[...]
            m_new = jnp.maximum(m_old, mx)
            alpha = jnp.exp(m_old - m_new)
            pk = jnp.exp(sk - m_new)
            m_ref[...] = m_new
            l_ref[...] = alpha * l_ref[...] + jnp.sum(pk, axis=-1, keepdims=True)
            acc_ref[...] = alpha * acc_ref[...] + _mm(pk.astype(BF16), v)

    slc_tile(0, True)
    for kt in range(1, t_len // tk):
        pl.when(kt * tk <= q0 + tq - 1)(functools.partial(slc_tile, kt, False))
    o_slc = acc_ref[...] / l_ref[...]

    span = WINDOW + tq
    start = pl.multiple_of(jnp.maximum(q0 - WINDOW, 0), LANES)
    kw = kw_ref[pl.ds(start, span), :].astype(BF16)
    vw = vw_ref[pl.ds(start, span), :].astype(BF16)
    sw = _nt(q4, kw)
    kpos = start + lax.broadcasted_iota(I32, (1, span), 1)
    okw = (kpos <= t4) & (kpos > t4 - WINDOW)
    o_win = _mm(_masked_softmax(sw, okw).astype(BF16), vw)

    gt = gate_ref[...]

    def gcol(kk):
        cols = []
        for h in range(NSA_HPG):
            c0 = gt[:, h * 3 + kk:h * 3 + kk + 1]
            c1 = gt[:, (NSA_HPG + h) * 3 + kk:(NSA_HPG + h) * 3 + kk + 1]
            cols.append(jnp.where(g == 0, c0, c1))
        return jnp.concatenate(cols, axis=0)

    o = gcol(0) * o_cmp + gcol(1) * o_slc + gcol(2) * o_win
    for h in range(NSA_HPG):
        o_ref[:, h * HD:(h + 1) * HD] = o[h * tq:(h + 1) * tq].astype(BF16)


def _cmp_to_slc(n_cmp, n_slc):
    cs = np.arange(n_cmp) * CMP_STRIDE
    ss = np.arange(n_slc) * SLC_BLOCK
    shared = (np.minimum(cs[:, None] + CMP_BLOCK, ss[None, :] + SLC_BLOCK)
              - np.maximum(cs[:, None], ss[None, :]))
    return np.clip(shared, 0, None) / CMP_STRIDE


def nsa_prompt(qn, ck, cv, kvs, kvw, gate, b, t):
    tq, tk = 128, 512
    nq = t // tq
    n_cmp = (t - CMP_BLOCK) // CMP_STRIDE + 1
    n_slc = -(-t // SLC_BLOCK)
    c2s = np.zeros((LANES, LANES), np.float32)
    c2s[:n_cmp, :n_slc] = _cmp_to_slc(n_cmp, n_slc)
    ee = (np.arange(LANES)[:, None] == (np.arange(t)[None, :] // SLC_BLOCK)).astype(np.float32)
    kern = functools.partial(_nsa_prompt_kernel, tq=tq, tk=tk, t_len=t, n_cmp=n_cmp, n_slc=n_slc)
    r = NSA_HPG * tq
    return pl.pallas_call(
        kern,
        out_shape=jax.ShapeDtypeStruct((b * t, NSA_G * NSA_HPG * HD), BF16),
        grid=(b, NSA_G, nq),
        in_specs=[
            pl.BlockSpec((tq, NSA_HPG * HD), lambda i, g, q: (i * nq + q, g)),
            pl.BlockSpec((1, 1, ck.shape[2], HD), lambda i, g, q: (i, g, 0, 0)),
            pl.BlockSpec((1, 1, cv.shape[2], HD), lambda i, g, q: (i, g, 0, 0)),
            pl.BlockSpec((t, HD), lambda i, g, q: (i, g)),
            pl.BlockSpec((t, HD), lambda i, g, q: (i, NSA_G + g)),
            pl.BlockSpec((t, HD), lambda i, g, q: (i, g)),
            pl.BlockSpec((t, HD), lambda i, g, q: (i, NSA_G + g)),
            pl.BlockSpec((tq, LANES), lambda i, g, q: (i * nq + q, 0)),
            pl.BlockSpec((LANES, LANES), lambda i, g, q: (0, 0)),
            pl.BlockSpec((LANES, t), lambda i, g, q: (0, 0)),
        ],
        out_specs=pl.BlockSpec((tq, NSA_HPG * HD), lambda i, g, q: (i * nq + q, g)),
        scratch_shapes=[pltpu.VMEM((tq, t), F32), pltpu.VMEM((r, 1), F32), pltpu.VMEM((r, 1), F32),
                        pltpu.VMEM((r, HD), F32)],
        compiler_params=_cp(("arbitrary", "arbitrary", "arbitrary")),
        name="nsa_prompt",
    )(qn, ck, cv, kvs, kvs, kvw, kvw, gate, jnp.asarray(c2s, BF16), jnp.asarray(ee, BF16))


MOBA_HPS = 4
KM_ROWS = 8


def _moba_kmean_kernel(k_ref, o_ref):
    n = pl.program_id(1)

    @pl.when(n == 0)
    def _():
        o_ref[...] = jnp.zeros(o_ref.shape, F32)

    o_ref[0, pl.ds(n, 1), :] = jnp.sum(k_ref[...], axis=0, keepdims=True) * (1.0 / MOBA_BLOCK)


def moba_kmean(kvm, b, t):
    nb = t // MOBA_BLOCK
    hk = MOBA_H * HD
    assert nb <= KM_ROWS
    return pl.pallas_call(
        _moba_kmean_kernel,
        out_shape=jax.ShapeDtypeStruct((b, KM_ROWS, hk), F32),
        grid=(b, nb),
        in_specs=[pl.BlockSpec((MOBA_BLOCK, hk), lambda i, n: (i * nb + n, 0))],
        out_specs=pl.BlockSpec((1, KM_ROWS, hk), lambda i, n: (i, 0, 0)),
        compiler_params=_cp(("arbitrary", "arbitrary")),
        name="moba_kmean",
    )(kvm)


def _moba_prompt_kernel(q_ref, k_ref, v_ref, km_ref, o_ref, *, t_len):
    qi = pl.program_id(2)
    nb = t_len // MOBA_BLOCK
    tq = MOBA_BLOCK
    lane = lax.broadcasted_iota(I32, (1, LANES), 1)
    blk = lax.broadcasted_iota(I32, (KM_ROWS, 1), 0)
    row = lax.broadcasted_iota(I32, (tq, 1), 0)
    col = lax.broadcasted_iota(I32, (1, tq), 1)
    d0 = pl.multiple_of(qi * MOBA_BLOCK, MOBA_BLOCK)

    qs_l, sel_l, init = [], [], []
    for hh in range(MOBA_HPS):
        cs = slice(hh * HD, (hh + 1) * HD)
        qf = q_ref[:, cs]
        qhi, qlo = _split(qf)
        khi, klo = _split(km_ref[0, :, cs])
        gt = _nt(khi, qhi) + _nt(klo, qhi) + _nt(khi, qlo)
        gt = jnp.where(blk < qi, gt, NEG)
        cnt = jnp.zeros((KM_ROWS, tq), F32)
        for i in range(nb):
            ci = gt[i:i + 1, :]
            cnt = cnt + ((ci > gt) | ((ci == gt) & (blk > i))).astype(F32)
        selt = ((cnt < min(MOBA_TOPK, nb)) & (blk < qi)).astype(F32)
        sel_l.append(jnp.concatenate([selt, jnp.zeros((LANES - KM_ROWS, tq), F32)], axis=0).T)
        qs = (qf * SCALE).astype(BF16)
        qs_l.append(qs)
        s = jnp.where(col <= row, _nt(qs, k_ref[pl.ds(d0, MOBA_BLOCK), cs]), NEG)
        m0 = jnp.max(s, axis=-1, keepdims=True)
        p0 = jnp.exp(s - m0)
        init.append((m0, jnp.sum(p0, axis=-1, keepdims=True),
                     _mm(p0.astype(BF16), v_ref[pl.ds(d0, MOBA_BLOCK), cs])))

    def body(kt, carry):
        k0 = pl.multiple_of(kt * MOBA_BLOCK, MOBA_BLOCK)
        out = []
        for hh in range(MOBA_HPS):
            cs = slice(hh * HD, (hh + 1) * HD)
            m_old, l_old, acc = carry[hh]
            on = jnp.sum(jnp.where(lane == kt, sel_l[hh], 0.0), axis=-1, keepdims=True) > 0.5
            sk = jnp.where(on, _nt(qs_l[hh], k_ref[pl.ds(k0, MOBA_BLOCK), cs]), NEG)
            m_new = jnp.maximum(m_old, jnp.max(sk, axis=-1, keepdims=True))
            alpha = jnp.exp(m_old - m_new)
            pk = jnp.exp(sk - m_new)
            out.append((m_new, alpha * l_old + jnp.sum(pk, axis=-1, keepdims=True),
                        alpha * acc + _mm(pk.astype(BF16), v_ref[pl.ds(k0, MOBA_BLOCK), cs])))
        return tuple(out)

    fin = lax.fori_loop(0, qi, body, tuple(init))
    for hh in range(MOBA_HPS):
        _, l_fin, acc = fin[hh]
        o_ref[:, hh * HD:(hh + 1) * HD] = (acc / l_fin).astype(BF16)


def moba_prompt(qm, kvm_bf, kmean, b, t):
    nq = t // MOBA_BLOCK
    ng = MOBA_H // MOBA_HPS
    w = MOBA_HPS * HD
    kern = functools.partial(_moba_prompt_kernel, t_len=t)
    return pl.pallas_call(
        kern,
        out_shape=jax.ShapeDtypeStruct((b * t, MOBA_H * HD), BF16),
        grid=(b, ng, nq),
        in_specs=[
            pl.BlockSpec((MOBA_BLOCK, w), lambda i, h, q: (i * nq + q, h)),
            pl.BlockSpec((t, w), lambda i, h, q: (i, h)),
            pl.BlockSpec((t, w), lambda i, h, q: (i, ng + h)),
            pl.BlockSpec((1, KM_ROWS, w), lambda i, h, q: (i, 0, h)),
        ],
        out_specs=pl.BlockSpec((MOBA_BLOCK, w), lambda i, h, q: (i * nq + q, h)),
        compiler_params=_cp(("arbitrary", "arbitrary", "arbitrary")),
        name="moba_prompt",
    )(qm, kvm_bf, kvm_bf, kmean)


def _outproj_kernel(on_ref, om_ref, w1_ref, w2_ref, x_ref, g_ref, o_ref):
    acc = _mm(on_ref[...], w1_ref[...]) + _mm(om_ref[...], w2_ref[...])
    o_ref[...] = x_ref[...] + g_ref[0] * acc


def attn_out_proj(o_nsa, o_moba, w_out, x2d, gate, tm, tiles_per_mod):
    m, d = x2d.shape
    r = gate.shape[1]
    kn = o_nsa.shape[1]
    return pl.pallas_call(
        _outproj_kernel,
        out_shape=jax.ShapeDtypeStruct((m, d), F32),
        grid=(m // tm,),
        in_specs=[
            pl.BlockSpec((tm, kn), lambda i: (i, 0)),
            pl.BlockSpec((tm, kn), lambda i: (i, 0)),
            pl.BlockSpec((kn, d), lambda i: (0, 0)),
            pl.BlockSpec((kn, d), lambda i: (1, 0)),
            pl.BlockSpec((tm, d), lambda i: (i, 0)),
            pl.BlockSpec((1, r, d), lambda i: (i // tiles_per_mod, 0, 0)),
        ],
        out_specs=pl.BlockSpec((tm, d), lambda i: (i, 0)),
        compiler_params=_cp(("arbitrary",)),
        name="attn_out_proj",
    )(o_nsa, o_moba, w_out, w_out, x2d, gate)


_GROUP_LANE0 = 64


def _router_kernel(x_ref, g_ref, sh_ref, sc_ref, wh_ref, wl_ref, b_ref, h_ref, meta_ref, cnt_ref, carry_ref,
                   *, tm):
    i = pl.program_id(0)

    @pl.when(i == 0)
    def _():
        carry_ref[...] = jnp.zeros(carry_ref.shape, F32)

    h = _rms_mod(x_ref[...], g_ref[...], sh_ref[0], sc_ref[0])
    h_ref[...] = h
    hhi, hlo = _split(h)
    lg = _mm(hhi, wh_ref[...]) + _mm(hhi, wl_ref[...]) + _mm(hlo, wh_ref[...]) + b_ref[...]
    lane = lax.broadcasted_iota(I32, (1, LANES), 1)
    lanef = lane.astype(F32)
    big = 1.0e9

    isg = (lane >= _GROUP_LANE0) & (lane < _GROUP_LANE0 + N_GROUPS)
    mxg = jnp.max(jnp.where(isg, lg, NEG), axis=-1, keepdims=True)
    grp = jnp.min(jnp.where(isg & (lg == mxg), lanef - _GROUP_LANE0, big), axis=-1, keepdims=True)
    pg = 1.0 / jnp.sum(jnp.where(isg, jnp.exp(lg - mxg), 0.0), axis=-1, keepdims=True)

    ing = (lane < N_EXP) & ((lane >> 3).astype(F32) == grp)
    l1 = jnp.max(jnp.where(ing, lg, NEG), axis=-1, keepdims=True)
    i1 = jnp.min(jnp.where(ing & (lg == l1), lanef, big), axis=-1, keepdims=True)
    ing2 = ing & (lanef != i1)
    l2 = jnp.max(jnp.where(ing2, lg, NEG), axis=-1, keepdims=True)
    i2 = jnp.min(jnp.where(ing2 & (lg == l2), lanef, big), axis=-1, keepdims=True)
    e21 = jnp.exp(l2 - l1)
    w1 = pg / (1.0 + e21)
    w2 = pg * e21 / (1.0 + e21)

    oh1 = lanef == i1
    oh2 = lanef == i2
    oh = (oh1 | oh2).astype(F32)
    r_i = lax.broadcasted_iota(I32, (tm, tm), 0)
    c_i = lax.broadcasted_iota(I32, (tm, tm), 1)
    lower = (c_i < r_i).astype(BF16)
    pref = _mm(lower, oh.astype(BF16)) + carry_ref[0:1, :]
    r1 = jnp.sum(jnp.where(oh1, pref, 0.0), axis=-1, keepdims=True)
    r2 = jnp.sum(jnp.where(oh2, pref, 0.0), axis=-1, keepdims=True)
    carry_ref[0:1, :] = carry_ref[0:1, :] + jnp.sum(oh, axis=0, keepdims=True)
    cnt_ref[...] = carry_ref[...]

    meta = jnp.where(lane == 0, i1, 0.0) + jnp.where(lane == 1, i2, 0.0) + jnp.where(lane == 2, r1, 0.0) \
        + jnp.where(lane == 3, r2, 0.0) + jnp.where(lane == 4, w1, 0.0) + jnp.where(lane == 5, w2, 0.0)
    meta_ref[...] = meta


def moe_router(x2d, g, shift, scale, wr_hi, wr_lo, br, tm, tiles_per_mod):
    m, d = x2d.shape
    r = shift.shape[1]
    mod_spec = pl.BlockSpec((1, r, d), lambda i: (i // tiles_per_mod, 0, 0))
    kern = functools.partial(_router_kernel, tm=tm)
    return pl.pallas_call(
        kern,
        out_shape=[jax.ShapeDtypeStruct((m, d), F32), jax.ShapeDtypeStruct((m, LANES), F32),
                   jax.ShapeDtypeStruct((8, LANES), F32)],
        grid=(m // tm,),
        in_specs=[
            pl.BlockSpec((tm, d), lambda i: (i, 0)),
            pl.BlockSpec((1, d), lambda i: (0, 0)),
            mod_spec, mod_spec,
            pl.BlockSpec((d, LANES), lambda i: (0, 0)),
            pl.BlockSpec((d, LANES), lambda i: (0, 0)),
            pl.BlockSpec((1, LANES), lambda i: (0, 0)),
        ],
        out_specs=[pl.BlockSpec((tm, d), lambda i: (i, 0)), pl.BlockSpec((tm, LANES), lambda i: (i, 0)),
                   pl.BlockSpec((8, LANES), lambda i: (0, 0))],
        scratch_shapes=[pltpu.VMEM((8, LANES), F32)],
        compiler_params=_cp(("arbitrary",)),
        name="moe_router",
    )(x2d, g, shift, scale, wr_hi, wr_lo, br)


def _scatter_kernel(seg_ref, has_ref, nu_ref, dest_ref, h_ref, xs_ref, zbuf, sem, zsem, *, tm, n_tiles, n_steps):
    i = pl.program_id(0)

    def zero_copy(row0):
        return pltpu.make_async_copy(zbuf, xs_ref.at[pl.ds(pl.multiple_of(row0, TE), TE), :], zsem)

    @pl.when(i == 0)
    def _():
        zbuf[...] = jnp.zeros(zbuf.shape, F32)
        for e in range(N_EXP):
            pl.when(has_ref[e] > 0)(lambda e=e: zero_copy(seg_ref[e]).start())

        def tail_start(tl, c):
            zero_copy(tl * TE).start()
            return c

        def tail_wait(tl, c):
            zero_copy(tl * TE).wait()
            return c

        lax.fori_loop(nu_ref[0], n_tiles, tail_start, 0)
        for e in range(N_EXP):
            pl.when(has_ref[e] > 0)(lambda e=e: zero_copy(seg_ref[e]).wait())
        lax.fori_loop(nu_ref[0], n_tiles, tail_wait, 0)

    base = i * tm

    def row_copy(r, k):
        return pltpu.make_async_copy(h_ref.at[pl.ds(base + r, 1), :], xs_ref.at[pl.ds(dest_ref[2 * r + k], 1), :],
                                     sem)

    def start(r, c):
        row_copy(r, 0).start(priority=0)
        row_copy(r, 1).start(priority=1)
        return c

    def wait(r, c):
        row_copy(r, 0).wait()
        row_copy(r, 1).wait()
        return c

    def wait_all():
        lax.fori_loop(0, tm, wait, 0, unroll=8)

    lax.fori_loop(0, tm, start, 0, unroll=8)
    pl.when(i > 0)(wait_all)
    pl.when(i == n_steps - 1)(wait_all)


def moe_scatter(h, dest_flat, seg_last, seg_has, n_used, n_rows, tm):
    m, d = h.shape
    kern = functools.partial(_scatter_kernel, tm=tm, n_tiles=n_rows // TE, n_steps=m // tm)
    gs = pltpu.PrefetchScalarGridSpec(
        num_scalar_prefetch=3,
        grid=(m // tm,),
        in_specs=[
            pl.BlockSpec((2 * tm,), lambda i, *_: (i,), memory_space=pltpu.SMEM),
            pl.BlockSpec(memory_space=pl.ANY),
        ],
        out_specs=pl.BlockSpec(memory_space=pl.ANY),
        scratch_shapes=[pltpu.VMEM((TE, d), F32), pltpu.SemaphoreType.DMA(()), pltpu.SemaphoreType.DMA(())],
    )
    return pl.pallas_call(
        kern,
        out_shape=jax.ShapeDtypeStruct((n_rows, d), F32),
        grid_spec=gs,
        compiler_params=_cp(("arbitrary",)),
        name="moe_scatter",
    )(seg_last, seg_has, n_used, dest_flat, h)


def _expert_kernel(te_ref, nu_ref, xs_ref, wg_ref, wu_ref, wd_ref, ys_ref):
    i = pl.program_id(0)

    @pl.when(i < nu_ref[0])
    def _():
        xb = xs_ref[...].astype(BF16)
        a = _mm(xb, wg_ref[0, 0].astype(BF16))
        u = _mm(xb, wu_ref[0, 0].astype(BF16))
        hid = (a * _sigmoid(a) * u).astype(BF16)
        ys_ref[...] = _mm(hid, wd_ref[0, 0].astype(BF16))

    @pl.when(i >= nu_ref[0])
    def _():
        ys_ref[...] = jnp.zeros(ys_ref.shape, F32)


def moe_experts(xs, tile_expert, n_used, w_gate, w_up, w_down, layer):
    n_rows, d = xs.shape
    f = w_gate.shape[-1]
    nt = n_rows // TE
    gs = pltpu.PrefetchScalarGridSpec(
        num_scalar_prefetch=2,
        grid=(nt,),
        in_specs=[
            pl.BlockSpec((TE, d), lambda i, te, nu: (jnp.minimum(i, nu[0] - 1), 0)),
            pl.BlockSpec((1, 1, d, f), lambda i, te, nu: (layer, te[i], 0, 0)),
            pl.BlockSpec((1, 1, d, f), lambda i, te, nu: (layer, te[i], 0, 0)),
            pl.BlockSpec((1, 1, f, d), lambda i, te, nu: (layer, te[i], 0, 0)),
        ],
        out_specs=pl.BlockSpec((TE, d), lambda i, te, nu: (i, 0)),
    )
    return pl.pallas_call(
        _expert_kernel,
        out_shape=jax.ShapeDtypeStruct((n_rows, d), F32),
        grid_spec=gs,
        compiler_params=_cp(("arbitrary",)),
        name="moe_experts",
    )(tile_expert, n_used, xs, w_gate, w_up, w_down)


def _combine_kernel(dcur_ref, dnext_ref, x_ref, g_ref, meta_ref, fg_ref, ys_ref, o_ref, rows, sem,
                    *, tm, final, n_steps):
    i = pl.program_id(0)
    slot = i % 2

    def row_copy(dref, s, r, k):
        return pltpu.make_async_copy(ys_ref.at[pl.ds(dref[2 * r + k], 1), :],
                                     rows.at[s, k, pl.ds(r, 1), :], sem.at[s])

    def gather(dref, s):
        def start(r, c):
            row_copy(dref, s, r, 0).start(priority=0)
            row_copy(dref, s, r, 1).start(priority=1)
            return c
        lax.fori_loop(0, tm, start, 0, unroll=8)

    pl.when(i == 0)(functools.partial(gather, dcur_ref, 0))
    pl.when(i + 1 < n_steps)(functools.partial(gather, dnext_ref, 1 - slot))

    def wait(r, c):
        row_copy(dcur_ref, slot, r, 0).wait()
        row_copy(dcur_ref, slot, r, 1).wait()
        return c

    lax.fori_loop(0, tm, wait, 0, unroll=8)
    meta = meta_ref[...]
    y = meta[:, 4:5] * rows[slot, 0] + meta[:, 5:6] * rows[slot, 1]
    x = x_ref[...] + g_ref[0] * y
    if final:
        x = (x * lax.rsqrt(jnp.mean(x * x, axis=-1, keepdims=True) + EPS)) * fg_ref[...]
    o_ref[...] = x


def moe_combine(x2d, gate, meta, final_g, ys, dest_flat, tm, tiles_per_mod, final):
    m, d = x2d.shape
    r = gate.shape[1]
    n_steps = m // tm
    kern = functools.partial(_combine_kernel, tm=tm, final=final, n_steps=n_steps)
    return pl.pallas_call(
        kern,
        out_shape=jax.ShapeDtypeStruct((m, d), F32),
        grid=(n_steps,),
        in_specs=[
            pl.BlockSpec((2 * tm,), lambda i: (i,), memory_space=pltpu.SMEM),
            pl.BlockSpec((2 * tm,), lambda i: (jnp.minimum(i + 1, n_steps - 1),), memory_space=pltpu.SMEM),
            pl.BlockSpec((tm, d), lambda i: (i, 0)),
            pl.BlockSpec((1, r, d), lambda i: (i // tiles_per_mod, 0, 0)),
            pl.BlockSpec((tm, LANES), lambda i: (i, 0)),
            pl.BlockSpec((1, d), lambda i: (0, 0)),
            pl.BlockSpec(memory_space=pl.ANY),
        ],
        out_specs=pl.BlockSpec((tm, d), lambda i: (i, 0)),
        scratch_shapes=[pltpu.VMEM((2, 2, tm, d), F32), pltpu.SemaphoreType.DMA((2,))],
        compiler_params=_cp(("arbitrary",)),
        name="moe_combine",
    )(dest_flat, dest_flat, x2d, gate, meta, final_g, ys)


def hier_moe_block(x2d, g_ffn, shift, scale, gate, p_layer, layer, tm, tiles_per_mod, final_g, final):
    m, d = x2d.shape
    wr_hi, wr_lo, br, w_gate, w_up, w_down = p_layer
    h, meta, cnt = moe_router(x2d, g_ffn, shift, scale, wr_hi, wr_lo, br, tm, tiles_per_mod)
    counts = cnt[0, :N_EXP].astype(I32)
    padded = ((counts + TE - 1) // TE) * TE
    ends = jnp.cumsum(padded)
    offs = ends - padded
    eid = meta[:, 0:2].astype(I32)
    dest = (offs[eid] + meta[:, 2:4].astype(I32)).reshape(-1)
    n_tiles = (2 * m) // TE + N_EXP
    n_used = (ends[-1] // TE).astype(I32).reshape(1)
    tile_start = jnp.arange(n_tiles, dtype=I32) * TE
    tile_clamped = jnp.minimum(tile_start, ends[-1] - 1)
    tile_expert = jnp.minimum(jnp.sum((tile_clamped[:, None] >= ends[None, :]).astype(I32), axis=1), N_EXP - 1)
    seg_last = jnp.maximum(ends - TE, 0).astype(I32)
    seg_has = (counts > 0).astype(I32)
    xs = moe_scatter(h, dest, seg_last, seg_has, n_used, n_tiles * TE, tm)
    ys = moe_experts(xs, tile_expert, n_used, w_gate, w_up, w_down, layer)
    return moe_combine(x2d, gate, meta, final_g, ys, dest, tm, tiles_per_mod, final)


def _pw1_kernel(x_ref, g_ref, sh_ref, sc_ref, wa_ref, wb_ref, ba_ref, bb_ref, u_ref, h_scr):
    @pl.when(pl.program_id(1) == 0)
    def _():
        h_scr[...] = _rms_mod(x_ref[...], g_ref[...], sh_ref[0], sc_ref[0]).astype(BF16)

    h = h_scr[...]
    a = _mm(h, wa_ref[...]) + ba_ref[...]
    b = _mm(h, wb_ref[...]) + bb_ref[...]
    u_ref[...] = a * _sigmoid(b)


def conv_pw1_glu(x2d, g, shift, scale, w_pw1, b_pw1, tm, tiles_per_mod):
    m, d = x2d.shape
    r = shift.shape[1]
    dc = w_pw1.shape[1] // 2
    tn = 512 if dc % 512 == 0 else 256
    nj = dc // tn
    mod_spec = pl.BlockSpec((1, r, d), lambda i, j: (i // tiles_per_mod, 0, 0))
    return pl.pallas_call(
        _pw1_kernel,
        out_shape=jax.ShapeDtypeStruct((m, dc), F32),
        grid=(m // tm, nj),
        in_specs=[
            pl.BlockSpec((tm, d), lambda i, j: (i, 0)),
            pl.BlockSpec((1, d), lambda i, j: (0, 0)),
            mod_spec, mod_spec,
            pl.BlockSpec((d, tn), lambda i, j: (0, j)),
            pl.BlockSpec((d, tn), lambda i, j: (0, nj + j)),
            pl.BlockSpec((1, tn), lambda i, j: (0, j)),
            pl.BlockSpec((1, tn), lambda i, j: (0, nj + j)),
        ],
        out_specs=pl.BlockSpec((tm, tn), lambda i, j: (i, j)),
        scratch_shapes=[pltpu.VMEM((tm, d), BF16)],
        compiler_params=_cp(("arbitrary", "arbitrary")),
        name="conv_pw1_glu",
    )(x2d, g, shift, scale, w_pw1, w_pw1, b_pw1, b_pw1)


_TAIL = 32


def _ln_silu(y, g, b):
    yc = y - jnp.mean(y, axis=-1, keepdims=True)
    z = yc * lax.rsqrt(jnp.mean(yc * yc, axis=-1, keepdims=True) + EPS)
    z = z * g + b
    return z * _sigmoid(z)


def _conv_prompt_kernel(u_ref, wdw_ref, bdw_ref, lg_ref, lb_ref, w2_ref, b2_ref, x_ref, g_ref, o_ref, buf,
                        *, tt):
    ti = pl.program_id(1)

    @pl.when(ti == 0)
    def _():
        buf[0:_TAIL, :] = jnp.zeros((_TAIL, buf.shape[1]), F32)

    buf[_TAIL:_TAIL + tt, :] = u_ref[...]
    off = _TAIL - (CONV_W - 1)
    acc = wdw_ref[0:1, :] * buf[off:off + tt, :]
    for k in range(1, CONV_W):
        acc = acc + wdw_ref[k:k + 1, :] * buf[off + k:off + k + tt, :]
    tail = buf[tt:tt + _TAIL, :]
    buf[0:_TAIL, :] = tail
    z = _ln_silu(acc + bdw_ref[...], lg_ref[...], lb_ref[...]).astype(BF16)
    out = _mm(z, w2_ref[...]) + b2_ref[...]
    o_ref[...] = x_ref[...] + g_ref[0] * out


def conv_prompt(u, w_dw, b_dw, ln_g, ln_b, w_pw2, b_pw2, x2d, gate, b, t):
    m, d = x2d.shape
    dc = u.shape[1]
    tt = 256
    nt = t // tt
    kern = functools.partial(_conv_prompt_kernel, tt=tt)
    vec = lambda n: pl.BlockSpec((1, n), lambda i, j: (0, 0))
    return pl.pallas_call(
        kern,
        out_shape=jax.ShapeDtypeStruct((m, d), F32),
        grid=(b, nt),
        in_specs=[
            pl.BlockSpec((tt, dc), lambda i, j: (i * nt + j, 0)),
            pl.BlockSpec((_TAIL, dc), lambda i, j: (0, 0)),
            vec(dc), vec(dc), vec(dc),
            pl.BlockSpec((dc, d), lambda i, j: (0, 0)),
            vec(d),
            pl.BlockSpec((tt, d), lambda i, j: (i * nt + j, 0)),
            pl.BlockSpec((1, 1, d), lambda i, j: (i, 0, 0)),
        ],
        out_specs=pl.BlockSpec((tt, d), lambda i, j: (i * nt + j, 0)),
        scratch_shapes=[pltpu.VMEM((_TAIL + tt, dc), F32)],
        compiler_params=_cp(("arbitrary", "arbitrary")),
        name="conv_prompt",
    )(u, w_dw, b_dw, ln_g, ln_b, w_pw2, b_pw2, x2d, gate)


def _prep_params(p):
    d = p["attn_w_in"].shape[1]
    nsa_q = NSA_G * NSA_HPG * HD
    nsa_kv = 2 * NSA_G * HD
    gate_w = 3 * NSA_G * NSA_HPG
    c_gate = nsa_q + 3 * nsa_kv
    w_in = p["attn_w_in"][0]
    w_main = jnp.concatenate([w_in[:, :c_gate], w_in[:, c_gate + gate_w:]], axis=1).astype(BF16)
    w_gate = jnp.pad(w_in[:, c_gate:c_gate + gate_w], ((0, 0), (0, LANES - gate_w))).astype(BF16)
    w1 = p["nsa_cmp_w1"][0]
    wab = jnp.concatenate([w1[:, :CMP_STRIDE].reshape(2, CMP_STRIDE * HD, HD),
                           w1[:, CMP_STRIDE:].reshape(2, CMP_STRIDE * HD, HD)], axis=2).astype(BF16)
    pe = p["nsa_cmp_pe"][0].reshape(2, 2, CMP_STRIDE * HD)
    pe2 = jnp.pad(pe, ((0, 0), (0, 14), (0, 0))).astype(BF16)
    w2 = p["nsa_cmp_w2"][0].astype(BF16)
    moe = []
    for layer in range(p["moe_wg"].shape[0]):
        wr = jnp.zeros((d, LANES), F32)
        wr = wr.at[:, :N_EXP].set(p["moe_we"][layer]).at[:, _GROUP_LANE0:_GROUP_LANE0 + N_GROUPS].set(
            p["moe_wg"][layer])
        br = jnp.zeros((1, LANES), F32)
        br = br.at[0, :N_EXP].set(p["moe_be"][layer]).at[0, _GROUP_LANE0:_GROUP_LANE0 + N_GROUPS].set(
            p["moe_bg"][layer])
        hi, lo = _split(wr)
        moe.append((hi, lo, br, p["moe_w_gate"], p["moe_w_up"], p["moe_w_down"]))
    w_dw = jnp.pad(p["conv_w_dw"][0], ((0, _TAIL - CONV_W), (0, 0)))
    return dict(
        w_main=w_main, w_gate=w_gate, wab=wab, pe2=pe2, w2=w2,
        w_out=p["attn_w_out"][0].astype(BF16), moe=moe,
        w_pw1=p["conv_w_pw1"][0].astype(BF16), b_pw1=p["conv_b_pw1"][0][None, :],
        w_dw=w_dw, b_dw=p["conv_b_dw"][0][None, :], ln_g=p["conv_ln_g"][0][None, :],
        ln_b=p["conv_ln_b"][0][None, :], w_pw2=p["conv_w_pw2"][0].astype(BF16),
        b_pw2=p["conv_b_pw2"][0][None, :],
        norm_mix_g=p["norm_mix_g"], norm_ffn_g=p["norm_ffn_g"], final_g=p["final_norm_g"][None, :],
    )


def _trunk_prompt(x, mods, pp):
    b, t, d = x.shape
    m = b * t
    x2d = x.reshape(m, d)
    sh1, sc1, g1, sh2, sc2, g2 = mods[0]
    tm = 512
    qn, kvc, kvs, kvw, qm, kvm, gate, kvm_bf = attn_in_proj(
        x2d, pp["norm_mix_g"][0:1], sh1, sc1, pp["w_main"], pp["w_gate"], tm, t // tm)
    ck, cv = compress_prompt(kvc, b, t, pp["wab"], pp["pe2"], pp["w2"])
    o_nsa = nsa_prompt(qn, ck, cv, kvs, kvw, gate, b, t)
    o_moba = moba_prompt(qm, kvm_bf, moba_kmean(kvm, b, t), b, t)
    tm2 = 256
    x1 = attn_out_proj(o_nsa, o_moba, pp["w_out"], x2d, g1, tm2, t // tm2)
    x2 = hier_moe_block(x1, pp["norm_ffn_g"][0:1], sh2, sc2, g2, pp["moe"][0], 0, tm2, t // tm2,
                        pp["final_g"], False)
    sh1, sc1, g1, sh2, sc2, g2 = mods[1]
    u = conv_pw1_glu(x2, pp["norm_mix_g"][1:2], sh1, sc1, pp["w_pw1"], pp["b_pw1"], tm, t // tm)
    x3 = conv_prompt(u, pp["w_dw"], pp["b_dw"], pp["ln_g"], pp["ln_b"], pp["w_pw2"], pp["b_pw2"], x2, g1, b, t)
    y = hier_moe_block(x3, pp["norm_ffn_g"][1:2], sh2, sc2, g2, pp["moe"][1], 1, tm2, t // tm2,
                       pp["final_g"], True)
    wlen = min(WINDOW, t)
    state = (
        kvc.reshape(1, b, t, 2, NSA_G, HD), kvs.reshape(1, b, t, 2, NSA_G, HD),
        kvw.reshape(b, t, 2, NSA_G, HD)[None, :, t - wlen:], kvm.reshape(1, b, t, 2, MOBA_H, HD),
        u.reshape(b, t, -1)[None, :, t - (CONV_W - 1):],
    )
    return y.reshape(b, t, d), state


TPAD = 8


def _page_specs(npg, n_pages, block, col_block):
    nd = len(block)

    def spec(i):
        def imap(b, j, pt):
            return (pt[b * n_pages + j * npg + i],) + (0,) * (nd - 2) + (col_block,)
        return pl.BlockSpec(block, imap)

    return [spec(i) for i in range(npg)]


def _compress_sample_kernel(pt_ref, *refs, npg, nsteps):
    pages = refs[:npg]
    wab_ref, pe_ref, w2_ref, ck_ref, cv_ref, ab_scr = refs[npg:]
    j = pl.program_id(1)
    rows = npg * (PAGE // CMP_STRIDE)
    r0 = pl.multiple_of(j * rows, rows)
    gpp = PAGE // CMP_STRIDE
    rpt = 2 * NSA_G
    for c in range(2):
        for g in range(NSA_G):
            cg = c * NSA_G + g
            xs = jnp.concatenate(
                [jnp.concatenate([pg[pl.ds(r * rpt + cg, gpp, stride=CMP_STRIDE * rpt), :]
                                  for r in range(CMP_STRIDE)], axis=1) for pg in pages], axis=0).astype(BF16)
            ab_scr[cg, pl.ds(r0, rows), :] = _mm(xs, wab_ref[c])

    @pl.when(j == nsteps - 1)
    def _():
        for c, ref in ((0, ck_ref), (1, cv_ref)):
            for g in range(NSA_G):
                ref[0, g] = _compress_finish(ab_scr[c * NSA_G + g], c, pe_ref, wab_ref, w2_ref).astype(BF16)


def compress_sample(pt, cache, bs, n_pages, wab, pe2, w2):
    npg = 16
    nsteps = n_pages // npg
    gpp = PAGE // CMP_STRIDE
    ng = n_pages * gpp
    x = cache.reshape(-1, HD)
    kern = functools.partial(_compress_sample_kernel, npg=npg, nsteps=nsteps)
    out = jax.ShapeDtypeStruct((bs, NSA_G, ng, HD), BF16)
    ospec = pl.BlockSpec((1, NSA_G, ng, HD), lambda b, j, pt: (b, 0, 0, 0))
    const = lambda a: pl.BlockSpec(a.shape, lambda b, j, pt: (0,) * a.ndim)
    gs = pltpu.PrefetchScalarGridSpec(
        num_scalar_prefetch=1,
        grid=(bs, nsteps),
        in_specs=_page_specs(npg, n_pages, (PAGE * 2 * NSA_G, HD), 0) + [const(wab), const(pe2), const(w2)],
        out_specs=[ospec, ospec],
        scratch_shapes=[pltpu.VMEM((2 * NSA_G, ng, 2 * HD), F32)],
    )
    return pl.pallas_call(
        kern, out_shape=[out, out], grid_spec=gs,
        compiler_params=_cp(("arbitrary", "arbitrary")), name="nsa_compress_sample",
    )(pt, *([x] * npg), wab, pe2, w2)


def _nsa_sample_kernel(pt_ref, q_ref, ck_ref, cv_ref, ksn_ref, kwn_ref, win_ref, gate_ref, c2s_ref, *refs,
                       npg, nsteps, past, ts, n_cmp, n_slc):
    pages = refs[:npg]
    o_ref, selx_scr, m_scr, l_scr, acc_scr, part_scr = refs[npg:]
    j = pl.program_id(1)
    qr = NSA_HPG * TPAD
    keys = npg * PAGE
    rpt = 2 * NSA_G
    wbuf = win_ref.shape[0] // rpt
    t_row = past + (lax.broadcasted_iota(I32, (qr, 1), 0) & (TPAD - 1))
    lane = lax.broadcasted_iota(I32, (1, LANES), 1)

    @pl.when(j == 0)
    def _():
        t8 = past + lax.broadcasted_iota(I32, (TPAD, 1), 0)
        for g in range(NSA_G):
            q = (q_ref[0, g] * SCALE).astype(BF16)
            gt = gate_ref[0, g]
            ncg = ck_ref.shape[2]
            lane_c = lax.broadcasted_iota(I32, (1, ncg), 1)
            valid = (lane_c * CMP_STRIDE + (CMP_BLOCK - 1) <= t_row) & (lane_c < n_cmp)
            p = _masked_softmax(_nt(q, ck_ref[0, g]), valid)
            o_cmp = _mm(p.astype(BF16), cv_ref[0, g])
            psum = p[0:TPAD] + p[TPAD:2 * TPAD] + p[2 * TPAD:3 * TPAD] + p[3 * TPAD:4 * TPAD]
            phi, plo = _split(psum)
            imp = _mm(phi, c2s_ref[...]) + _mm(plo, c2s_ref[...])
            nl = c2s_ref.shape[1]
            lane_s = lax.broadcasted_iota(I32, (1, nl), 1)
            qb = t8 >> 6
            forced = (lane_s == 0) | (lane_s == qb) | (lane_s == qb - 1)
            score = jnp.where(forced, FORCED, imp)
            score = jnp.where(lane_s * SLC_BLOCK <= t8, score, -1.0)
            score = jnp.where(lane_s < n_slc, score, -2.0)
            sel = _select_blocks(score, lane_s, n_slc, min(SLC_TOPN, n_slc)).astype(BF16)
            bps = keys // SLC_BLOCK
            for jj in range(nsteps):
                blk = lax.broadcasted_iota(I32, (nl, keys), 0)
                key = lax.broadcasted_iota(I32, (nl, keys), 1)
                ee = (blk == jj * bps + (key >> 6)).astype(BF16)
                selx_scr[g, jj] = _mm(sel, ee)
            own = jnp.sum(jnp.where(lane_s == (past >> 6), sel.astype(F32), 0.0), axis=-1, keepdims=True)
            own4 = jnp.concatenate([own] * NSA_HPG, axis=0) > 0.5
            zpad = jnp.zeros((LANES - TPAD, HD), F32)
            kw = jnp.concatenate([win_ref[pl.ds(g, wbuf, stride=rpt), :],
                                  kwn_ref[0, :, g * HD:(g + 1) * HD], zpad], axis=0).astype(BF16)
            vw = jnp.concatenate([win_ref[pl.ds(NSA_G + g, wbuf, stride=rpt), :],
                                  kwn_ref[0, :, (NSA_G + g) * HD:(NSA_G + g + 1) * HD], zpad], axis=0).astype(BF16)
            idx = lax.broadcasted_iota(I32, (1, wbuf + LANES), 1)
            kpos = past - wbuf + idx
            okw = (idx < wbuf + ts) & (kpos <= t_row) & (kpos > t_row - WINDOW)
            o_win = _mm(_masked_softmax(_nt(q, kw), okw).astype(BF16), vw)
            part_scr[g] = gt[:, 0:1] * o_cmp + gt[:, 2:3] * o_win
            kn = jnp.concatenate([ksn_ref[0, :, g * HD:(g + 1) * HD], zpad], axis=0).astype(BF16)
            vn = jnp.concatenate([ksn_ref[0, :, (NSA_G + g) * HD:(NSA_G + g + 1) * HD], zpad], axis=0).astype(BF16)
            ok0 = (past + lane <= t_row) & (lane < ts) & own4
            s0 = jnp.where(ok0, _nt(q, kn), NEG)
            m0 = jnp.max(s0, axis=-1, keepdims=True)
            p0 = jnp.where(ok0, jnp.exp(s0 - m0), 0.0)
            m_scr[g] = m0
            l_scr[g] = jnp.sum(p0, axis=-1, keepdims=True)
            acc_scr[g] = _mm(p0.astype(BF16), vn)

    for g in range(NSA_G):
        q = (q_ref[0, g] * SCALE).astype(BF16)
        k = jnp.concatenate([pg[pl.ds(g, PAGE, stride=rpt), :] for pg in pages], axis=0).astype(BF16)
        v = jnp.concatenate([pg[pl.ds(NSA_G + g, PAGE, stride=rpt), :] for pg in pages], axis=0).astype(BF16)
        mk = selx_scr[g, j]
        ok = jnp.concatenate([mk] * NSA_HPG, axis=0) > 0.5
        s = jnp.where(ok, _nt(q, k), NEG)
        m_old = m_scr[g]
        m_new = jnp.maximum(m_old, jnp.max(s, axis=-1, keepdims=True))
        alpha = jnp.exp(m_old - m_new)
        pk = jnp.where(ok, jnp.exp(s - m_new), 0.0)
        m_scr[g] = m_new
        l_scr[g] = alpha * l_scr[g] + jnp.sum(pk, axis=-1, keepdims=True)
        acc_scr[g] = alpha * acc_scr[g] + _mm(pk.astype(BF16), v)

    @pl.when(j == nsteps - 1)
    def _():
        for g in range(NSA_G):
            o_slc = acc_scr[g] / jnp.maximum(l_scr[g], 1e-30)
            o_ref[0, g] = (part_scr[g] + gate_ref[0, g][:, 1:2] * o_slc).astype(BF16)


def nsa_sample(pt, q_g, ck, cv, ksn, kwn, win, gate_g, cache, bs, n_pages, ts):
    npg = 8
    nsteps = n_pages // npg
    past = n_pages * PAGE
    n_cmp = (past + ts - CMP_BLOCK) // CMP_STRIDE + 1
    n_slc = -(-(past + ts) // SLC_BLOCK)
    ncg = ck.shape[2]
    nl = -(-n_slc // LANES) * LANES
    c2s = np.zeros((ncg, nl), np.float32)
    c2s[:n_cmp, :n_slc] = _cmp_to_slc(n_cmp, n_slc)
    c2s = jnp.asarray(c2s, BF16)
    qr = NSA_HPG * TPAD
    keys = npg * PAGE
    rpt = 2 * NSA_G
    x = cache.reshape(-1, HD)
    kern = functools.partial(_nsa_sample_kernel, npg=npg, nsteps=nsteps, past=past, ts=ts, n_cmp=n_cmp,
                             n_slc=n_slc)

    def per_b(a):
        return pl.BlockSpec((1,) + a.shape[1:], lambda b, j, pt: (b,) + (0,) * (a.ndim - 1))

    win_rows = win.shape[0] // bs
    gs = pltpu.PrefetchScalarGridSpec(
        num_scalar_prefetch=1,
        grid=(bs, nsteps),
        in_specs=[per_b(q_g), per_b(ck), per_b(cv), per_b(ksn), per_b(kwn),
                  pl.BlockSpec((win_rows, HD), lambda b, j, pt: (b, 0)), per_b(gate_g),
                  pl.BlockSpec(c2s.shape, lambda b, j, pt: (0, 0))]
        + _page_specs(npg, n_pages, (PAGE * rpt, HD), 0),
        out_specs=pl.BlockSpec((1, NSA_G, qr, HD), lambda b, j, pt: (b, 0, 0, 0)),
        scratch_shapes=[pltpu.VMEM((NSA_G, nsteps, TPAD, keys), F32), pltpu.VMEM((NSA_G, qr, 1), F32),
                        pltpu.VMEM((NSA_G, qr, 1), F32), pltpu.VMEM((NSA_G, qr, HD), F32),
                        pltpu.VMEM((NSA_G, qr, HD), F32)],
    )
    return pl.pallas_call(
        kern, out_shape=jax.ShapeDtypeStruct((bs, NSA_G, qr, HD), BF16), grid_spec=gs,
        compiler_params=_cp(("arbitrary", "arbitrary")), name="nsa_sample",
    )(pt, q_g, ck, cv, ksn, kwn, win, gate_g, c2s, *([x] * npg))


def _moba_gate_kernel(pt_ref, q_ref, *refs, npg, nsteps, nb_past):
    pages = refs[:npg]
    sel_ref, km_scr = refs[npg:]
    j = pl.program_id(1)
    ppb = MOBA_BLOCK // PAGE
    bps = npg // ppb

    @pl.when(j == 0)
    def _():
        km_scr[...] = jnp.zeros(km_scr.shape, F32)

    for i in range(bps):
        ssum = jnp.sum(pages[ppb * i][...], axis=0)
        for pp_ in range(1, ppb):
            ssum = ssum + jnp.sum(pages[ppb * i + pp_][...], axis=0)
        ssum = ssum * (1.0 / MOBA_BLOCK)
        for h in range(MOBA_H):
            km_scr[h, pl.ds(j * bps + i, 1), :] = ssum[h:h + 1, :]

    @pl.when(j == nsteps - 1)
    def _():
        lane = lax.broadcasted_iota(I32, (1, LANES), 1)
        for h in range(MOBA_H):
            qhi, qlo = _split(q_ref[0, h])
            khi, klo = _split(km_scr[h])
            gate = _nt(qhi, khi) + _nt(qhi, klo) + _nt(qlo, khi)
            gate = jnp.where(lane < nb_past, gate, NEG)
            sel = _select_blocks(gate, lane, nb_past + 1, min(MOBA_TOPK, nb_past + 1)) & (lane < nb_past)
            sel_ref[0, h] = sel.astype(F32)


def moba_gate_sample(pt, q_h, cache, bs, n_pages):
    npg = 8
    nsteps = n_pages // npg
    hk = MOBA_H * HD
    x = cache.reshape(-1, 2 * MOBA_H, HD)
    kern = functools.partial(_moba_gate_kernel, npg=npg, nsteps=nsteps, nb_past=n_pages * PAGE // MOBA_BLOCK)
    gs = pltpu.PrefetchScalarGridSpec(
        num_scalar_prefetch=1,
        grid=(bs, nsteps),
        in_specs=[pl.BlockSpec((1, MOBA_H, TPAD, HD), lambda b, j, pt: (b, 0, 0, 0))]
        + _page_specs(npg, n_pages, (PAGE, 2 * MOBA_H, HD), 0),
        out_specs=pl.BlockSpec((1, MOBA_H, TPAD, LANES), lambda b, j, pt: (b, 0, 0, 0)),
        scratch_shapes=[pltpu.VMEM((MOBA_H, LANES, HD), F32)],
    )
    return pl.pallas_call(
        kern, out_shape=jax.ShapeDtypeStruct((bs, MOBA_H, TPAD, LANES), F32), grid_spec=gs,
        compiler_params=_cp(("arbitrary", "arbitrary")), name="moba_gate_sample",
    )(pt, q_h, *([x] * npg))


def _moba_sample_kernel(pt_ref, qbd_ref, sel_ref, kvn_ref, *refs, npg, nsteps, ts):
    pages = refs[:npg]
    o_ref, m_scr, l_scr, acc_scr = refs[npg:]
    j = pl.program_id(1)
    hk = MOBA_H * HD
    qr = MOBA_H * TPAD
    qbd = qbd_ref[0]
    lane = lax.broadcasted_iota(I32, (1, LANES), 1)

    def diag(o_all):
        return jnp.concatenate([o_all[h * TPAD:(h + 1) * TPAD, h * HD:(h + 1) * HD] for h in range(MOBA_H)],
                               axis=0)

    @pl.when(j == 0)
    def _():
        zpad = jnp.zeros((LANES - TPAD, hk), F32)
        kn = jnp.concatenate([kvn_ref[0, :, :hk], zpad], axis=0).astype(BF16)
        vn = jnp.concatenate([kvn_ref[0, :, hk:], zpad], axis=0).astype(BF16)
        t8 = lax.broadcasted_iota(I32, (qr, 1), 0) & (TPAD - 1)
        ok0 = (lane <= t8) & (lane < ts)
        s0 = jnp.where(ok0, _nt(qbd, kn), NEG)
        m0 = jnp.max(s0, axis=-1, keepdims=True)
        p0 = jnp.where(ok0, jnp.exp(s0 - m0), 0.0)
        m_scr[...] = m0
        l_scr[...] = jnp.sum(p0, axis=-1, keepdims=True)
        acc_scr[...] = diag(_mm(p0.astype(BF16), vn))

    rpt = 2 * MOBA_H

    def heads(pg, c):
        return jnp.concatenate([pg[pl.ds(c * MOBA_H + h, PAGE, stride=rpt), :] for h in range(MOBA_H)], axis=1)

    k = jnp.concatenate([heads(pg, 0) for pg in pages], axis=0).astype(BF16)
    v = jnp.concatenate([heads(pg, 1) for pg in pages], axis=0).astype(BF16)
    s = _nt(qbd, k)
    sel = sel_ref[0]
    bps = npg * PAGE // MOBA_BLOCK
    cols = []
    for bb in range(bps):
        on = jnp.sum(jnp.where(lane == j * bps + bb, sel, 0.0), axis=-1, keepdims=True)
        cols.append(jnp.broadcast_to(on, (qr, MOBA_BLOCK)))
    ok = jnp.concatenate(cols, axis=1) > 0.5
    s = jnp.where(ok, s, NEG)
    m_old = m_scr[...]
    m_new = jnp.maximum(m_old, jnp.max(s, axis=-1, keepdims=True))
    alpha = jnp.exp(m_old - m_new)
    pk = jnp.where(ok, jnp.exp(s - m_new), 0.0)
    m_scr[...] = m_new
    l_scr[...] = alpha * l_scr[...] + jnp.sum(pk, axis=-1, keepdims=True)
    acc_scr[...] = alpha * acc_scr[...] + diag(_mm(pk.astype(BF16), v))

    @pl.when(j == nsteps - 1)
    def _():
        o_ref[0] = (acc_scr[...] / l_scr[...]).astype(BF16)


def moba_sample(pt, q_bd, sel, kvn, cache, bs, n_pages, ts):
    npg = 8
    nsteps = n_pages // npg
    hk = MOBA_H * HD
    qr = MOBA_H * TPAD
    x = cache.reshape(-1, HD)
    kern = functools.partial(_moba_sample_kernel, npg=npg, nsteps=nsteps, ts=ts)

    def per_b(a):
        return pl.BlockSpec((1,) + a.shape[1:], lambda b, j, pt: (b,) + (0,) * (a.ndim - 1))

    gs = pltpu.PrefetchScalarGridSpec(
        num_scalar_prefetch=1,
        grid=(bs, nsteps),
        in_specs=[per_b(q_bd), per_b(sel), per_b(kvn)]
        + _page_specs(npg, n_pages, (PAGE * 2 * MOBA_H, HD), 0),
        out_specs=pl.BlockSpec((1, qr, HD), lambda b, j, pt: (b, 0, 0)),
        scratch_shapes=[pltpu.VMEM((qr, 1), F32), pltpu.VMEM((qr, 1), F32), pltpu.VMEM((qr, HD), F32)],
    )
    return pl.pallas_call(
        kern, out_shape=jax.ShapeDtypeStruct((bs, qr, HD), BF16), grid_spec=gs,
        compiler_params=_cp(("arbitrary", "arbitrary")), name="moba_sample",
    )(pt, q_bd, sel, kvn, *([x] * npg))


def _conv_sample_kernel(st_ref, u_ref, wdw_ref, bdw_ref, lg_ref, lb_ref, w2_ref, b2_ref, x_ref, g_ref, o_ref,
                        *, ts):
    nst = CONV_W - 1
    bs = st_ref.shape[1]
    zs = []
    for t in range(ts):
        acc = None
        for k in range(CONV_W):
            r = t + k
            row = st_ref[r] if r < nst else u_ref[r - nst]
            term = wdw_ref[k:k + 1, :] * row
            acc = term if acc is None else acc + term
        zs.append(_ln_silu(acc + bdw_ref[...], lg_ref[...], lb_ref[...]))
    z = jnp.concatenate(zs, axis=0).astype(BF16)
    out = _mm(z, w2_ref[...]) + b2_ref[...]
    for t in range(ts):
        o_ref[t] = x_ref[t] + g_ref[t] * out[t * bs:(t + 1) * bs]


def conv_sample(st_t, u_t, w_dw, b_dw, ln_g, ln_b, w_pw2, b_pw2, x_t, g_t):
    ts = u_t.shape[0]
    kern = functools.partial(_conv_sample_kernel, ts=ts)
    full = lambda a: pl.BlockSpec(a.shape, lambda i: (0,) * a.ndim)
    args = (st_t, u_t, w_dw, b_dw, ln_g, ln_b, w_pw2, b_pw2, x_t, g_t)
    return pl.pallas_call(
        kern, out_shape=jax.ShapeDtypeStruct(x_t.shape, F32), grid=(1,),
        in_specs=[full(a) for a in args], out_specs=full(x_t),
        compiler_params=_cp(("arbitrary",)), name="conv_sample",
    )(*args)


def _trunk_sample(x, mods, pp, past):
    bs, ts, d = x.shape
    m = bs * ts
    n_pages = past["page_table"].shape[1]
    plen = n_pages * PAGE
    assert ts <= TPAD and ts < CMP_STRIDE and plen % MOBA_BLOCK == 0 and m % 8 == 0
    pt = past["page_table"].reshape(-1).astype(I32)
    x2d = x.reshape(m, d)
    sh1, sc1, g1, sh2, sc2, g2 = mods[0]
    qn, kvc, kvs, kvw, qm, kvm, gate, _ = attn_in_proj(
        x2d, pp["norm_mix_g"][0:1], sh1, sc1, pp["w_main"], pp["w_gate"], m, 1)

    def pad_t(a, axis):
        w = [(0, 0)] * a.ndim
        w[axis] = (0, TPAD - ts)
        return jnp.pad(a, w)

    ck, cv = compress_sample(pt, past["nsa_cmp"], bs, n_pages, pp["wab"], pp["pe2"], pp["w2"])
    q_g = pad_t(qn.reshape(bs, ts, NSA_G, NSA_HPG, HD).transpose(0, 2, 3, 1, 4), 3)
    q_g = q_g.reshape(bs, NSA_G, NSA_HPG * TPAD, HD)
    gate_g = pad_t(gate[:, :3 * NSA_G * NSA_HPG].reshape(bs, ts, NSA_G, NSA_HPG, 3).transpose(0, 2, 3, 1, 4), 3)
    gate_g = jnp.pad(gate_g.reshape(bs, NSA_G, NSA_HPG * TPAD, 3), ((0, 0), (0, 0), (0, 0), (0, LANES - 3)))
    ksn = pad_t(kvs.reshape(bs, ts, 4 * HD), 1)
    kwn = pad_t(kvw.reshape(bs, ts, 4 * HD), 1)
    win = past["nsa_win"].reshape(-1, HD)
    o_g = nsa_sample(pt, q_g, ck, cv, ksn, kwn, win, gate_g, past["nsa_slc"], bs, n_pages, ts)
    o_nsa = o_g.reshape(bs, NSA_G, NSA_HPG, TPAD, HD)[:, :, :, :ts].transpose(0, 3, 1, 2, 4).reshape(m, -1)

    q_h = pad_t(qm.reshape(bs, ts, MOBA_H, HD).transpose(0, 2, 1, 3), 2)
    sel = moba_gate_sample(pt, q_h, past["moba"], bs, n_pages)
    eye = jnp.eye(MOBA_H, dtype=F32)
    q_bd = ((q_h * SCALE)[:, :, :, None, :] * eye[None, :, None, :, None]).astype(BF16)
    q_bd = q_bd.reshape(bs, MOBA_H * TPAD, MOBA_H * HD)
    kvn = pad_t(kvm.reshape(bs, ts, 2 * MOBA_H * HD), 1)
    o_m = moba_sample(pt, q_bd, sel.reshape(bs, MOBA_H * TPAD, LANES), kvn, past["moba"], bs, n_pages, ts)
    o_moba = o_m.reshape(bs, MOBA_H, TPAD, HD)[:, :, :ts].transpose(0, 2, 1, 3).reshape(m, -1)

    x1 = attn_out_proj(o_nsa, o_moba, pp["w_out"], x2d, g1, m, 1)
    x2 = hier_moe_block(x1, pp["norm_ffn_g"][0:1], sh2, sc2, g2, pp["moe"][0], 0, m, 1, pp["final_g"], False)
    sh1, sc1, g1, sh2, sc2, g2 = mods[1]
    u = conv_pw1_glu(x2, pp["norm_mix_g"][1:2], sh1, sc1, pp["w_pw1"], pp["b_pw1"], m, 1)
    tb = lambda a: a.reshape(bs, ts, -1).transpose(1, 0, 2)
    st = past["conv"][0]
    x3_t = conv_sample(st.transpose(1, 0, 2), tb(u), pp["w_dw"], pp["b_dw"], pp["ln_g"], pp["ln_b"],
                       pp["w_pw2"], pp["b_pw2"], tb(x2), tb(g1[0]))
    x3 = x3_t.transpose(1, 0, 2).reshape(m, d)
    y = hier_moe_block(x3, pp["norm_ffn_g"][1:2], sh2, sc2, g2, pp["moe"][1], 1, m, 1, pp["final_g"], True)
    state = (
        kvc.reshape(1, bs, ts, 2, NSA_G, HD), kvs.reshape(1, bs, ts, 2, NSA_G, HD),
        jnp.concatenate([past["nsa_win"][0][:, ts:], kvw.reshape(bs, ts, 2, NSA_G, HD)], axis=1)[None],
        kvm.reshape(1, bs, ts, 2, MOBA_H, HD),
        jnp.concatenate([st[:, ts:], u.reshape(bs, ts, -1)], axis=1)[None],
    )
    return y.reshape(bs, ts, d), state


def _mods_from(m_all, rows, expand):
    out = []
    for layer in range(m_all.shape[0]):
        parts = jnp.split(m_all[layer, rows], 6, axis=-1)
        if expand:
            parts = [jnp.repeat(a, expand, axis=0)[None] for a in parts]
        else:
            parts = [a[:, None, :] for a in parts]
        out.append(parts)
    return out


def kernel(x_prompt, x_sample, cache_nsa_cmp_kv, cache_nsa_slc_kv, state_nsa_win_kv, cache_moba_kv, state_conv, page_table, c_prompt, c_sample, norm_mix_g, norm_ffn_g, ada_w, ada_b, attn_w_in, attn_w_out, nsa_cmp_pe, nsa_cmp_w1, nsa_cmp_w2, conv_w_pw1, conv_b_pw1, conv_w_dw, conv_b_dw, conv_ln_g, conv_ln_b, conv_w_pw2, conv_b_pw2, moe_wg, moe_bg, moe_we, moe_be, moe_w_gate, moe_w_up, moe_w_down, final_norm_g):
    p = dict(norm_mix_g=norm_mix_g, norm_ffn_g=norm_ffn_g, ada_w=ada_w, ada_b=ada_b, attn_w_in=attn_w_in,
             attn_w_out=attn_w_out, nsa_cmp_pe=nsa_cmp_pe, nsa_cmp_w1=nsa_cmp_w1, nsa_cmp_w2=nsa_cmp_w2,
             conv_w_pw1=conv_w_pw1, conv_b_pw1=conv_b_pw1, conv_w_dw=conv_w_dw, conv_b_dw=conv_b_dw,
             conv_ln_g=conv_ln_g, conv_ln_b=conv_ln_b, conv_w_pw2=conv_w_pw2, conv_b_pw2=conv_b_pw2,
             moe_wg=moe_wg, moe_bg=moe_bg, moe_we=moe_we, moe_be=moe_be, moe_w_gate=moe_w_gate,
             moe_w_up=moe_w_up, moe_w_down=moe_w_down, final_norm_g=final_norm_g)
    pp = _prep_params(p)
    bp = x_prompt.shape[0]
    bs, ts, d = x_sample.shape
    c_all = jnp.concatenate([c_prompt, c_sample], axis=0)
    pad = (-c_all.shape[0]) % 16
    c_all = jnp.pad(c_all, ((0, pad), (0, 0)))
    m_all = ada_params(c_all, ada_w, ada_b)
    mods_p = _mods_from(m_all, slice(0, bp), 0)
    y_p, (cmp_p, slc_p, win_p, moba_p, conv_p) = _trunk_prompt(x_prompt, mods_p, pp)
    mods_s = _mods_from(m_all, slice(bp, bp + bs), ts)
    past = dict(page_table=page_table, nsa_cmp=cache_nsa_cmp_kv, nsa_slc=cache_nsa_slc_kv,
                nsa_win=state_nsa_win_kv, moba=cache_moba_kv, conv=state_conv)
    y_s, (cmp_s, slc_s, win_s, moba_s, conv_s) = _trunk_sample(x_sample, mods_s, pp, past)
    return (y_p, y_s, cmp_p, cmp_s, slc_p, slc_s, win_p, win_s, moba_p, moba_s, conv_p, conv_s)
```

```python
import functools

import numpy as np
import jax
import jax.numpy as jnp
from jax import lax
from jax.experimental import pallas as pl
from jax.experimental.pallas import tpu as pltpu

F32 = jnp.float32
BF16 = jnp.bfloat16
I32 = jnp.int32

HD = 128
LANES = 128
SCALE = HD ** -0.5
NSA_G = 2
NSA_HPG = 4
CMP_BLOCK = 32
CMP_STRIDE = 16
SLC_BLOCK = 64
SLC_TOPN = 16
WINDOW = 512
FORCED = 1.0e4
MOBA_H = 8
MOBA_BLOCK = 256
MOBA_TOPK = 3
CONV_W = 31
N_GROUPS = 4
EPG = 8
N_EXP = N_GROUPS * EPG
PAGE = 128
EPS = 1e-6
NEG = -1e30
TE = 256
VMEM_LIMIT = 56 * 1024 * 1024


def _cp(sem, vmem=VMEM_LIMIT):
    return pltpu.CompilerParams(dimension_semantics=sem, vmem_limit_bytes=vmem)


def _nt(a, b):
    return lax.dot_general(a, b, (((1,), (1,)), ((), ())), preferred_element_type=F32)


def _mm(a, b):
    return jnp.dot(a, b, preferred_element_type=F32)


def _split(x):
    hi = x.astype(BF16)
    lo = (x - hi.astype(F32)).astype(BF16)
    return hi, lo


def _sigmoid(x):
    return 1.0 / (1.0 + jnp.exp(-x))


def _rms_mod(x, g, shift, scale):
    y = x * lax.rsqrt(jnp.mean(x * x, axis=-1, keepdims=True) + EPS)
    return (y * g) * (1.0 + scale) + shift


def _masked_softmax(s, valid):
    sm = jnp.where(valid, s, NEG)
    mx = jnp.max(sm, axis=-1, keepdims=True)
    e = jnp.where(valid, jnp.exp(sm - mx), 0.0)
    return e / jnp.maximum(jnp.sum(e, axis=-1, keepdims=True), 1e-30)


def _ada_kernel(c_ref, w_ref, b_ref, o_ref):
    c = c_ref[...]
    s = c * _sigmoid(c)
    shi, slo = _split(s)
    whi, wlo = _split(w_ref[0])
    o_ref[0] = _mm(shi, whi) + _mm(shi, wlo) + _mm(slo, whi) + b_ref[0]


def ada_params(c_all, ada_w, ada_b):
    depth, d, n6 = ada_w.shape
    r = c_all.shape[0]
    tn = 1024 if n6 % 1024 == 0 else 512
    return pl.pallas_call(
        _ada_kernel,
        out_shape=jax.ShapeDtypeStruct((depth, r, n6), F32),
        grid=(depth, n6 // tn),
        in_specs=[
            pl.BlockSpec((r, d), lambda l, j: (0, 0)),
            pl.BlockSpec((1, d, tn), lambda l, j: (l, 0, j)),
            pl.BlockSpec((1, 1, tn), lambda l, j: (l, 0, j)),
        ],
        out_specs=pl.BlockSpec((1, r, tn), lambda l, j: (l, 0, j)),
        compiler_params=_cp(("arbitrary", "arbitrary")),
        name="ada_params",
    )(c_all, ada_w, ada_b.reshape(depth, 1, n6))


_TN = 512
_SEGS = ((0, 2), (2, 1), (3, 1), (4, 1), (5, 2), (7, 4))
_N_MAIN_TILES = 11


def _inproj_kernel(x_ref, g_ref, sh_ref, sc_ref, w_ref, wg_ref,
                   qn_ref, kvc_ref, kvs_ref, kvw_ref, qm_ref, kvm_ref, gate_ref, kvmb_ref, h_scr):
    j = pl.program_id(1)

    @pl.when(j == 0)
    def _():
        h = _rms_mod(x_ref[...], g_ref[...], sh_ref[0], sc_ref[0]).astype(BF16)
        h_scr[...] = h
        gate_ref[...] = _sigmoid(_mm(h, wg_ref[...]))

    z = _mm(h_scr[...], w_ref[...])
    outs = (qn_ref, kvc_ref, kvs_ref, kvw_ref, qm_ref, kvm_ref)
    for ref, (start, n) in zip(outs, _SEGS):
        @pl.when((j >= start) & (j < start + n))
        def _(ref=ref):
            ref[...] = z
            if ref is kvm_ref:
                kvmb_ref[...] = z.astype(BF16)


def attn_in_proj(x2d, g, shift, scale, w_main, w_gate, tm, tiles_per_mod):
    m, d = x2d.shape
    r = shift.shape[1]

    def seg_spec(start, n):
        return pl.BlockSpec((tm, _TN), lambda i, j: (i, jnp.clip(j - start, 0, n - 1)))

    out_shape = [jax.ShapeDtypeStruct((m, n * _TN), F32) for (_, n) in _SEGS]
    out_shape.append(jax.ShapeDtypeStruct((m, LANES), F32))
    out_specs = [seg_spec(s, n) for (s, n) in _SEGS]
    out_specs.append(pl.BlockSpec((tm, LANES), lambda i, j: (i, 0)))
    out_shape.append(jax.ShapeDtypeStruct((m, _SEGS[-1][1] * _TN), BF16))
    out_specs.append(seg_spec(*_SEGS[-1]))
    mod_spec = pl.BlockSpec((1, r, d), lambda i, j: (i // tiles_per_mod, 0, 0))
    return pl.pallas_call(
        _inproj_kernel,
        out_shape=out_shape,
        grid=(m // tm, _N_MAIN_TILES),
        in_specs=[
            pl.BlockSpec((tm, d), lambda i, j: (i, 0)),
            pl.BlockSpec((1, d), lambda i, j: (0, 0)),
            mod_spec, mod_spec,
            pl.BlockSpec((d, _TN), lambda i, j: (0, j)),
            pl.BlockSpec((d, LANES), lambda i, j: (0, 0)),
        ],
        out_specs=out_specs,
        scratch_shapes=[pltpu.VMEM((tm, d), BF16)],
        compiler_params=_cp(("arbitrary", "arbitrary")),
        name="attn_in_proj",
    )(x2d, g, shift, scale, w_main, w_gate)


def _gelu_tanh(x):
    return 0.5 * x * (1.0 + jnp.tanh(0.7978845608028654 * (x + 0.044715 * x * x * x)))


def _compress_groups(x, c, g, wab_ref):
    col = (c * NSA_G + g) * HD
    xs = jnp.concatenate([x[:, r * 4 * HD + col: r * 4 * HD + col + HD] for r in range(CMP_STRIDE)],
                         axis=1).astype(BF16)
    return _mm(xs, wab_ref[c])


def _compress_finish(ab, c, pe_ref, wab_ref, w2_ref):
    rows = ab.shape[0]
    pe2 = _mm(pe_ref[c], wab_ref[c])
    pe_term = pe2[0:1, :HD] + pe2[1:2, HD:]
    pre = ab[:, :HD] + pltpu.roll(ab[:, HD:], rows - 1, 0) + pe_term
    return _mm(_gelu_tanh(pre).astype(BF16), w2_ref[c])


def _compress_prompt_kernel(x_ref, wab_ref, pe_ref, w2_ref, ck_ref, cv_ref):
    x = x_ref[0]
    for c, ref in ((0, ck_ref), (1, cv_ref)):
        for g in range(NSA_G):
            ab = _compress_groups(x, c, g, wab_ref)
            ref[0, g] = _compress_finish(ab, c, pe_ref, wab_ref, w2_ref).astype(BF16)


def compress_prompt(kvc, b, t, wab, pe2, w2):
    ng = t // CMP_STRIDE
    x = kvc.reshape(b, ng, CMP_STRIDE * 4 * HD)
    out = jax.ShapeDtypeStruct((b, NSA_G, ng, HD), BF16)
    ospec = pl.BlockSpec((1, NSA_G, ng, HD), lambda i: (i, 0, 0, 0))
    return pl.pallas_call(
        _compress_prompt_kernel,
        out_shape=[out, out],
        grid=(b,),
        in_specs=[
            pl.BlockSpec((1, ng, CMP_STRIDE * 4 * HD), lambda i: (i, 0, 0)),
            pl.BlockSpec(wab.shape, lambda i: (0, 0, 0)),
            pl.BlockSpec(pe2.shape, lambda i: (0, 0, 0)),
            pl.BlockSpec(w2.shape, lambda i: (0, 0, 0)),
        ],
        out_specs=[ospec, ospec],
        compiler_params=_cp(("arbitrary",)),
        name="nsa_compress_prompt",
    )(x, wab, pe2, w2)


def _select_blocks(score, lane, n_blocks, topn):
    cnt = jnp.zeros(score.shape, F32)
    for i in range(n_blocks):
        ci = score[:, i:i + 1]
        beats = (ci > score) | ((ci == score) & (lane > i))
        cnt = cnt + beats.astype(F32)
    return (cnt < topn) & (lane < n_blocks)


def _nsa_prompt_kernel(q_ref, ck_ref, cv_ref, ks_ref, vs_ref, kw_ref, vw_ref, gate_ref, c2s_ref, ee_ref,
                       o_ref, selx_ref, m_ref, l_ref, acc_ref, *, tq, tk, t_len, n_cmp, n_slc):
    g = pl.program_id(1)
    q0 = pl.program_id(2) * tq
    q = q_ref[...] * SCALE
    q4 = jnp.concatenate([q[:, h * HD:(h + 1) * HD] for h in range(NSA_HPG)], axis=0).astype(BF16)
    trow = q0 + lax.broadcasted_iota(I32, (tq, 1), 0)
    t4 = jnp.concatenate([trow] * NSA_HPG, axis=0)
    lane = lax.broadcasted_iota(I32, (1, LANES), 1)

    s = _nt(q4, ck_ref[0, 0])
    valid = (lane * CMP_STRIDE + (CMP_BLOCK - 1) <= t4) & (lane < n_cmp)
    p = _masked_softmax(s, valid)
    o_cmp = _mm(p.astype(BF16), cv_ref[0, 0])

    psum = p[0:tq] + p[tq:2 * tq] + p[2 * tq:3 * tq] + p[3 * tq:4 * tq]
    phi, plo = _split(psum)
    imp = _mm(phi, c2s_ref[...]) + _mm(plo, c2s_ref[...])
    qb = trow >> 6
    forced = (lane == 0) | (lane == qb) | (lane == qb - 1)
    causal = lane * SLC_BLOCK <= trow
    score = jnp.where(forced, FORCED, imp)
    score = jnp.where(causal, score, -1.0)
    score = jnp.where(lane < n_slc, score, -2.0)
    sel = _select_blocks(score, lane, n_slc, min(SLC_TOPN, n_slc))
    selx_ref[...] = _mm(sel.astype(BF16), ee_ref[...])

    def slc_tile(kt, first):
        k = ks_ref[kt * tk:(kt + 1) * tk, :].astype(BF16)
        v = vs_ref[kt * tk:(kt + 1) * tk, :].astype(BF16)
        sk = _nt(q4, k)
        mk = selx_ref[:, kt * tk:(kt + 1) * tk]
        mk4 = jnp.concatenate([mk] * NSA_HPG, axis=0)
        kpos = kt * tk + lax.broadcasted_iota(I32, (1, tk), 1)
        ok = (mk4 > 0.5) & (kpos <= t4)
        sk = jnp.where(ok, sk, NEG)
        mx = jnp.max(sk, axis=-1, keepdims=True)
        if first:
            pk = jnp.exp(sk - mx)
            m_ref[...] = mx
            l_ref[...] = jnp.sum(pk, axis=-1, keepdims=True)
            acc_ref[...] = _mm(pk.astype(BF16), v)
        else:
            m_old = m_ref[...]
            m_new = jnp.maximum(m_old, mx)
            alpha = jnp.exp(m_old - m_new)
            pk = jnp.exp(sk - m_new)
            m_ref[...] = m_new
            l_ref[...] = alpha * l_ref[...] + jnp.sum(pk, axis=-1, keepdims=True)
            acc_ref[...] = alpha * acc_ref[...] + _mm(pk.astype(BF16), v)

    slc_tile(0, True)
    for kt in range(1, t_len // tk):
        pl.when(kt * tk <= q0 + tq - 1)(functools.partial(slc_tile, kt, False))
    o_slc = acc_ref[...] / l_ref[...]

    span = WINDOW + tq
    start = pl.multiple_of(jnp.maximum(q0 - WINDOW, 0), LANES)
    kw = kw_ref[pl.ds(start, span), :].astype(BF16)
    vw = vw_ref[pl.ds(start, span), :].astype(BF16)
    sw = _nt(q4, kw)
    kpos = start + lax.broadcasted_iota(I32, (1, span), 1)
    okw = (kpos <= t4) & (kpos > t4 - WINDOW)
    o_win = _mm(_masked_softmax(sw, okw).astype(BF16), vw)

    gt = gate_ref[...]

    def gcol(kk):
        cols = []
        for h in range(NSA_HPG):
            c0 = gt[:, h * 3 + kk:h * 3 + kk + 1]
            c1 = gt[:, (NSA_HPG + h) * 3 + kk:(NSA_HPG + h) * 3 + kk + 1]
            cols.append(jnp.where(g == 0, c0, c1))
        return jnp.concatenate(cols, axis=0)

    o = gcol(0) * o_cmp + gcol(1) * o_slc + gcol(2) * o_win
    for h in range(NSA_HPG):
        o_ref[:, h * HD:(h + 1) * HD] = o[h * tq:(h + 1) * tq].astype(BF16)


def _cmp_to_slc(n_cmp, n_slc):
    cs = np.arange(n_cmp) * CMP_STRIDE
    ss = np.arange(n_slc) * SLC_BLOCK
    shared = (np.minimum(cs[:, None] + CMP_BLOCK, ss[None, :] + SLC_BLOCK)
              - np.maximum(cs[:, None], ss[None, :]))
    return np.clip(shared, 0, None) / CMP_STRIDE


def nsa_prompt(qn, ck, cv, kvs, kvw, gate, b, t):
    tq, tk = 128, 512
    nq = t // tq
    n_cmp = (t - CMP_BLOCK) // CMP_STRIDE + 1
    n_slc = -(-t // SLC_BLOCK)
    c2s = np.zeros((LANES, LANES), np.float32)
    c2s[:n_cmp, :n_slc] = _cmp_to_slc(n_cmp, n_slc)
    ee = (np.arange(LANES)[:, None] == (np.arange(t)[None, :] // SLC_BLOCK)).astype(np.float32)
    kern = functools.partial(_nsa_prompt_kernel, tq=tq, tk=tk, t_len=t, n_cmp=n_cmp, n_slc=n_slc)
    r = NSA_HPG * tq
    return pl.pallas_call(
        kern,
        out_shape=jax.ShapeDtypeStruct((b * t, NSA_G * NSA_HPG * HD), BF16),
        grid=(b, NSA_G, nq),
        in_specs=[
            pl.BlockSpec((tq, NSA_HPG * HD), lambda i, g, q: (i * nq + q, g)),
            pl.BlockSpec((1, 1, ck.shape[2], HD), lambda i, g, q: (i, g, 0, 0)),
            pl.BlockSpec((1, 1, cv.shape[2], HD), lambda i, g, q: (i, g, 0, 0)),
            pl.BlockSpec((t, HD), lambda i, g, q: (i, g)),
            pl.BlockSpec((t, HD), lambda i, g, q: (i, NSA_G + g)),
            pl.BlockSpec((t, HD), lambda i, g, q: (i, g)),
            pl.BlockSpec((t, HD), lambda i, g, q: (i, NSA_G + g)),
            pl.BlockSpec((tq, LANES), lambda i, g, q: (i * nq + q, 0)),
            pl.BlockSpec((LANES, LANES), lambda i, g, q: (0, 0)),
            pl.BlockSpec((LANES, t), lambda i, g, q: (0, 0)),
        ],
        out_specs=pl.BlockSpec((tq, NSA_HPG * HD), lambda i, g, q: (i * nq + q, g)),
        scratch_shapes=[pltpu.VMEM((tq, t), F32), pltpu.VMEM((r, 1), F32), pltpu.VMEM((r, 1), F32),
                        pltpu.VMEM((r, HD), F32)],
        compiler_params=_cp(("arbitrary", "arbitrary", "arbitrary")),
        name="nsa_prompt",
    )(qn, ck, cv, kvs, kvs, kvw, kvw, gate, jnp.asarray(c2s, BF16), jnp.asarray(ee, BF16))


MOBA_HPS = 4
KM_ROWS = 8


def _moba_kmean_kernel(k_ref, o_ref):
    n = pl.program_id(1)

    @pl.when(n == 0)
    def _():
        o_ref[...] = jnp.zeros(o_ref.shape, F32)

    o_ref[0, pl.ds(n, 1), :] = jnp.sum(k_ref[...], axis=0, keepdims=True) * (1.0 / MOBA_BLOCK)


def moba_kmean(kvm, b, t):
    nb = t // MOBA_BLOCK
    hk = MOBA_H * HD
    assert nb <= KM_ROWS
    return pl.pallas_call(
        _moba_kmean_kernel,
        out_shape=jax.ShapeDtypeStruct((b, KM_ROWS, hk), F32),
        grid=(b, nb),
        in_specs=[pl.BlockSpec((MOBA_BLOCK, hk), lambda i, n: (i * nb + n, 0))],
        out_specs=pl.BlockSpec((1, KM_ROWS, hk), lambda i, n: (i, 0, 0)),
        compiler_params=_cp(("arbitrary", "arbitrary")),
        name="moba_kmean",
    )(kvm)


def _moba_prompt_kernel(q_ref, k_ref, v_ref, km_ref, o_ref, *, t_len):
    qi = pl.program_id(2)
    nb = t_len // MOBA_BLOCK
    tq = MOBA_BLOCK
    lane = lax.broadcasted_iota(I32, (1, LANES), 1)
    blk = lax.broadcasted_iota(I32, (KM_ROWS, 1), 0)
    row = lax.broadcasted_iota(I32, (tq, 1), 0)
    col = lax.broadcasted_iota(I32, (1, tq), 1)
    d0 = pl.multiple_of(qi * MOBA_BLOCK, MOBA_BLOCK)

    qs_l, sel_l, init = [], [], []
    for hh in range(MOBA_HPS):
        cs = slice(hh * HD, (hh + 1) * HD)
        qf = q_ref[:, cs]
        qhi, qlo = _split(qf)
        khi, klo = _split(km_ref[0, :, cs])
        gt = _nt(khi, qhi) + _nt(klo, qhi) + _nt(khi, qlo)
        gt = jnp.where(blk < qi, gt, NEG)
        cnt = jnp.zeros((KM_ROWS, tq), F32)
        for i in range(nb):
            ci = gt[i:i + 1, :]
            cnt = cnt + ((ci > gt) | ((ci == gt) & (blk > i))).astype(F32)
        selt = ((cnt < min(MOBA_TOPK, nb)) & (blk < qi)).astype(F32)
        sel_l.append(jnp.concatenate([selt, jnp.zeros((LANES - KM_ROWS, tq), F32)], axis=0).T)
        qs = (qf * SCALE).astype(BF16)
        qs_l.append(qs)
        s = jnp.where(col <= row, _nt(qs, k_ref[pl.ds(d0, MOBA_BLOCK), cs]), NEG)
        m0 = jnp.max(s, axis=-1, keepdims=True)
        p0 = jnp.exp(s - m0)
        init.append((m0, jnp.sum(p0, axis=-1, keepdims=True),
                     _mm(p0.astype(BF16), v_ref[pl.ds(d0, MOBA_BLOCK), cs])))

    def body(kt, carry):
        k0 = pl.multiple_of(kt * MOBA_BLOCK, MOBA_BLOCK)
        out = []
        for hh in range(MOBA_HPS):
            cs = slice(hh * HD, (hh + 1) * HD)
            m_old, l_old, acc = carry[hh]
            on = jnp.sum(jnp.where(lane == kt, sel_l[hh], 0.0), axis=-1, keepdims=True) > 0.5
            sk = jnp.where(on, _nt(qs_l[hh], k_ref[pl.ds(k0, MOBA_BLOCK), cs]), NEG)
            m_new = jnp.maximum(m_old, jnp.max(sk, axis=-1, keepdims=True))
            alpha = jnp.exp(m_old - m_new)
            pk = jnp.exp(sk - m_new)
            out.append((m_new, alpha * l_old + jnp.sum(pk, axis=-1, keepdims=True),
                        alpha * acc + _mm(pk.astype(BF16), v_ref[pl.ds(k0, MOBA_BLOCK), cs])))
        return tuple(out)

    fin = lax.fori_loop(0, qi, body, tuple(init))
    for hh in range(MOBA_HPS):
        _, l_fin, acc = fin[hh]
        o_ref[:, hh * HD:(hh + 1) * HD] = (acc / l_fin).astype(BF16)


def moba_prompt(qm, kvm_bf, kmean, b, t):
    nq = t // MOBA_BLOCK
    ng = MOBA_H // MOBA_HPS
    w = MOBA_HPS * HD
    kern = functools.partial(_moba_prompt_kernel, t_len=t)
    return pl.pallas_call(
        kern,
        out_shape=jax.ShapeDtypeStruct((b * t, MOBA_H * HD), BF16),
        grid=(b, ng, nq),
        in_specs=[
            pl.BlockSpec((MOBA_BLOCK, w), lambda i, h, q: (i * nq + q, h)),
            pl.BlockSpec((t, w), lambda i, h, q: (i, h)),
            pl.BlockSpec((t, w), lambda i, h, q: (i, ng + h)),
            pl.BlockSpec((1, KM_ROWS, w), lambda i, h, q: (i, 0, h)),
        ],
        out_specs=pl.BlockSpec((MOBA_BLOCK, w), lambda i, h, q: (i * nq + q, h)),
        compiler_params=_cp(("arbitrary", "arbitrary", "arbitrary")),
        name="moba_prompt",
    )(qm, kvm_bf, kvm_bf, kmean)


def _outproj_kernel(on_ref, om_ref, w1_ref, w2_ref, x_ref, g_ref, o_ref):
    acc = _mm(on_ref[...], w1_ref[...]) + _mm(om_ref[...], w2_ref[...])
    o_ref[...] = x_ref[...] + g_ref[0] * acc


def attn_out_proj(o_nsa, o_moba, w_out, x2d, gate, tm, tiles_per_mod):
    m, d = x2d.shape
    r = gate.shape[1]
    kn = o_nsa.shape[1]
    return pl.pallas_call(
        _outproj_kernel,
        out_shape=jax.ShapeDtypeStruct((m, d), F32),
        grid=(m // tm,),
        in_specs=[
            pl.BlockSpec((tm, kn), lambda i: (i, 0)),
            pl.BlockSpec((tm, kn), lambda i: (i, 0)),
            pl.BlockSpec((kn, d), lambda i: (0, 0)),
            pl.BlockSpec((kn, d), lambda i: (1, 0)),
            pl.BlockSpec((tm, d), lambda i: (i, 0)),
            pl.BlockSpec((1, r, d), lambda i: (i // tiles_per_mod, 0, 0)),
        ],
        out_specs=pl.BlockSpec((tm, d), lambda i: (i, 0)),
        compiler_params=_cp(("arbitrary",)),
        name="attn_out_proj",
    )(o_nsa, o_moba, w_out, w_out, x2d, gate)


_GROUP_LANE0 = 64


def _router_kernel(x_ref, g_ref, sh_ref, sc_ref, wh_ref, wl_ref, b_ref, h_ref, meta_ref, cnt_ref, carry_ref,
                   *, tm):
    i = pl.program_id(0)

    @pl.when(i == 0)
    def _():
        carry_ref[...] = jnp.zeros(carry_ref.shape, F32)

    h = _rms_mod(x_ref[...], g_ref[...], sh_ref[0], sc_ref[0])
    h_ref[...] = h
    hhi, hlo = _split(h)
    lg = _mm(hhi, wh_ref[...]) + _mm(hhi, wl_ref[...]) + _mm(hlo, wh_ref[...]) + b_ref[...]
    lane = lax.broadcasted_iota(I32, (1, LANES), 1)
    lanef = lane.astype(F32)
    big = 1.0e9

    isg = (lane >= _GROUP_LANE0) & (lane < _GROUP_LANE0 + N_GROUPS)
    mxg = jnp.max(jnp.where(isg, lg, NEG), axis=-1, keepdims=True)
    grp = jnp.min(jnp.where(isg & (lg == mxg), lanef - _GROUP_LANE0, big), axis=-1, keepdims=True)
    pg = 1.0 / jnp.sum(jnp.where(isg, jnp.exp(lg - mxg), 0.0), axis=-1, keepdims=True)

    ing = (lane < N_EXP) & ((lane >> 3).astype(F32) == grp)
    l1 = jnp.max(jnp.where(ing, lg, NEG), axis=-1, keepdims=True)
    i1 = jnp.min(jnp.where(ing & (lg == l1), lanef, big), axis=-1, keepdims=True)
    ing2 = ing & (lanef != i1)
    l2 = jnp.max(jnp.where(ing2, lg, NEG), axis=-1, keepdims=True)
    i2 = jnp.min(jnp.where(ing2 & (lg == l2), lanef, big), axis=-1, keepdims=True)
    e21 = jnp.exp(l2 - l1)
    w1 = pg / (1.0 + e21)
    w2 = pg * e21 / (1.0 + e21)

    oh1 = lanef == i1
    oh2 = lanef == i2
    oh = (oh1 | oh2).astype(F32)
    r_i = lax.broadcasted_iota(I32, (tm, tm), 0)
    c_i = lax.broadcasted_iota(I32, (tm, tm), 1)
    lower = (c_i < r_i).astype(BF16)
    pref = _mm(lower, oh.astype(BF16)) + carry_ref[0:1, :]
    r1 = jnp.sum(jnp.where(oh1, pref, 0.0), axis=-1, keepdims=True)
    r2 = jnp.sum(jnp.where(oh2, pref, 0.0), axis=-1, keepdims=True)
    carry_ref[0:1, :] = carry_ref[0:1, :] + jnp.sum(oh, axis=0, keepdims=True)
    cnt_ref[...] = carry_ref[...]

    meta = jnp.where(lane == 0, i1, 0.0) + jnp.where(lane == 1, i2, 0.0) + jnp.where(lane == 2, r1, 0.0) \
        + jnp.where(lane == 3, r2, 0.0) + jnp.where(lane == 4, w1, 0.0) + jnp.where(lane == 5, w2, 0.0)
    meta_ref[...] = meta


def moe_router(x2d, g, shift, scale, wr_hi, wr_lo, br, tm, tiles_per_mod):
    m, d = x2d.shape
    r = shift.shape[1]
    mod_spec = pl.BlockSpec((1, r, d), lambda i: (i // tiles_per_mod, 0, 0))
    kern = functools.partial(_router_kernel, tm=tm)
    return pl.pallas_call(
        kern,
        out_shape=[jax.ShapeDtypeStruct((m, d), F32), jax.ShapeDtypeStruct((m, LANES), F32),
                   jax.ShapeDtypeStruct((8, LANES), F32)],
        grid=(m // tm,),
        in_specs=[
            pl.BlockSpec((tm, d), lambda i: (i, 0)),
            pl.BlockSpec((1, d), lambda i: (0, 0)),
            mod_spec, mod_spec,
            pl.BlockSpec((d, LANES), lambda i: (0, 0)),
            pl.BlockSpec((d, LANES), lambda i: (0, 0)),
            pl.BlockSpec((1, LANES), lambda i: (0, 0)),
        ],
        out_specs=[pl.BlockSpec((tm, d), lambda i: (i, 0)), pl.BlockSpec((tm, LANES), lambda i: (i, 0)),
                   pl.BlockSpec((8, LANES), lambda i: (0, 0))],
        scratch_shapes=[pltpu.VMEM((8, LANES), F32)],
        compiler_params=_cp(("arbitrary",)),
        name="moe_router",
    )(x2d, g, shift, scale, wr_hi, wr_lo, br)


def _scatter_kernel(seg_ref, has_ref, nu_ref, dest_ref, h_ref, xs_ref, zbuf, hbuf, sem, zsem,
                    *, tm, n_tiles, n_steps):
    i = pl.program_id(0)

    def zero_copy(row0):
        return pltpu.make_async_copy(zbuf, xs_ref.at[pl.ds(pl.multiple_of(row0, TE), TE), :], zsem)

    @pl.when(i == 0)
    def _():
        zbuf[...] = jnp.zeros(zbuf.shape, F32)
        for e in range(N_EXP):
            pl.when(has_ref[e] > 0)(lambda e=e: zero_copy(seg_ref[e]).start())

        def tail_start(tl, c):
            zero_copy(tl * TE).start()
            return c

        def tail_wait(tl, c):
            zero_copy(tl * TE).wait()
            return c

        lax.fori_loop(nu_ref[0], n_tiles, tail_start, 0)
        for e in range(N_EXP):
            pl.when(has_ref[e] > 0)(lambda e=e: zero_copy(seg_ref[e]).wait())
        lax.fori_loop(nu_ref[0], n_tiles, tail_wait, 0)

    slot = i % 2
    hbuf[slot] = h_ref[...]

    def row_copy(s, r, k):
        return pltpu.make_async_copy(hbuf.at[s, pl.ds(r, 1), :], xs_ref.at[pl.ds(dest_ref[2 * r + k], 1), :],
                                     sem.at[s])

    def start(r, c):
        row_copy(slot, r, 0).start(priority=0)
        row_copy(slot, r, 1).start(priority=1)
        return c

    def wait_slot(s):
        def wait(r, c):
            row_copy(s, r, 0).wait()
            row_copy(s, r, 1).wait()
            return c
        lax.fori_loop(0, tm, wait, 0, unroll=8)

    lax.fori_loop(0, tm, start, 0, unroll=8)
    pl.when(i > 0)(functools.partial(wait_slot, 1 - slot))
    pl.when(i == n_steps - 1)(functools.partial(wait_slot, slot))


def moe_scatter(h, dest_flat, seg_last, seg_has, n_used, n_rows, tm):
    m, d = h.shape
    kern = functools.partial(_scatter_kernel, tm=tm, n_tiles=n_rows // TE, n_steps=m // tm)
    gs = pltpu.PrefetchScalarGridSpec(
        num_scalar_prefetch=3,
        grid=(m // tm,),
        in_specs=[
            pl.BlockSpec((2 * tm,), lambda i, *_: (i,), memory_space=pltpu.SMEM),
            pl.BlockSpec((tm, d), lambda i, *_: (i, 0)),
        ],
        out_specs=pl.BlockSpec(memory_space=pl.ANY),
        scratch_shapes=[pltpu.VMEM((TE, d), F32), pltpu.VMEM((2, tm, d), F32),
                        pltpu.SemaphoreType.DMA((2,)), pltpu.SemaphoreType.DMA(())],
    )
    return pl.pallas_call(
        kern,
        out_shape=jax.ShapeDtypeStruct((n_rows, d), F32),
        grid_spec=gs,
        compiler_params=_cp(("arbitrary",)),
        name="moe_scatter",
    )(seg_last, seg_has, n_used, dest_flat, h)


def _expert_kernel(te_ref, nu_ref, xs_ref, wg_ref, wu_ref, wd_ref, ys_ref):
    i = pl.program_id(0)

    @pl.when(i < nu_ref[0])
    def _():
        xb = xs_ref[...].astype(BF16)
        a = _mm(xb, wg_ref[0, 0].astype(BF16))
        u = _mm(xb, wu_ref[0, 0].astype(BF16))
        hid = (a * _sigmoid(a) * u).astype(BF16)
        ys_ref[...] = _mm(hid, wd_ref[0, 0].astype(BF16))

    @pl.when(i >= nu_ref[0])
    def _():
        ys_ref[...] = jnp.zeros(ys_ref.shape, F32)


def moe_experts(xs, tile_expert, n_used, w_gate, w_up, w_down, layer):
    n_rows, d = xs.shape
    f = w_gate.shape[-1]
    nt = n_rows // TE
    gs = pltpu.PrefetchScalarGridSpec(
        num_scalar_prefetch=2,
        grid=(nt,),
        in_specs=[
            pl.BlockSpec((TE, d), lambda i, te, nu: (jnp.minimum(i, nu[0] - 1), 0)),
            pl.BlockSpec((1, 1, d, f), lambda i, te, nu: (layer, te[i], 0, 0)),
            pl.BlockSpec((1, 1, d, f), lambda i, te, nu: (layer, te[i], 0, 0)),
            pl.BlockSpec((1, 1, f, d), lambda i, te, nu: (layer, te[i], 0, 0)),
        ],
        out_specs=pl.BlockSpec((TE, d), lambda i, te, nu: (i, 0)),
    )
    return pl.pallas_call(
        _expert_kernel,
        out_shape=jax.ShapeDtypeStruct((n_rows, d), F32),
        grid_spec=gs,
        compiler_params=_cp(("arbitrary",)),
        name="moe_experts",
    )(tile_expert, n_used, xs, w_gate, w_up, w_down)


def _combine_kernel(dcur_ref, dnext_ref, x_ref, g_ref, meta_ref, fg_ref, ys_ref, o_ref, rows, sem,
                    *, tm, final, n_steps):
    i = pl.program_id(0)
    slot = i % 2

    def row_copy(dref, s, r, k):
        return pltpu.make_async_copy(ys_ref.at[pl.ds(dref[2 * r + k], 1), :],
                                     rows.at[s, k, pl.ds(r, 1), :], sem.at[s])

    def gather(dref, s):
        def start(r, c):
            row_copy(dref, s, r, 0).start(priority=0)
            row_copy(dref, s, r, 1).start(priority=1)
            return c
        lax.fori_loop(0, tm, start, 0, unroll=8)

    pl.when(i == 0)(functools.partial(gather, dcur_ref, 0))
    pl.when(i + 1 < n_steps)(functools.partial(gather, dnext_ref, 1 - slot))

    def wait(r, c):
        row_copy(dcur_ref, slot, r, 0).wait()
        row_copy(dcur_ref, slot, r, 1).wait()
        return c

    lax.fori_loop(0, tm, wait, 0, unroll=8)
    meta = meta_ref[...]
    y = meta[:, 4:5] * rows[slot, 0] + meta[:, 5:6] * rows[slot, 1]
    x = x_ref[...] + g_ref[0] * y
    if final:
        x = (x * lax.rsqrt(jnp.mean(x * x, axis=-1, keepdims=True) + EPS)) * fg_ref[...]
    o_ref[...] = x


def moe_combine(x2d, gate, meta, final_g, ys, dest_flat, tm, tiles_per_mod, final):
    m, d = x2d.shape
    r = gate.shape[1]
    n_steps = m // tm
    kern = functools.partial(_combine_kernel, tm=tm, final=final, n_steps=n_steps)
    return pl.pallas_call(
        kern,
        out_shape=jax.ShapeDtypeStruct((m, d), F32),
        grid=(n_steps,),
        in_specs=[
            pl.BlockSpec((2 * tm,), lambda i: (i,), memory_space=pltpu.SMEM),
            pl.BlockSpec((2 * tm,), lambda i: (jnp.minimum(i + 1, n_steps - 1),), memory_space=pltpu.SMEM),
            pl.BlockSpec((tm, d), lambda i: (i, 0)),
            pl.BlockSpec((1, r, d), lambda i: (i // tiles_per_mod, 0, 0)),
            pl.BlockSpec((tm, LANES), lambda i: (i, 0)),
            pl.BlockSpec((1, d), lambda i: (0, 0)),
            pl.BlockSpec(memory_space=pl.ANY),
        ],
        out_specs=pl.BlockSpec((tm, d), lambda i: (i, 0)),
        scratch_shapes=[pltpu.VMEM((2, 2, tm, d), F32), pltpu.SemaphoreType.DMA((2,))],
        compiler_params=_cp(("arbitrary",)),
        name="moe_combine",
    )(dest_flat, dest_flat, x2d, gate, meta, final_g, ys)


def hier_moe_block(x2d, g_ffn, shift, scale, gate, p_layer, layer, tm, tiles_per_mod, final_g, final):
    m, d = x2d.shape
    wr_hi, wr_lo, br, w_gate, w_up, w_down = p_layer
    h, meta, cnt = moe_router(x2d, g_ffn, shift, scale, wr_hi, wr_lo, br, tm, tiles_per_mod)
    counts = cnt[0, :N_EXP].astype(I32)
    padded = ((counts + TE - 1) // TE) * TE
    ends = jnp.cumsum(padded)
    offs = ends - padded
    eid = meta[:, 0:2].astype(I32)
    dest = (offs[eid] + meta[:, 2:4].astype(I32)).reshape(-1)
    n_tiles = (2 * m) // TE + N_EXP
    n_used = (ends[-1] // TE).astype(I32).reshape(1)
    tile_start = jnp.arange(n_tiles, dtype=I32) * TE
    tile_clamped = jnp.minimum(tile_start, ends[-1] - 1)
    tile_expert = jnp.minimum(jnp.sum((tile_clamped[:, None] >= ends[None, :]).astype(I32), axis=1), N_EXP - 1)
    seg_last = jnp.maximum(ends - TE, 0).astype(I32)
    seg_has = (counts > 0).astype(I32)
    xs = moe_scatter(h, dest, seg_last, seg_has, n_used, n_tiles * TE, tm)
    ys = moe_experts(xs, tile_expert, n_used, w_gate, w_up, w_down, layer)
    return moe_combine(x2d, gate, meta, final_g, ys, dest, tm, tiles_per_mod, final)


def _pw1_kernel(x_ref, g_ref, sh_ref, sc_ref, wa_ref, wb_ref, ba_ref, bb_ref, u_ref, h_scr):
    @pl.when(pl.program_id(1) == 0)
    def _():
        h_scr[...] = _rms_mod(x_ref[...], g_ref[...], sh_ref[0], sc_ref[0]).astype(BF16)

    h = h_scr[...]
    a = _mm(h, wa_ref[...]) + ba_ref[...]
    b = _mm(h, wb_ref[...]) + bb_ref[...]
    u_ref[...] = a * _sigmoid(b)


def conv_pw1_glu(x2d, g, shift, scale, w_pw1, b_pw1, tm, tiles_per_mod):
    m, d = x2d.shape
    r = shift.shape[1]
    dc = w_pw1.shape[1] // 2
    tn = 512 if dc % 512 == 0 else 256
    nj = dc // tn
    mod_spec = pl.BlockSpec((1, r, d), lambda i, j: (i // tiles_per_mod, 0, 0))
    return pl.pallas_call(
        _pw1_kernel,
        out_shape=jax.ShapeDtypeStruct((m, dc), F32),
        grid=(m // tm, nj),
        in_specs=[
            pl.BlockSpec((tm, d), lambda i, j: (i, 0)),
            pl.BlockSpec((1, d), lambda i, j: (0, 0)),
            mod_spec, mod_spec,
            pl.BlockSpec((d, tn), lambda i, j: (0, j)),
            pl.BlockSpec((d, tn), lambda i, j: (0, nj + j)),
            pl.BlockSpec((1, tn), lambda i, j: (0, j)),
            pl.BlockSpec((1, tn), lambda i, j: (0, nj + j)),
        ],
        out_specs=pl.BlockSpec((tm, tn), lambda i, j: (i, j)),
        scratch_shapes=[pltpu.VMEM((tm, d), BF16)],
        compiler_params=_cp(("arbitrary", "arbitrary")),
        name="conv_pw1_glu",
    )(x2d, g, shift, scale, w_pw1, w_pw1, b_pw1, b_pw1)


_TAIL = 32


def _ln_silu(y, g, b):
    yc = y - jnp.mean(y, axis=-1, keepdims=True)
    z = yc * lax.rsqrt(jnp.mean(yc * yc, axis=-1, keepdims=True) + EPS)
    z = z * g + b
    return z * _sigmoid(z)


_CONV_CH = 512


def _conv_prompt_kernel(u_ref, wdw_ref, bdw_ref, lg_ref, lb_ref, w2_ref, b2_ref, x_ref, g_ref, o_ref, buf, sh,
                        acc_scr, *, tt):
    ti = pl.program_id(1)

    @pl.when(ti == 0)
    def _():
        buf[0:_TAIL, :] = jnp.zeros((_TAIL, buf.shape[1]), F32)

    buf[_TAIL:_TAIL + tt, :] = u_ref[...]
    off = _TAIL - (CONV_W - 1)
    sub = 8
    span = tt + _TAIL - sub
    dc = buf.shape[1]
    ch = sh.shape[2]
    for c0 in range(0, dc, ch):
        cs = slice(c0, c0 + ch)
        for r in range(1, sub):
            sh[r, 0:span, :] = buf[r:r + span, cs]
        acc = None
        for k in range(CONV_W):
            a, r = divmod(off + k, sub)
            src = buf[sub * a:sub * a + tt, cs] if r == 0 else sh[r, sub * a:sub * a + tt, :]
            term = wdw_ref[k:k + 1, cs] * src
            acc = term if acc is None else acc + term
        acc_scr[:, cs] = acc
    tail = buf[tt:tt + _TAIL, :]
    buf[0:_TAIL, :] = tail
    z = _ln_silu(acc_scr[...] + bdw_ref[...], lg_ref[...], lb_ref[...]).astype(BF16)
    out = _mm(z, w2_ref[...]) + b2_ref[...]
    o_ref[...] = x_ref[...] + g_ref[0] * out


def conv_prompt(u, w_dw, b_dw, ln_g, ln_b, w_pw2, b_pw2, x2d, gate, b, t):
    m, d = x2d.shape
    dc = u.shape[1]
    tt = 256
    nt = t // tt
    kern = functools.partial(_conv_prompt_kernel, tt=tt)
    vec = lambda n: pl.BlockSpec((1, n), lambda i, j: (0, 0))
    return pl.pallas_call(
        kern,
        out_shape=jax.ShapeDtypeStruct((m, d), F32),
        grid=(b, nt),
        in_specs=[
            pl.BlockSpec((tt, dc), lambda i, j: (i * nt + j, 0)),
            pl.BlockSpec((_TAIL, dc), lambda i, j: (0, 0)),
            vec(dc), vec(dc), vec(dc),
            pl.BlockSpec((dc, d), lambda i, j: (0, 0)),
            vec(d),
            pl.BlockSpec((tt, d), lambda i, j: (i * nt + j, 0)),
            pl.BlockSpec((1, 1, d), lambda i, j: (i, 0, 0)),
        ],
        out_specs=pl.BlockSpec((tt, d), lambda i, j: (i * nt + j, 0)),
        scratch_shapes=[pltpu.VMEM((_TAIL + tt, dc), F32), pltpu.VMEM((8, _TAIL + tt, min(_CONV_CH, dc)), F32),
                        pltpu.VMEM((tt, dc), F32)],
        compiler_params=_cp(("arbitrary", "arbitrary")),
        name="conv_prompt",
    )(u, w_dw, b_dw, ln_g, ln_b, w_pw2, b_pw2, x2d, gate)


def _prep_params(p):
    d = p["attn_w_in"].shape[1]
    nsa_q = NSA_G * NSA_HPG * HD
    nsa_kv = 2 * NSA_G * HD
    gate_w = 3 * NSA_G * NSA_HPG
    c_gate = nsa_q + 3 * nsa_kv
    w_in = p["attn_w_in"][0]
    w_main = jnp.concatenate([w_in[:, :c_gate], w_in[:, c_gate + gate_w:]], axis=1).astype(BF16)
    w_gate = jnp.pad(w_in[:, c_gate:c_gate + gate_w], ((0, 0), (0, LANES - gate_w))).astype(BF16)
    w1 = p["nsa_cmp_w1"][0]
    wab = jnp.concatenate([w1[:, :CMP_STRIDE].reshape(2, CMP_STRIDE * HD, HD),
                           w1[:, CMP_STRIDE:].reshape(2, CMP_STRIDE * HD, HD)], axis=2).astype(BF16)
    pe = p["nsa_cmp_pe"][0].reshape(2, 2, CMP_STRIDE * HD)
    pe2 = jnp.pad(pe, ((0, 0), (0, 14), (0, 0))).astype(BF16)
    w2 = p["nsa_cmp_w2"][0].astype(BF16)
    moe = []
    for layer in range(p["moe_wg"].shape[0]):
        wr = jnp.zeros((d, LANES), F32)
        wr = wr.at[:, :N_EXP].set(p["moe_we"][layer]).at[:, _GROUP_LANE0:_GROUP_LANE0 + N_GROUPS].set(
            p["moe_wg"][layer])
        br = jnp.zeros((1, LANES), F32)
        br = br.at[0, :N_EXP].set(p["moe_be"][layer]).at[0, _GROUP_LANE0:_GROUP_LANE0 + N_GROUPS].set(
            p["moe_bg"][layer])
        hi, lo = _split(wr)
        moe.append((hi, lo, br, p["moe_w_gate"], p["moe_w_up"], p["moe_w_down"]))
    w_dw = jnp.pad(p["conv_w_dw"][0], ((0, _TAIL - CONV_W), (0, 0)))
    return dict(
        w_main=w_main, w_gate=w_gate, wab=wab, pe2=pe2, w2=w2,
        w_out=p["attn_w_out"][0].astype(BF16), moe=moe,
        w_pw1=p["conv_w_pw1"][0].astype(BF16), b_pw1=p["conv_b_pw1"][0][None, :],
        w_dw=w_dw, b_dw=p["conv_b_dw"][0][None, :], ln_g=p["conv_ln_g"][0][None, :],
        ln_b=p["conv_ln_b"][0][None, :], w_pw2=p["conv_w_pw2"][0].astype(BF16),
        b_pw2=p["conv_b_pw2"][0][None, :],
        norm_mix_g=p["norm_mix_g"], norm_ffn_g=p["norm_ffn_g"], final_g=p["final_norm_g"][None, :],
    )


def _trunk_prompt(x, mods, pp):
    b, t, d = x.shape
    m = b * t
    x2d = x.reshape(m, d)
    sh1, sc1, g1, sh2, sc2, g2 = mods[0]
    tm = 512
    qn, kvc, kvs, kvw, qm, kvm, gate, kvm_bf = attn_in_proj(
        x2d, pp["norm_mix_g"][0:1], sh1, sc1, pp["w_main"], pp["w_gate"], tm, t // tm)
    ck, cv = compress_prompt(kvc, b, t, pp["wab"], pp["pe2"], pp["w2"])
    o_nsa = nsa_prompt(qn, ck, cv, kvs, kvw, gate, b, t)
    o_moba = moba_prompt(qm, kvm_bf, moba_kmean(kvm, b, t), b, t)
    tm2 = 256
    x1 = attn_out_proj(o_nsa, o_moba, pp["w_out"], x2d, g1, tm2, t // tm2)
    x2 = hier_moe_block(x1, pp["norm_ffn_g"][0:1], sh2, sc2, g2, pp["moe"][0], 0, tm2, t // tm2,
                        pp["final_g"], False)
    sh1, sc1, g1, sh2, sc2, g2 = mods[1]
    u = conv_pw1_glu(x2, pp["norm_mix_g"][1:2], sh1, sc1, pp["w_pw1"], pp["b_pw1"], tm, t // tm)
    x3 = conv_prompt(u, pp["w_dw"], pp["b_dw"], pp["ln_g"], pp["ln_b"], pp["w_pw2"], pp["b_pw2"], x2, g1, b, t)
    y = hier_moe_block(x3, pp["norm_ffn_g"][1:2], sh2, sc2, g2, pp["moe"][1], 1, tm2, t // tm2,
                       pp["final_g"], True)
    wlen = min(WINDOW, t)
    state = (
        kvc.reshape(1, b, t, 2, NSA_G, HD), kvs.reshape(1, b, t, 2, NSA_G, HD),
        kvw.reshape(b, t, 2, NSA_G, HD)[None, :, t - wlen:], kvm.reshape(1, b, t, 2, MOBA_H, HD),
        u.reshape(b, t, -1)[None, :, t - (CONV_W - 1):],
    )
    return y.reshape(b, t, d), state


TPAD = 8


def _page_specs(npg, n_pages, block, col_block):
    nd = len(block)

    def spec(i):
        def imap(b, j, pt):
            return (pt[b * n_pages + j * npg + i],) + (0,) * (nd - 2) + (col_block,)
        return pl.BlockSpec(block, imap)

    return [spec(i) for i in range(npg)]


def _compress_sample_kernel(pt_ref, *refs, npg, nsteps):
    pages = refs[:npg]
    wab_ref, pe_ref, w2_ref, ck_ref, cv_ref, ab_scr = refs[npg:]
    j = pl.program_id(1)
    rows = npg * (PAGE // CMP_STRIDE)
    r0 = pl.multiple_of(j * rows, rows)
    gpp = PAGE // CMP_STRIDE
    rpt = 2 * NSA_G
    for c in range(2):
        for g in range(NSA_G):
            cg = c * NSA_G + g
            xs = jnp.concatenate(
                [jnp.concatenate([pg[pl.ds(r * rpt + cg, gpp, stride=CMP_STRIDE * rpt), :]
                                  for r in range(CMP_STRIDE)], axis=1) for pg in pages], axis=0).astype(BF16)
            ab_scr[cg, pl.ds(r0, rows), :] = _mm(xs, wab_ref[c])

    @pl.when(j == nsteps - 1)
    def _():
        for c, ref in ((0, ck_ref), (1, cv_ref)):
            for g in range(NSA_G):
                ref[0, g] = _compress_finish(ab_scr[c * NSA_G + g], c, pe_ref, wab_ref, w2_ref).astype(BF16)


def compress_sample(pt, cache, bs, n_pages, wab, pe2, w2):
    npg = 16
    nsteps = n_pages // npg
    gpp = PAGE // CMP_STRIDE
    ng = n_pages * gpp
    x = cache.reshape(-1, HD)
    kern = functools.partial(_compress_sample_kernel, npg=npg, nsteps=nsteps)
    out = jax.ShapeDtypeStruct((bs, NSA_G, ng, HD), BF16)
    ospec = pl.BlockSpec((1, NSA_G, ng, HD), lambda b, j, pt: (b, 0, 0, 0))
    const = lambda a: pl.BlockSpec(a.shape, lambda b, j, pt: (0,) * a.ndim)
    gs = pltpu.PrefetchScalarGridSpec(
        num_scalar_prefetch=1,
        grid=(bs, nsteps),
        in_specs=_page_specs(npg, n_pages, (PAGE * 2 * NSA_G, HD), 0) + [const(wab), const(pe2), const(w2)],
        out_specs=[ospec, ospec],
        scratch_shapes=[pltpu.VMEM((2 * NSA_G, ng, 2 * HD), F32)],
    )
    return pl.pallas_call(
        kern, out_shape=[out, out], grid_spec=gs,
        compiler_params=_cp(("arbitrary", "arbitrary")), name="nsa_compress_sample",
    )(pt, *([x] * npg), wab, pe2, w2)


def _nsa_sample_kernel(pt_ref, q_ref, ck_ref, cv_ref, ksn_ref, kwn_ref, win_ref, gate_ref, c2s_ref, *refs,
                       npg, nsteps, past, ts, n_cmp, n_slc):
    pages = refs[:npg]
    o_ref, selx_scr, m_scr, l_scr, acc_scr, part_scr = refs[npg:]
    j = pl.program_id(1)
    qr = NSA_HPG * TPAD
    keys = npg * PAGE
    rpt = 2 * NSA_G
    wbuf = win_ref.shape[0] // rpt
    t_row = past + (lax.broadcasted_iota(I32, (qr, 1), 0) & (TPAD - 1))
    lane = lax.broadcasted_iota(I32, (1, LANES), 1)

    @pl.when(j == 0)
    def _():
        t8 = past + lax.broadcasted_iota(I32, (TPAD, 1), 0)
        for g in range(NSA_G):
            q = (q_ref[0, g] * SCALE).astype(BF16)
            gt = gate_ref[0, g]
            ncg = ck_ref.shape[2]
            lane_c = lax.broadcasted_iota(I32, (1, ncg), 1)
            valid = (lane_c * CMP_STRIDE + (CMP_BLOCK - 1) <= t_row) & (lane_c < n_cmp)
            p = _masked_softmax(_nt(q, ck_ref[0, g]), valid)
            o_cmp = _mm(p.astype(BF16), cv_ref[0, g])
            psum = p[0:TPAD] + p[TPAD:2 * TPAD] + p[2 * TPAD:3 * TPAD] + p[3 * TPAD:4 * TPAD]
            phi, plo = _split(psum)
            imp = _mm(phi, c2s_ref[...]) + _mm(plo, c2s_ref[...])
            nl = c2s_ref.shape[1]
            lane_s = lax.broadcasted_iota(I32, (1, nl), 1)
            qb = t8 >> 6
            forced = (lane_s == 0) | (lane_s == qb) | (lane_s == qb - 1)
            score = jnp.where(forced, FORCED, imp)
            score = jnp.where(lane_s * SLC_BLOCK <= t8, score, -1.0)
            score = jnp.where(lane_s < n_slc, score, -2.0)
            sel = _select_blocks(score, lane_s, n_slc, min(SLC_TOPN, n_slc)).astype(BF16)
            bps = keys // SLC_BLOCK
            for jj in range(nsteps):
                blk = lax.broadcasted_iota(I32, (nl, keys), 0)
                key = lax.broadcasted_iota(I32, (nl, keys), 1)
                ee = (blk == jj * bps + (key >> 6)).astype(BF16)
                selx_scr[g, jj] = _mm(sel, ee)
            own = jnp.sum(jnp.where(lane_s == (past >> 6), sel.astype(F32), 0.0), axis=-1, keepdims=True)
            own4 = jnp.concatenate([own] * NSA_HPG, axis=0) > 0.5
            zpad = jnp.zeros((LANES - TPAD, HD), F32)
            kw = jnp.concatenate([win_ref[pl.ds(g, wbuf, stride=rpt), :],
                                  kwn_ref[0, :, g * HD:(g + 1) * HD], zpad], axis=0).astype(BF16)
            vw = jnp.concatenate([win_ref[pl.ds(NSA_G + g, wbuf, stride=rpt), :],
                                  kwn_ref[0, :, (NSA_G + g) * HD:(NSA_G + g + 1) * HD], zpad], axis=0).astype(BF16)
            idx = lax.broadcasted_iota(I32, (1, wbuf + LANES), 1)
            kpos = past - wbuf + idx
            okw = (idx < wbuf + ts) & (kpos <= t_row) & (kpos > t_row - WINDOW)
            o_win = _mm(_masked_softmax(_nt(q, kw), okw).astype(BF16), vw)
            part_scr[g] = gt[:, 0:1] * o_cmp + gt[:, 2:3] * o_win
            kn = jnp.concatenate([ksn_ref[0, :, g * HD:(g + 1) * HD], zpad], axis=0).astype(BF16)
            vn = jnp.concatenate([ksn_ref[0, :, (NSA_G + g) * HD:(NSA_G + g + 1) * HD], zpad], axis=0).astype(BF16)
            ok0 = (past + lane <= t_row) & (lane < ts) & own4
            s0 = jnp.where(ok0, _nt(q, kn), NEG)
            m0 = jnp.max(s0, axis=-1, keepdims=True)
            p0 = jnp.where(ok0, jnp.exp(s0 - m0), 0.0)
            m_scr[g] = m0
            l_scr[g] = jnp.sum(p0, axis=-1, keepdims=True)
            acc_scr[g] = _mm(p0.astype(BF16), vn)

    for g in range(NSA_G):
        q = (q_ref[0, g] * SCALE).astype(BF16)
        k = jnp.concatenate([pg[pl.ds(g, PAGE, stride=rpt), :] for pg in pages], axis=0).astype(BF16)
        v = jnp.concatenate([pg[pl.ds(NSA_G + g, PAGE, stride=rpt), :] for pg in pages], axis=0).astype(BF16)
        mk = selx_scr[g, j]
        ok = jnp.concatenate([mk] * NSA_HPG, axis=0) > 0.5
        s = jnp.where(ok, _nt(q, k), NEG)
        m_old = m_scr[g]
        m_new = jnp.maximum(m_old, jnp.max(s, axis=-1, keepdims=True))
        alpha = jnp.exp(m_old - m_new)
        pk = jnp.where(ok, jnp.exp(s - m_new), 0.0)
        m_scr[g] = m_new
        l_scr[g] = alpha * l_scr[g] + jnp.sum(pk, axis=-1, keepdims=True)
        acc_scr[g] = alpha * acc_scr[g] + _mm(pk.astype(BF16), v)

    @pl.when(j == nsteps - 1)
    def _():
        for g in range(NSA_G):
            o_slc = acc_scr[g] / jnp.maximum(l_scr[g], 1e-30)
            o_ref[0, g] = (part_scr[g] + gate_ref[0, g][:, 1:2] * o_slc).astype(BF16)


def nsa_sample(pt, q_g, ck, cv, ksn, kwn, win, gate_g, cache, bs, n_pages, ts):
    npg = 8
    nsteps = n_pages // npg
    past = n_pages * PAGE
    n_cmp = (past + ts - CMP_BLOCK) // CMP_STRIDE + 1
    n_slc = -(-(past + ts) // SLC_BLOCK)
    ncg = ck.shape[2]
    nl = -(-n_slc // LANES) * LANES
    c2s = np.zeros((ncg, nl), np.float32)
    c2s[:n_cmp, :n_slc] = _cmp_to_slc(n_cmp, n_slc)
    c2s = jnp.asarray(c2s, BF16)
    qr = NSA_HPG * TPAD
    keys = npg * PAGE
    rpt = 2 * NSA_G
    x = cache.reshape(-1, HD)
    kern = functools.partial(_nsa_sample_kernel, npg=npg, nsteps=nsteps, past=past, ts=ts, n_cmp=n_cmp,
                             n_slc=n_slc)

    def per_b(a):
        return pl.BlockSpec((1,) + a.shape[1:], lambda b, j, pt: (b,) + (0,) * (a.ndim - 1))

    win_rows = win.shape[0] // bs
    gs = pltpu.PrefetchScalarGridSpec(
        num_scalar_prefetch=1,
        grid=(bs, nsteps),
        in_specs=[per_b(q_g), per_b(ck), per_b(cv), per_b(ksn), per_b(kwn),
                  pl.BlockSpec((win_rows, HD), lambda b, j, pt: (b, 0)), per_b(gate_g),
                  pl.BlockSpec(c2s.shape, lambda b, j, pt: (0, 0))]
        + _page_specs(npg, n_pages, (PAGE * rpt, HD), 0),
        out_specs=pl.BlockSpec((1, NSA_G, qr, HD), lambda b, j, pt: (b, 0, 0, 0)),
        scratch_shapes=[pltpu.VMEM((NSA_G, nsteps, TPAD, keys), F32), pltpu.VMEM((NSA_G, qr, 1), F32),
                        pltpu.VMEM((NSA_G, qr, 1), F32), pltpu.VMEM((NSA_G, qr, HD), F32),
                        pltpu.VMEM((NSA_G, qr, HD), F32)],
    )
    return pl.pallas_call(
        kern, out_shape=jax.ShapeDtypeStruct((bs, NSA_G, qr, HD), BF16), grid_spec=gs,
        compiler_params=_cp(("arbitrary", "arbitrary")), name="nsa_sample",
    )(pt, q_g, ck, cv, ksn, kwn, win, gate_g, c2s, *([x] * npg))


def _moba_gate_kernel(pt_ref, q_ref, *refs, npg, nsteps, nb_past):
    pages = refs[:npg]
    sel_ref, km_scr = refs[npg:]
    j = pl.program_id(1)
    ppb = MOBA_BLOCK // PAGE
    bps = npg // ppb

    @pl.when(j == 0)
    def _():
        km_scr[...] = jnp.zeros(km_scr.shape, F32)

    for i in range(bps):
        ssum = jnp.sum(pages[ppb * i][...], axis=0)[0]
        for pp_ in range(1, ppb):
            ssum = ssum + jnp.sum(pages[ppb * i + pp_][...], axis=0)[0]
        ssum = ssum * (1.0 / MOBA_BLOCK)
        for h in range(MOBA_H):
            km_scr[h, pl.ds(j * bps + i, 1), :] = ssum[h:h + 1, :]

    @pl.when(j == nsteps - 1)
    def _():
        lane = lax.broadcasted_iota(I32, (1, LANES), 1)
        for h in range(MOBA_H):
            qhi, qlo = _split(q_ref[0, h])
            khi, klo = _split(km_scr[h])
            gate = _nt(qhi, khi) + _nt(qhi, klo) + _nt(qlo, khi)
            gate = jnp.where(lane < nb_past, gate, NEG)
            sel = _select_blocks(gate, lane, nb_past + 1, min(MOBA_TOPK, nb_past + 1)) & (lane < nb_past)
            sel_ref[0, h] = sel.astype(F32)


def moba_gate_sample(pt, q_h, cache, bs, n_pages):
    npg = 8
    nsteps = n_pages // npg
    hk = MOBA_H * HD
    x = cache.reshape(-1, 2, MOBA_H, HD)
    kern = functools.partial(_moba_gate_kernel, npg=npg, nsteps=nsteps, nb_past=n_pages * PAGE // MOBA_BLOCK)
    gs = pltpu.PrefetchScalarGridSpec(
        num_scalar_prefetch=1,
        grid=(bs, nsteps),
        in_specs=[pl.BlockSpec((1, MOBA_H, TPAD, HD), lambda b, j, pt: (b, 0, 0, 0))]
        + _page_specs(npg, n_pages, (PAGE, 1, MOBA_H, HD), 0),
        out_specs=pl.BlockSpec((1, MOBA_H, TPAD, LANES), lambda b, j, pt: (b, 0, 0, 0)),
        scratch_shapes=[pltpu.VMEM((MOBA_H, LANES, HD), F32)],
    )
    return pl.pallas_call(
        kern, out_shape=jax.ShapeDtypeStruct((bs, MOBA_H, TPAD, LANES), F32), grid_spec=gs,
        compiler_params=_cp(("arbitrary", "arbitrary")), name="moba_gate_sample",
    )(pt, q_h, *([x] * npg))


def _moba_sample_kernel(pt_ref, qbd_ref, sel_ref, kvn_ref, *refs, npg, nsteps, ts):
    pages = refs[:npg]
    o_ref, m_scr, l_scr, acc_scr = refs[npg:]
    j = pl.program_id(1)
    hk = MOBA_H * HD
    qr = MOBA_H * TPAD
    qbd = qbd_ref[0]
    lane = lax.broadcasted_iota(I32, (1, LANES), 1)

    def diag(o_all):
        return jnp.concatenate([o_all[h * TPAD:(h + 1) * TPAD, h * HD:(h + 1) * HD] for h in range(MOBA_H)],
                               axis=0)

    @pl.when(j == 0)
    def _():
        zpad = jnp.zeros((LANES - TPAD, hk), F32)
        kn = jnp.concatenate([kvn_ref[0, :, :hk], zpad], axis=0).astype(BF16)
        vn = jnp.concatenate([kvn_ref[0, :, hk:], zpad], axis=0).astype(BF16)
        t8 = lax.broadcasted_iota(I32, (qr, 1), 0) & (TPAD - 1)
        ok0 = (lane <= t8) & (lane < ts)
        s0 = jnp.where(ok0, _nt(qbd, kn), NEG)
        m0 = jnp.max(s0, axis=-1, keepdims=True)
        p0 = jnp.where(ok0, jnp.exp(s0 - m0), 0.0)
        m_scr[...] = m0
        l_scr[...] = jnp.sum(p0, axis=-1, keepdims=True)
        acc_scr[...] = diag(_mm(p0.astype(BF16), vn))

    rpt = 2 * MOBA_H

    def heads(pg, c):
        return jnp.concatenate([pg[pl.ds(c * MOBA_H + h, PAGE, stride=rpt), :] for h in range(MOBA_H)], axis=1)

    k = jnp.concatenate([heads(pg, 0) for pg in pages], axis=0).astype(BF16)
    v = jnp.concatenate([heads(pg, 1) for pg in pages], axis=0).astype(BF16)
    s = _nt(qbd, k)
    sel = sel_ref[0]
    bps = npg * PAGE // MOBA_BLOCK
    cols = []
    for bb in range(bps):
        on = jnp.sum(jnp.where(lane == j * bps + bb, sel, 0.0), axis=-1, keepdims=True)
        cols.append(jnp.broadcast_to(on, (qr, MOBA_BLOCK)))
    ok = jnp.concatenate(cols, axis=1) > 0.5
    s = jnp.where(ok, s, NEG)
    m_old = m_scr[...]
    m_new = jnp.maximum(m_old, jnp.max(s, axis=-1, keepdims=True))
    alpha = jnp.exp(m_old - m_new)
    pk = jnp.where(ok, jnp.exp(s - m_new), 0.0)
    m_scr[...] = m_new
    l_scr[...] = alpha * l_scr[...] + jnp.sum(pk, axis=-1, keepdims=True)
    acc_scr[...] = alpha * acc_scr[...] + diag(_mm(pk.astype(BF16), v))

    @pl.when(j == nsteps - 1)
    def _():
        o_ref[0] = (acc_scr[...] / l_scr[...]).astype(BF16)


def moba_sample(pt, q_bd, sel, kvn, cache, bs, n_pages, ts):
    npg = 8
    nsteps = n_pages // npg
    hk = MOBA_H * HD
    qr = MOBA_H * TPAD
    x = cache.reshape(-1, HD)
    kern = functools.partial(_moba_sample_kernel, npg=npg, nsteps=nsteps, ts=ts)

    def per_b(a):
        return pl.BlockSpec((1,) + a.shape[1:], lambda b, j, pt: (b,) + (0,) * (a.ndim - 1))

    gs = pltpu.PrefetchScalarGridSpec(
        num_scalar_prefetch=1,
        grid=(bs, nsteps),
        in_specs=[per_b(q_bd), per_b(sel), per_b(kvn)]
        + _page_specs(npg, n_pages, (PAGE * 2 * MOBA_H, HD), 0),
        out_specs=pl.BlockSpec((1, qr, HD), lambda b, j, pt: (b, 0, 0)),
        scratch_shapes=[pltpu.VMEM((qr, 1), F32), pltpu.VMEM((qr, 1), F32), pltpu.VMEM((qr, HD), F32)],
    )
    return pl.pallas_call(
        kern, out_shape=jax.ShapeDtypeStruct((bs, qr, HD), BF16), grid_spec=gs,
        compiler_params=_cp(("arbitrary", "arbitrary")), name="moba_sample",
    )(pt, q_bd, sel, kvn, *([x] * npg))


def _conv_sample_kernel(st_ref, u_ref, wdw_ref, bdw_ref, lg_ref, lb_ref, w2_ref, b2_ref, x_ref, g_ref, o_ref,
                        *, ts):
    nst = CONV_W - 1
    bs = st_ref.shape[1]
    zs = []
    for t in range(ts):
        acc = None
        for k in range(CONV_W):
            r = t + k
            row = st_ref[r] if r < nst else u_ref[r - nst]
            term = wdw_ref[k:k + 1, :] * row
            acc = term if acc is None else acc + term
        zs.append(_ln_silu(acc + bdw_ref[...], lg_ref[...], lb_ref[...]))
    z = jnp.concatenate(zs, axis=0).astype(BF16)
    out = _mm(z, w2_ref[...]) + b2_ref[...]
    for t in range(ts):
        o_ref[t] = x_ref[t] + g_ref[t] * out[t * bs:(t + 1) * bs]


def conv_sample(st_t, u_t, w_dw, b_dw, ln_g, ln_b, w_pw2, b_pw2, x_t, g_t):
    ts = u_t.shape[0]
    kern = functools.partial(_conv_sample_kernel, ts=ts)
    full = lambda a: pl.BlockSpec(a.shape, lambda i: (0,) * a.ndim)
    args = (st_t, u_t, w_dw, b_dw, ln_g, ln_b, w_pw2, b_pw2, x_t, g_t)
    return pl.pallas_call(
        kern, out_shape=jax.ShapeDtypeStruct(x_t.shape, F32), grid=(1,),
        in_specs=[full(a) for a in args], out_specs=full(x_t),
        compiler_params=_cp(("arbitrary",)), name="conv_sample",
    )(*args)


def _trunk_sample(x, mods, pp, past):
    bs, ts, d = x.shape
    m = bs * ts
    n_pages = past["page_table"].shape[1]
    plen = n_pages * PAGE
    assert ts <= TPAD and ts < CMP_STRIDE and plen % MOBA_BLOCK == 0 and m % 8 == 0
    pt = past["page_table"].reshape(-1).astype(I32)
    x2d = x.reshape(m, d)
    sh1, sc1, g1, sh2, sc2, g2 = mods[0]
    qn, kvc, kvs, kvw, qm, kvm, gate, _ = attn_in_proj(
        x2d, pp["norm_mix_g"][0:1], sh1, sc1, pp["w_main"], pp["w_gate"], m, 1)

    def pad_t(a, axis):
        w = [(0, 0)] * a.ndim
        w[axis] = (0, TPAD - ts)
        return jnp.pad(a, w)

    ck, cv = compress_sample(pt, past["nsa_cmp"], bs, n_pages, pp["wab"], pp["pe2"], pp["w2"])
    q_g = pad_t(qn.reshape(bs, ts, NSA_G, NSA_HPG, HD).transpose(0, 2, 3, 1, 4), 3)
    q_g = q_g.reshape(bs, NSA_G, NSA_HPG * TPAD, HD)
    gate_g = pad_t(gate[:, :3 * NSA_G * NSA_HPG].reshape(bs, ts, NSA_G, NSA_HPG, 3).transpose(0, 2, 3, 1, 4), 3)
    gate_g = jnp.pad(gate_g.reshape(bs, NSA_G, NSA_HPG * TPAD, 3), ((0, 0), (0, 0), (0, 0), (0, LANES - 3)))
    ksn = pad_t(kvs.reshape(bs, ts, 4 * HD), 1)
    kwn = pad_t(kvw.reshape(bs, ts, 4 * HD), 1)
    win = past["nsa_win"].reshape(-1, HD)
    o_g = nsa_sample(pt, q_g, ck, cv, ksn, kwn, win, gate_g, past["nsa_slc"], bs, n_pages, ts)
    o_nsa = o_g.reshape(bs, NSA_G, NSA_HPG, TPAD, HD)[:, :, :, :ts].transpose(0, 3, 1, 2, 4).reshape(m, -1)

    q_h = pad_t(qm.reshape(bs, ts, MOBA_H, HD).transpose(0, 2, 1, 3), 2)
    sel = moba_gate_sample(pt, q_h, past["moba"], bs, n_pages)
    eye = jnp.eye(MOBA_H, dtype=F32)
    q_bd = ((q_h * SCALE)[:, :, :, None, :] * eye[None, :, None, :, None]).astype(BF16)
    q_bd = q_bd.reshape(bs, MOBA_H * TPAD, MOBA_H * HD)
    kvn = pad_t(kvm.reshape(bs, ts, 2 * MOBA_H * HD), 1)
    o_m = moba_sample(pt, q_bd, sel.reshape(bs, MOBA_H * TPAD, LANES), kvn, past["moba"], bs, n_pages, ts)
    o_moba = o_m.reshape(bs, MOBA_H, TPAD, HD)[:, :, :ts].transpose(0, 2, 1, 3).reshape(m, -1)

    x1 = attn_out_proj(o_nsa, o_moba, pp["w_out"], x2d, g1, m, 1)
    x2 = hier_moe_block(x1, pp["norm_ffn_g"][0:1], sh2, sc2, g2, pp["moe"][0], 0, m, 1, pp["final_g"], False)
    sh1, sc1, g1, sh2, sc2, g2 = mods[1]
    u = conv_pw1_glu(x2, pp["norm_mix_g"][1:2], sh1, sc1, pp["w_pw1"], pp["b_pw1"], m, 1)
    tb = lambda a: a.reshape(bs, ts, -1).transpose(1, 0, 2)
    st = past["conv"][0]
    x3_t = conv_sample(st.transpose(1, 0, 2), tb(u), pp["w_dw"], pp["b_dw"], pp["ln_g"], pp["ln_b"],
                       pp["w_pw2"], pp["b_pw2"], tb(x2), tb(g1[0]))
    x3 = x3_t.transpose(1, 0, 2).reshape(m, d)
    y = hier_moe_block(x3, pp["norm_ffn_g"][1:2], sh2, sc2, g2, pp["moe"][1], 1, m, 1, pp["final_g"], True)
    state = (
        kvc.reshape(1, bs, ts, 2, NSA_G, HD), kvs.reshape(1, bs, ts, 2, NSA_G, HD),
        jnp.concatenate([past["nsa_win"][0][:, ts:], kvw.reshape(bs, ts, 2, NSA_G, HD)], axis=1)[None],
        kvm.reshape(1, bs, ts, 2, MOBA_H, HD),
        jnp.concatenate([st[:, ts:], u.reshape(bs, ts, -1)], axis=1)[None],
    )
    return y.reshape(bs, ts, d), state


def _mods_from(m_all, rows, expand):
    out = []
    for layer in range(m_all.shape[0]):
        parts = jnp.split(m_all[layer, rows], 6, axis=-1)
        if expand:
            parts = [jnp.repeat(a, expand, axis=0)[None] for a in parts]
        else:
            parts = [a[:, None, :] for a in parts]
        out.append(parts)
    return out


def kernel(x_prompt, x_sample, cache_nsa_cmp_kv, cache_nsa_slc_kv, state_nsa_win_kv, cache_moba_kv, state_conv, page_table, c_prompt, c_sample, norm_mix_g, norm_ffn_g, ada_w, ada_b, attn_w_in, attn_w_out, nsa_cmp_pe, nsa_cmp_w1, nsa_cmp_w2, conv_w_pw1, conv_b_pw1, conv_w_dw, conv_b_dw, conv_ln_g, conv_ln_b, conv_w_pw2, conv_b_pw2, moe_wg, moe_bg, moe_we, moe_be, moe_w_gate, moe_w_up, moe_w_down, final_norm_g):
    p = dict(norm_mix_g=norm_mix_g, norm_ffn_g=norm_ffn_g, ada_w=ada_w, ada_b=ada_b, attn_w_in=attn_w_in,
             attn_w_out=attn_w_out, nsa_cmp_pe=nsa_cmp_pe, nsa_cmp_w1=nsa_cmp_w1, nsa_cmp_w2=nsa_cmp_w2,
             conv_w_pw1=conv_w_pw1, conv_b_pw1=conv_b_pw1, conv_w_dw=conv_w_dw, conv_b_dw=conv_b_dw,
             conv_ln_g=conv_ln_g, conv_ln_b=conv_ln_b, conv_w_pw2=conv_w_pw2, conv_b_pw2=conv_b_pw2,
             moe_wg=moe_wg, moe_bg=moe_bg, moe_we=moe_we, moe_be=moe_be, moe_w_gate=moe_w_gate,
             moe_w_up=moe_w_up, moe_w_down=moe_w_down, final_norm_g=final_norm_g)
    pp = _prep_params(p)
    bp = x_prompt.shape[0]
    bs, ts, d = x_sample.shape
    c_all = jnp.concatenate([c_prompt, c_sample], axis=0)
    pad = (-c_all.shape[0]) % 16
    c_all = jnp.pad(c_all, ((0, pad), (0, 0)))
    m_all = ada_params(c_all, ada_w, ada_b)
    mods_p = _mods_from(m_all, slice(0, bp), 0)
    y_p, (cmp_p, slc_p, win_p, moba_p, conv_p) = _trunk_prompt(x_prompt, mods_p, pp)
    mods_s = _mods_from(m_all, slice(bp, bp + bs), ts)
    past = dict(page_table=page_table, nsa_cmp=cache_nsa_cmp_kv, nsa_slc=cache_nsa_slc_kv,
                nsa_win=state_nsa_win_kv, moba=cache_moba_kv, conv=state_conv)
    y_s, (cmp_s, slc_s, win_s, moba_s, conv_s) = _trunk_sample(x_sample, mods_s, pp, past)
    return (y_p, y_s, cmp_p, cmp_s, slc_p, slc_s, win_p, win_s, moba_p, moba_s, conv_p, conv_s)
```

```python
import functools

import numpy as np
import jax
import jax.numpy as jnp
from jax import lax
from jax.experimental import pallas as pl
from jax.experimental.pallas import tpu as pltpu

F32 = jnp.float32
BF16 = jnp.bfloat16
I32 = jnp.int32

HD = 128
LANES = 128
SCALE = HD ** -0.5
NSA_G = 2
NSA_HPG = 4
CMP_BLOCK = 32
CMP_STRIDE = 16
SLC_BLOCK = 64
SLC_TOPN = 16
WINDOW = 512
FORCED = 1.0e4
MOBA_H = 8
MOBA_BLOCK = 256
MOBA_TOPK = 3
CONV_W = 31
N_GROUPS = 4
EPG = 8
N_EXP = N_GROUPS * EPG
PAGE = 128
EPS = 1e-6
NEG = -1e30
TE = 256
VMEM_LIMIT = 56 * 1024 * 1024


def _cp(sem, vmem=VMEM_LIMIT):
    return pltpu.CompilerParams(dimension_semantics=sem, vmem_limit_bytes=vmem)


def _nt(a, b):
    return lax.dot_general(a, b, (((1,), (1,)), ((), ())), preferred_element_type=F32)


def _mm(a, b):
    return jnp.dot(a, b, preferred_element_type=F32)


def _split(x):
    hi = x.astype(BF16)
    lo = (x - hi.astype(F32)).astype(BF16)
    return hi, lo


def _sigmoid(x):
    return 1.0 / (1.0 + jnp.exp(-x))


def _rms_mod(x, g, shift, scale):
    y = x * lax.rsqrt(jnp.mean(x * x, axis=-1, keepdims=True) + EPS)
    return (y * g) * (1.0 + scale) + shift


def _masked_softmax(s, valid):
    sm = jnp.where(valid, s, NEG)
    mx = jnp.max(sm, axis=-1, keepdims=True)
    e = jnp.where(valid, jnp.exp(sm - mx), 0.0)
    return e / jnp.maximum(jnp.sum(e, axis=-1, keepdims=True), 1e-30)


def _ada_kernel(c_ref, w_ref, b_ref, o_ref):
    c = c_ref[...]
    s = c * _sigmoid(c)
    shi, slo = _split(s)
    whi, wlo = _split(w_ref[0])
    o_ref[0] = _mm(shi, whi) + _mm(shi, wlo) + _mm(slo, whi) + b_ref[0]


def ada_params(c_all, ada_w, ada_b):
    depth, d, n6 = ada_w.shape
    r = c_all.shape[0]
    tn = 1024 if n6 % 1024 == 0 else 512
    return pl.pallas_call(
        _ada_kernel,
        out_shape=jax.ShapeDtypeStruct((depth, r, n6), F32),
        grid=(depth, n6 // tn),
        in_specs=[
            pl.BlockSpec((r, d), lambda l, j: (0, 0)),
            pl.BlockSpec((1, d, tn), lambda l, j: (l, 0, j)),
            pl.BlockSpec((1, 1, tn), lambda l, j: (l, 0, j)),
        ],
        out_specs=pl.BlockSpec((1, r, tn), lambda l, j: (l, 0, j)),
        compiler_params=_cp(("arbitrary", "arbitrary")),
        name="ada_params",
    )(c_all, ada_w, ada_b.reshape(depth, 1, n6))


_TN = 512
_SEGS = ((0, 2), (2, 1), (3, 1), (4, 1), (5, 2), (7, 4))
_N_MAIN_TILES = 11


def _inproj_kernel(x_ref, g_ref, sh_ref, sc_ref, w_ref, wg_ref,
                   qn_ref, kvc_ref, kvs_ref, kvw_ref, qm_ref, kvm_ref, gate_ref, kvmb_ref, h_scr):
    j = pl.program_id(1)

    @pl.when(j == 0)
    def _():
        h = _rms_mod(x_ref[...], g_ref[...], sh_ref[0], sc_ref[0]).astype(BF16)
        h_scr[...] = h
        gate_ref[...] = _sigmoid(_mm(h, wg_ref[...]))

    z = _mm(h_scr[...], w_ref[...])
    outs = (qn_ref, kvc_ref, kvs_ref, kvw_ref, qm_ref, kvm_ref)
    for ref, (start, n) in zip(outs, _SEGS):
        @pl.when((j >= start) & (j < start + n))
        def _(ref=ref):
            ref[...] = z
            if ref is kvm_ref:
                kvmb_ref[...] = z.astype(BF16)


def attn_in_proj(x2d, g, shift, scale, w_main, w_gate, tm, tiles_per_mod):
    m, d = x2d.shape
    r = shift.shape[1]

    def seg_spec(start, n):
        return pl.BlockSpec((tm, _TN), lambda i, j: (i, jnp.clip(j - start, 0, n - 1)))

    out_shape = [jax.ShapeDtypeStruct((m, n * _TN), F32) for (_, n) in _SEGS]
    out_shape.append(jax.ShapeDtypeStruct((m, LANES), F32))
    out_specs = [seg_spec(s, n) for (s, n) in _SEGS]
    out_specs.append(pl.BlockSpec((tm, LANES), lambda i, j: (i, 0)))
    out_shape.append(jax.ShapeDtypeStruct((m, _SEGS[-1][1] * _TN), BF16))
    out_specs.append(seg_spec(*_SEGS[-1]))
    mod_spec = pl.BlockSpec((1, r, d), lambda i, j: (i // tiles_per_mod, 0, 0))
    return pl.pallas_call(
        _inproj_kernel,
        out_shape=out_shape,
        grid=(m // tm, _N_MAIN_TILES),
        in_specs=[
            pl.BlockSpec((tm, d), lambda i, j: (i, 0)),
            pl.BlockSpec((1, d), lambda i, j: (0, 0)),
            mod_spec, mod_spec,
            pl.BlockSpec((d, _TN), lambda i, j: (0, j)),
            pl.BlockSpec((d, LANES), lambda i, j: (0, 0)),
        ],
        out_specs=out_specs,
        scratch_shapes=[pltpu.VMEM((tm, d), BF16)],
        compiler_params=_cp(("arbitrary", "arbitrary")),
        name="attn_in_proj",
    )(x2d, g, shift, scale, w_main, w_gate)


def _gelu_tanh(x):
    return 0.5 * x * (1.0 + jnp.tanh(0.7978845608028654 * (x + 0.044715 * x * x * x)))


def _compress_groups(x, c, g, wab_ref):
    col = (c * NSA_G + g) * HD
    xs = jnp.concatenate([x[:, r * 4 * HD + col: r * 4 * HD + col + HD] for r in range(CMP_STRIDE)],
                         axis=1).astype(BF16)
    return _mm(xs, wab_ref[c])


def _compress_finish(ab, c, pe_ref, wab_ref, w2_ref):
    rows = ab.shape[0]
    pe2 = _mm(pe_ref[c], wab_ref[c])
    pe_term = pe2[0:1, :HD] + pe2[1:2, HD:]
    pre = ab[:, :HD] + pltpu.roll(ab[:, HD:], rows - 1, 0) + pe_term
    return _mm(_gelu_tanh(pre).astype(BF16), w2_ref[c])


def _compress_prompt_kernel(x_ref, wab_ref, pe_ref, w2_ref, ck_ref, cv_ref):
    x = x_ref[0]
    for c, ref in ((0, ck_ref), (1, cv_ref)):
        for g in range(NSA_G):
            ab = _compress_groups(x, c, g, wab_ref)
            ref[0, g] = _compress_finish(ab, c, pe_ref, wab_ref, w2_ref).astype(BF16)


def compress_prompt(kvc, b, t, wab, pe2, w2):
    ng = t // CMP_STRIDE
    x = kvc.reshape(b, ng, CMP_STRIDE * 4 * HD)
    out = jax.ShapeDtypeStruct((b, NSA_G, ng, HD), BF16)
    ospec = pl.BlockSpec((1, NSA_G, ng, HD), lambda i: (i, 0, 0, 0))
    return pl.pallas_call(
        _compress_prompt_kernel,
        out_shape=[out, out],
        grid=(b,),
        in_specs=[
            pl.BlockSpec((1, ng, CMP_STRIDE * 4 * HD), lambda i: (i, 0, 0)),
            pl.BlockSpec(wab.shape, lambda i: (0, 0, 0)),
            pl.BlockSpec(pe2.shape, lambda i: (0, 0, 0)),
            pl.BlockSpec(w2.shape, lambda i: (0, 0, 0)),
        ],
        out_specs=[ospec, ospec],
        compiler_params=_cp(("arbitrary",)),
        name="nsa_compress_prompt",
    )(x, wab, pe2, w2)


def _select_blocks(score, lane, n_blocks, topn):
    cnt = jnp.zeros(score.shape, F32)
    for i in range(n_blocks):
        ci = score[:, i:i + 1]
        beats = (ci > score) | ((ci == score) & (lane > i))
        cnt = cnt + beats.astype(F32)
    return (cnt < topn) & (lane < n_blocks)


def _nsa_prompt_kernel(q_ref, ck_ref, cv_ref, ks_ref, vs_ref, kw_ref, vw_ref, gate_ref, c2st_ref, ee_ref,
                       o_ref, selx_ref, m_ref, l_ref, acc_ref, *, tq, tk, t_len, n_cmp, n_slc):
    g = pl.program_id(1)
    q0 = pl.program_id(2) * tq
    q = q_ref[...] * SCALE
    qh = [q[:, h * HD:(h + 1) * HD].astype(BF16) for h in range(NSA_HPG)]
    trow = q0 + lax.broadcasted_iota(I32, (tq, 1), 0)
    lane = lax.broadcasted_iota(I32, (1, LANES), 1)

    valid = (lane * CMP_STRIDE + (CMP_BLOCK - 1) <= trow) & (lane < n_cmp)
    p_h = [_masked_softmax(_nt(qh[h], ck_ref[0, 0]), valid) for h in range(NSA_HPG)]
    o_cmp = [_mm(p_h[h].astype(BF16), cv_ref[0, 0]) for h in range(NSA_HPG)]

    psum = p_h[0] + p_h[1] + p_h[2] + p_h[3]
    phi, plo = _split(psum)
    imp_t = _nt(c2st_ref[...], phi) + _nt(c2st_ref[...], plo)
    ns8 = -(-n_slc // 8) * 8
    blk = lax.broadcasted_iota(I32, (ns8, 1), 0)
    tl = q0 + lax.broadcasted_iota(I32, (1, tq), 1)
    qb = tl >> 6
    forced = (blk == 0) | (blk == qb) | (blk == qb - 1)
    score = jnp.where(forced, FORCED, imp_t[0:ns8])
    score = jnp.where(blk * SLC_BLOCK <= tl, score, -1.0)
    score = jnp.where(blk < n_slc, score, -2.0)
    cnt = jnp.zeros((ns8, tq), F32)
    for i in range(n_slc):
        ci = score[i:i + 1, :]
        cnt = cnt + ((ci > score) | ((ci == score) & (blk > i))).astype(F32)
    sel_t = ((cnt < min(SLC_TOPN, n_slc)) & (blk < n_slc)).astype(F32)
    sel = jnp.concatenate([sel_t, jnp.zeros((LANES - ns8, tq), F32)], axis=0).T
    selx_ref[...] = _mm(sel.astype(BF16), ee_ref[...])

    def slc_tile(kt, first):
        k = ks_ref[kt * tk:(kt + 1) * tk, :].astype(BF16)
        v = vs_ref[kt * tk:(kt + 1) * tk, :].astype(BF16)
        kpos = kt * tk + lax.broadcasted_iota(I32, (1, tk), 1)
        ok = (selx_ref[:, kt * tk:(kt + 1) * tk] > 0.5) & (kpos <= trow)
        for h in range(NSA_HPG):
            sk = jnp.where(ok, _nt(qh[h], k), NEG)
            mx = jnp.max(sk, axis=-1, keepdims=True)
            if first:
                pk = jnp.exp(sk - mx)
                m_ref[h] = mx
                l_ref[h] = jnp.sum(pk, axis=-1, keepdims=True)
                acc_ref[h] = _mm(pk.astype(BF16), v)
            else:
                m_old = m_ref[h]
                m_new = jnp.maximum(m_old, mx)
                alpha = jnp.exp(m_old - m_new)
                pk = jnp.exp(sk - m_new)
                m_ref[h] = m_new
                l_ref[h] = alpha * l_ref[h] + jnp.sum(pk, axis=-1, keepdims=True)
                acc_ref[h] = alpha * acc_ref[h] + _mm(pk.astype(BF16), v)

    slc_tile(0, True)
    for kt in range(1, t_len // tk):
        pl.when(kt * tk <= q0 + tq - 1)(functools.partial(slc_tile, kt, False))

    span = WINDOW + tq
    start = pl.multiple_of(jnp.maximum(q0 - WINDOW, 0), LANES)
    kw = kw_ref[pl.ds(start, span), :].astype(BF16)
    vw = vw_ref[pl.ds(start, span), :].astype(BF16)
    kpos = start + lax.broadcasted_iota(I32, (1, span), 1)
    okw = (kpos <= trow) & (kpos > trow - WINDOW)

    gt = gate_ref[...]
    for h in range(NSA_HPG):
        o_win = _mm(_masked_softmax(_nt(qh[h], kw), okw).astype(BF16), vw)
        o_slc = acc_ref[h] / l_ref[h]
        gc = [jnp.where(g == 0, gt[:, h * 3 + kk:h * 3 + kk + 1],
                        gt[:, (NSA_HPG + h) * 3 + kk:(NSA_HPG + h) * 3 + kk + 1]) for kk in range(3)]
        o_ref[:, h * HD:(h + 1) * HD] = (gc[0] * o_cmp[h] + gc[1] * o_slc + gc[2] * o_win).astype(BF16)


def _cmp_to_slc(n_cmp, n_slc):
    cs = np.arange(n_cmp) * CMP_STRIDE
    ss = np.arange(n_slc) * SLC_BLOCK
    shared = (np.minimum(cs[:, None] + CMP_BLOCK, ss[None, :] + SLC_BLOCK)
              - np.maximum(cs[:, None], ss[None, :]))
    return np.clip(shared, 0, None) / CMP_STRIDE


def nsa_prompt(qn, ck, cv, kvs, kvw, gate, b, t):
    tq, tk = 256, 512
    nq = t // tq
    n_cmp = (t - CMP_BLOCK) // CMP_STRIDE + 1
    n_slc = -(-t // SLC_BLOCK)
    c2s = np.zeros((LANES, LANES), np.float32)
    c2s[:n_slc, :n_cmp] = _cmp_to_slc(n_cmp, n_slc).T
    ee = (np.arange(LANES)[:, None] == (np.arange(t)[None, :] // SLC_BLOCK)).astype(np.float32)
    kern = functools.partial(_nsa_prompt_kernel, tq=tq, tk=tk, t_len=t, n_cmp=n_cmp, n_slc=n_slc)
    return pl.pallas_call(
        kern,
        out_shape=jax.ShapeDtypeStruct((b * t, NSA_G * NSA_HPG * HD), BF16),
        grid=(b, NSA_G, nq),
        in_specs=[
            pl.BlockSpec((tq, NSA_HPG * HD), lambda i, g, q: (i * nq + q, g)),
            pl.BlockSpec((1, 1, ck.shape[2], HD), lambda i, g, q: (i, g, 0, 0)),
            pl.BlockSpec((1, 1, cv.shape[2], HD), lambda i, g, q: (i, g, 0, 0)),
            pl.BlockSpec((t, HD), lambda i, g, q: (i, g)),
            pl.BlockSpec((t, HD), lambda i, g, q: (i, NSA_G + g)),
            pl.BlockSpec((t, HD), lambda i, g, q: (i, g)),
            pl.BlockSpec((t, HD), lambda i, g, q: (i, NSA_G + g)),
            pl.BlockSpec((tq, LANES), lambda i, g, q: (i * nq + q, 0)),
            pl.BlockSpec((LANES, LANES), lambda i, g, q: (0, 0)),
            pl.BlockSpec((LANES, t), lambda i, g, q: (0, 0)),
        ],
        out_specs=pl.BlockSpec((tq, NSA_HPG * HD), lambda i, g, q: (i * nq + q, g)),
        scratch_shapes=[pltpu.VMEM((tq, t), F32), pltpu.VMEM((NSA_HPG, tq, 1), F32),
                        pltpu.VMEM((NSA_HPG, tq, 1), F32), pltpu.VMEM((NSA_HPG, tq, HD), F32)],
        compiler_params=_cp(("arbitrary", "arbitrary", "arbitrary")),
        name="nsa_prompt",
    )(qn, ck, cv, kvs, kvs, kvw, kvw, gate, jnp.asarray(c2s, BF16), jnp.asarray(ee, BF16))


MOBA_HPS = 4
KM_ROWS = 8


def _moba_kmean_kernel(k_ref, o_ref):
    n = pl.program_id(1)

    @pl.when(n == 0)
    def _():
        o_ref[...] = jnp.zeros(o_ref.shape, F32)

    o_ref[0, pl.ds(n, 1), :] = jnp.sum(k_ref[...], axis=0, keepdims=True) * (1.0 / MOBA_BLOCK)


def moba_kmean(kvm, b, t):
    nb = t // MOBA_BLOCK
    hk = MOBA_H * HD
    assert nb <= KM_ROWS
    return pl.pallas_call(
        _moba_kmean_kernel,
        out_shape=jax.ShapeDtypeStruct((b, KM_ROWS, hk), F32),
        grid=(b, nb),
        in_specs=[pl.BlockSpec((MOBA_BLOCK, hk), lambda i, n: (i * nb + n, 0))],
        out_specs=pl.BlockSpec((1, KM_ROWS, hk), lambda i, n: (i, 0, 0)),
        compiler_params=_cp(("arbitrary", "arbitrary")),
        name="moba_kmean",
    )(kvm)


def _moba_prompt_kernel(q_ref, k_ref, v_ref, km_ref, o_ref, *, t_len):
    qi = pl.program_id(2)
    nb = t_len // MOBA_BLOCK
    tq = MOBA_BLOCK
    lane = lax.broadcasted_iota(I32, (1, LANES), 1)
    blk = lax.broadcasted_iota(I32, (KM_ROWS, 1), 0)
    row = lax.broadcasted_iota(I32, (tq, 1), 0)
    col = lax.broadcasted_iota(I32, (1, tq), 1)
    d0 = pl.multiple_of(qi * MOBA_BLOCK, MOBA_BLOCK)

    qs_l, sel_l, init = [], [], []
    for hh in range(MOBA_HPS):
        cs = slice(hh * HD, (hh + 1) * HD)
        qf = q_ref[:, cs]
        qhi, qlo = _split(qf)
        khi, klo = _split(km_ref[0, :, cs])
        gt = _nt(khi, qhi) + _nt(klo, qhi) + _nt(khi, qlo)
        gt = jnp.where(blk < qi, gt, NEG)
        cnt = jnp.zeros((KM_ROWS, tq), F32)
        for i in range(nb):
            ci = gt[i:i + 1, :]
            cnt = cnt + ((ci > gt) | ((ci == gt) & (blk > i))).astype(F32)
        selt = ((cnt < min(MOBA_TOPK, nb)) & (blk < qi)).astype(F32)
        sel_l.append(jnp.concatenate([selt, jnp.zeros((LANES - KM_ROWS, tq), F32)], axis=0).T)
        qs = (qf * SCALE).astype(BF16)
        qs_l.append(qs)
        s = jnp.where(col <= row, _nt(qs, k_ref[pl.ds(d0, MOBA_BLOCK), cs]), NEG)
        m0 = jnp.max(s, axis=-1, keepdims=True)
        p0 = jnp.exp(s - m0)
        init.append((m0, jnp.sum(p0, axis=-1, keepdims=True),
                     _mm(p0.astype(BF16), v_ref[pl.ds(d0, MOBA_BLOCK), cs])))

    def body(kt, carry):
        k0 = pl.multiple_of(kt * MOBA_BLOCK, MOBA_BLOCK)
        out = []
        for hh in range(MOBA_HPS):
            cs = slice(hh * HD, (hh + 1) * HD)
            m_old, l_old, acc = carry[hh]
            on = jnp.sum(jnp.where(lane == kt, sel_l[hh], 0.0), axis=-1, keepdims=True) > 0.5
            sk = jnp.where(on, _nt(qs_l[hh], k_ref[pl.ds(k0, MOBA_BLOCK), cs]), NEG)
            m_new = jnp.maximum(m_old, jnp.max(sk, axis=-1, keepdims=True))
            alpha = jnp.exp(m_old - m_new)
            pk = jnp.exp(sk - m_new)
            out.append((m_new, alpha * l_old + jnp.sum(pk, axis=-1, keepdims=True),
                        alpha * acc + _mm(pk.astype(BF16), v_ref[pl.ds(k0, MOBA_BLOCK), cs])))
        return tuple(out)

    fin = lax.fori_loop(0, qi, body, tuple(init))
    for hh in range(MOBA_HPS):
        _, l_fin, acc = fin[hh]
        o_ref[:, hh * HD:(hh + 1) * HD] = (acc / l_fin).astype(BF16)


def moba_prompt(qm, kvm_bf, kmean, b, t):
    nq = t // MOBA_BLOCK
    ng = MOBA_H // MOBA_HPS
    w = MOBA_HPS * HD
    kern = functools.partial(_moba_prompt_kernel, t_len=t)
    return pl.pallas_call(
        kern,
        out_shape=jax.ShapeDtypeStruct((b * t, MOBA_H * HD), BF16),
        grid=(b, ng, nq),
        in_specs=[
            pl.BlockSpec((MOBA_BLOCK, w), lambda i, h, q: (i * nq + q, h)),
            pl.BlockSpec((t, w), lambda i, h, q: (i, h)),
            pl.BlockSpec((t, w), lambda i, h, q: (i, ng + h)),
            pl.BlockSpec((1, KM_ROWS, w), lambda i, h, q: (i, 0, h)),
        ],
        out_specs=pl.BlockSpec((MOBA_BLOCK, w), lambda i, h, q: (i * nq + q, h)),
        compiler_params=_cp(("arbitrary", "arbitrary", "arbitrary")),
        name="moba_prompt",
    )(qm, kvm_bf, kvm_bf, kmean)


def _outproj_kernel(on_ref, om_ref, w1_ref, w2_ref, x_ref, g_ref, o_ref):
    acc = _mm(on_ref[...], w1_ref[...]) + _mm(om_ref[...], w2_ref[...])
    o_ref[...] = x_ref[...] + g_ref[0] * acc


def attn_out_proj(o_nsa, o_moba, w_out, x2d, gate, tm, tiles_per_mod):
    m, d = x2d.shape
    r = gate.shape[1]
    kn = o_nsa.shape[1]
    return pl.pallas_call(
        _outproj_kernel,
        out_shape=jax.ShapeDtypeStruct((m, d), F32),
        grid=(m // tm,),
        in_specs=[
            pl.BlockSpec((tm, kn), lambda i: (i, 0)),
            pl.BlockSpec((tm, kn), lambda i: (i, 0)),
            pl.BlockSpec((kn, d), lambda i: (0, 0)),
            pl.BlockSpec((kn, d), lambda i: (1, 0)),
            pl.BlockSpec((tm, d), lambda i: (i, 0)),
            pl.BlockSpec((1, r, d), lambda i: (i // tiles_per_mod, 0, 0)),
        ],
        out_specs=pl.BlockSpec((tm, d), lambda i: (i, 0)),
        compiler_params=_cp(("arbitrary",)),
        name="attn_out_proj",
    )(o_nsa, o_moba, w_out, w_out, x2d, gate)


_GROUP_LANE0 = 64
U32 = jnp.uint32
_HI16 = 0xFFFF0000


def _pack_bf16_pairs(xb):
    half = xb.shape[1] // 2
    bits = lax.bitcast_convert_type(xb.astype(F32), U32)
    return (bits[:, half:] & jnp.uint32(_HI16)) | (bits[:, :half] >> 16)


def _unpack_bf16_pairs(xp):
    lo = lax.bitcast_convert_type(xp << 16, F32)
    hi = lax.bitcast_convert_type(xp & jnp.uint32(_HI16), F32)
    return jnp.concatenate([lo, hi], axis=1)


def _router_kernel(x_ref, g_ref, sh_ref, sc_ref, wh_ref, wl_ref, b_ref, h_ref, meta_ref, cnt_ref, carry_ref,
                   *, tm):
    i = pl.program_id(0)

    @pl.when(i == 0)
    def _():
        carry_ref[...] = jnp.zeros(carry_ref.shape, F32)

    h = _rms_mod(x_ref[...], g_ref[...], sh_ref[0], sc_ref[0])
    hhi, hlo = _split(h)
    h_ref[...] = _pack_bf16_pairs(hhi)
    lg = _mm(hhi, wh_ref[...]) + _mm(hhi, wl_ref[...]) + _mm(hlo, wh_ref[...]) + b_ref[...]
    lane = lax.broadcasted_iota(I32, (1, LANES), 1)
    lanef = lane.astype(F32)
    big = 1.0e9

    isg = (lane >= _GROUP_LANE0) & (lane < _GROUP_LANE0 + N_GROUPS)
    mxg = jnp.max(jnp.where(isg, lg, NEG), axis=-1, keepdims=True)
    grp = jnp.min(jnp.where(isg & (lg == mxg), lanef - _GROUP_LANE0, big), axis=-1, keepdims=True)
    pg = 1.0 / jnp.sum(jnp.where(isg, jnp.exp(lg - mxg), 0.0), axis=-1, keepdims=True)

    ing = (lane < N_EXP) & ((lane >> 3).astype(F32) == grp)
    l1 = jnp.max(jnp.where(ing, lg, NEG), axis=-1, keepdims=True)
    i1 = jnp.min(jnp.where(ing & (lg == l1), lanef, big), axis=-1, keepdims=True)
    ing2 = ing & (lanef != i1)
    l2 = jnp.max(jnp.where(ing2, lg, NEG), axis=-1, keepdims=True)
    i2 = jnp.min(jnp.where(ing2 & (lg == l2), lanef, big), axis=-1, keepdims=True)
    e21 = jnp.exp(l2 - l1)
    w1 = pg / (1.0 + e21)
    w2 = pg * e21 / (1.0 + e21)

    oh1 = lanef == i1
    oh2 = lanef == i2
    oh = (oh1 | oh2).astype(F32)
    r_i = lax.broadcasted_iota(I32, (tm, tm), 0)
    c_i = lax.broadcasted_iota(I32, (tm, tm), 1)
    lower = (c_i < r_i).astype(BF16)
    pref = _mm(lower, oh.astype(BF16)) + carry_ref[0:1, :]
    r1 = jnp.sum(jnp.where(oh1, pref, 0.0), axis=-1, keepdims=True)
    r2 = jnp.sum(jnp.where(oh2, pref, 0.0), axis=-1, keepdims=True)
    carry_ref[0:1, :] = carry_ref[0:1, :] + jnp.sum(oh, axis=0, keepdims=True)
    cnt_ref[...] = carry_ref[...]

    meta = jnp.where(lane == 0, i1, 0.0) + jnp.where(lane == 1, i2, 0.0) + jnp.where(lane == 2, r1, 0.0) \
        + jnp.where(lane == 3, r2, 0.0) + jnp.where(lane == 4, w1, 0.0) + jnp.where(lane == 5, w2, 0.0)
    meta_ref[...] = meta


def moe_router(x2d, g, shift, scale, wr_hi, wr_lo, br, tm, tiles_per_mod):
    m, d = x2d.shape
    r = shift.shape[1]
    mod_spec = pl.BlockSpec((1, r, d), lambda i: (i // tiles_per_mod, 0, 0))
    kern = functools.partial(_router_kernel, tm=tm)
    return pl.pallas_call(
        kern,
        out_shape=[jax.ShapeDtypeStruct((m, d // 2), U32), jax.ShapeDtypeStruct((m, LANES), F32),
                   jax.ShapeDtypeStruct((8, LANES), F32)],
        grid=(m // tm,),
        in_specs=[
            pl.BlockSpec((tm, d), lambda i: (i, 0)),
            pl.BlockSpec((1, d), lambda i: (0, 0)),
            mod_spec, mod_spec,
            pl.BlockSpec((d, LANES), lambda i: (0, 0)),
            pl.BlockSpec((d, LANES), lambda i: (0, 0)),
            pl.BlockSpec((1, LANES), lambda i: (0, 0)),
        ],
        out_specs=[pl.BlockSpec((tm, d // 2), lambda i: (i, 0)), pl.BlockSpec((tm, LANES), lambda i: (i, 0)),
                   pl.BlockSpec((8, LANES), lambda i: (0, 0))],
        scratch_shapes=[pltpu.VMEM((8, LANES), F32)],
        compiler_params=_cp(("arbitrary",)),
        name="moe_router",
    )(x2d, g, shift, scale, wr_hi, wr_lo, br)


def _scatter_kernel(seg_ref, has_ref, nu_ref, dest_ref, h_ref, xs_ref, zbuf, hbuf, sem, zsem,
                    *, tm, n_tiles, n_steps):
    i = pl.program_id(0)

    def zero_copy(row0):
        return pltpu.make_async_copy(zbuf, xs_ref.at[pl.ds(pl.multiple_of(row0, TE), TE), :], zsem)

    @pl.when(i == 0)
    def _():
        zbuf[...] = jnp.zeros(zbuf.shape, zbuf.dtype)
        for e in range(N_EXP):
            pl.when(has_ref[e] > 0)(lambda e=e: zero_copy(seg_ref[e]).start())

        def tail_start(tl, c):
            zero_copy(tl * TE).start()
            return c

        def tail_wait(tl, c):
            zero_copy(tl * TE).wait()
            return c

        lax.fori_loop(nu_ref[0], n_tiles, tail_start, 0)
        for e in range(N_EXP):
            pl.when(has_ref[e] > 0)(lambda e=e: zero_copy(seg_ref[e]).wait())
        lax.fori_loop(nu_ref[0], n_tiles, tail_wait, 0)

    slot = i % 2
    hbuf[slot] = h_ref[...]

    def row_copy(s, r, k):
        return pltpu.make_async_copy(hbuf.at[s, pl.ds(r, 1), :], xs_ref.at[pl.ds(dest_ref[2 * r + k], 1), :],
                                     sem.at[s])

    def start(r, c):
        row_copy(slot, r, 0).start(priority=0)
        row_copy(slot, r, 1).start(priority=1)
        return c

    def wait_slot(s):
        def wait(r, c):
            row_copy(s, r, 0).wait()
            row_copy(s, r, 1).wait()
            return c
        lax.fori_loop(0, tm, wait, 0, unroll=8)

    lax.fori_loop(0, tm, start, 0, unroll=8)
    pl.when(i > 0)(functools.partial(wait_slot, 1 - slot))
    pl.when(i == n_steps - 1)(functools.partial(wait_slot, slot))


def moe_scatter(h, dest_flat, seg_last, seg_has, n_used, n_rows, tm):
    m, d = h.shape
    kern = functools.partial(_scatter_kernel, tm=tm, n_tiles=n_rows // TE, n_steps=m // tm)
    gs = pltpu.PrefetchScalarGridSpec(
        num_scalar_prefetch=3,
        grid=(m // tm,),
        in_specs=[
            pl.BlockSpec((2 * tm,), lambda i, *_: (i,), memory_space=pltpu.SMEM),
            pl.BlockSpec((tm, d), lambda i, *_: (i, 0)),
        ],
        out_specs=pl.BlockSpec(memory_space=pl.ANY),
        scratch_shapes=[pltpu.VMEM((TE, d), h.dtype), pltpu.VMEM((2, tm, d), h.dtype),
                        pltpu.SemaphoreType.DMA((2,)), pltpu.SemaphoreType.DMA(())],
    )
    return pl.pallas_call(
        kern,
        out_shape=jax.ShapeDtypeStruct((n_rows, d), h.dtype),
        grid_spec=gs,
        compiler_params=_cp(("arbitrary",)),
        name="moe_scatter",
    )(seg_last, seg_has, n_used, dest_flat, h)


def _expert_kernel(te_ref, nu_ref, xs_ref, wg_ref, wu_ref, wd_ref, ys_ref):
    i = pl.program_id(0)

    @pl.when(i < nu_ref[0])
    def _():
        xb = _unpack_bf16_pairs(xs_ref[...]).astype(BF16)
        a = _mm(xb, wg_ref[0, 0].astype(BF16))
        u = _mm(xb, wu_ref[0, 0].astype(BF16))
        hid = (a * _sigmoid(a) * u).astype(BF16)
        ys_ref[...] = _pack_bf16_pairs(_mm(hid, wd_ref[0, 0].astype(BF16)).astype(BF16))

    @pl.when(i >= nu_ref[0])
    def _():
        ys_ref[...] = jnp.zeros(ys_ref.shape, U32)


def moe_experts(xs, tile_expert, n_used, w_gate, w_up, w_down, layer):
    n_rows, dh = xs.shape
    d = 2 * dh
    f = w_gate.shape[-1]
    nt = n_rows // TE
    gs = pltpu.PrefetchScalarGridSpec(
        num_scalar_prefetch=2,
        grid=(nt,),
        in_specs=[
            pl.BlockSpec((TE, dh), lambda i, te, nu: (jnp.minimum(i, nu[0] - 1), 0)),
            pl.BlockSpec((1, 1, d, f), lambda i, te, nu: (layer, te[i], 0, 0)),
            pl.BlockSpec((1, 1, d, f), lambda i, te, nu: (layer, te[i], 0, 0)),
            pl.BlockSpec((1, 1, f, d), lambda i, te, nu: (layer, te[i], 0, 0)),
        ],
        out_specs=pl.BlockSpec((TE, dh), lambda i, te, nu: (i, 0)),
    )
    return pl.pallas_call(
        _expert_kernel,
        out_shape=jax.ShapeDtypeStruct((n_rows, dh), U32),
        grid_spec=gs,
        compiler_params=_cp(("arbitrary",)),
        name="moe_experts",
    )(tile_expert, n_used, xs, w_gate, w_up, w_down)


def _combine_kernel(dcur_ref, dnext_ref, x_ref, g_ref, meta_ref, fg_ref, ys_ref, o_ref, rows, sem,
                    *, tm, final, n_steps):
    i = pl.program_id(0)
    slot = i % 2

    def row_copy(dref, s, r, k):
        return pltpu.make_async_copy(ys_ref.at[pl.ds(dref[2 * r + k], 1), :],
                                     rows.at[s, k, pl.ds(r, 1), :], sem.at[s])

    def gather(dref, s):
        def start(r, c):
            row_copy(dref, s, r, 0).start(priority=0)
            row_copy(dref, s, r, 1).start(priority=1)
            return c
        lax.fori_loop(0, tm, start, 0, unroll=8)

    pl.when(i == 0)(functools.partial(gather, dcur_ref, 0))
    pl.when(i + 1 < n_steps)(functools.partial(gather, dnext_ref, 1 - slot))

    def wait(r, c):
        row_copy(dcur_ref, slot, r, 0).wait()
        row_copy(dcur_ref, slot, r, 1).wait()
        return c

    lax.fori_loop(0, tm, wait, 0, unroll=8)
    meta = meta_ref[...]
    y = meta[:, 4:5] * _unpack_bf16_pairs(rows[slot, 0]) + meta[:, 5:6] * _unpack_bf16_pairs(rows[slot, 1])
    x = x_ref[...] + g_ref[0] * y
    if final:
        x = (x * lax.rsqrt(jnp.mean(x * x, axis=-1, keepdims=True) + EPS)) * fg_ref[...]
    o_ref[...] = x


def moe_combine(x2d, gate, meta, final_g, ys, dest_flat, tm, tiles_per_mod, final):
    m, d = x2d.shape
    r = gate.shape[1]
    n_steps = m // tm
    kern = functools.partial(_combine_kernel, tm=tm, final=final, n_steps=n_steps)
    return pl.pallas_call(
        kern,
        out_shape=jax.ShapeDtypeStruct((m, d), F32),
        grid=(n_steps,),
        in_specs=[
            pl.BlockSpec((2 * tm,), lambda i: (i,), memory_space=pltpu.SMEM),
            pl.BlockSpec((2 * tm,), lambda i: (jnp.minimum(i + 1, n_steps - 1),), memory_space=pltpu.SMEM),
            pl.BlockSpec((tm, d), lambda i: (i, 0)),
            pl.BlockSpec((1, r, d), lambda i: (i // tiles_per_mod, 0, 0)),
            pl.BlockSpec((tm, LANES), lambda i: (i, 0)),
            pl.BlockSpec((1, d), lambda i: (0, 0)),
            pl.BlockSpec(memory_space=pl.ANY),
        ],
        out_specs=pl.BlockSpec((tm, d), lambda i: (i, 0)),
        scratch_shapes=[pltpu.VMEM((2, 2, tm, d // 2), U32), pltpu.SemaphoreType.DMA((2,))],
        compiler_params=_cp(("arbitrary",)),
        name="moe_combine",
    )(dest_flat, dest_flat, x2d, gate, meta, final_g, ys)


def hier_moe_block(x2d, g_ffn, shift, scale, gate, p_layer, layer, tm, tiles_per_mod, final_g, final):
    m, d = x2d.shape
    wr_hi, wr_lo, br, w_gate, w_up, w_down = p_layer
    h, meta, cnt = moe_router(x2d, g_ffn, shift, scale, wr_hi, wr_lo, br, tm, tiles_per_mod)
    counts = cnt[0, :N_EXP].astype(I32)
    padded = ((counts + TE - 1) // TE) * TE
    ends = jnp.cumsum(padded)
    offs = ends - padded
    eid = meta[:, 0:2].astype(I32)
    dest = (offs[eid] + meta[:, 2:4].astype(I32)).reshape(-1)
    n_tiles = (2 * m) // TE + N_EXP
    n_used = (ends[-1] // TE).astype(I32).reshape(1)
    tile_start = jnp.arange(n_tiles, dtype=I32) * TE
    tile_clamped = jnp.minimum(tile_start, ends[-1] - 1)
    tile_expert = jnp.minimum(jnp.sum((tile_clamped[:, None] >= ends[None, :]).astype(I32), axis=1), N_EXP - 1)
    seg_last = jnp.maximum(ends - TE, 0).astype(I32)
    seg_has = (counts > 0).astype(I32)
    xs = moe_scatter(h, dest, seg_last, seg_has, n_used, n_tiles * TE, tm)
    ys = moe_experts(xs, tile_expert, n_used, w_gate, w_up, w_down, layer)
    return moe_combine(x2d, gate, meta, final_g, ys, dest, tm, tiles_per_mod, final)


def _pw1_kernel(x_ref, g_ref, sh_ref, sc_ref, wa_ref, wb_ref, ba_ref, bb_ref, u_ref, h_scr):
    @pl.when(pl.program_id(1) == 0)
    def _():
        h_scr[...] = _rms_mod(x_ref[...], g_ref[...], sh_ref[0], sc_ref[0]).astype(BF16)

    h = h_scr[...]
    a = _mm(h, wa_ref[...]) + ba_ref[...]
    b = _mm(h, wb_ref[...]) + bb_ref[...]
    u_ref[...] = a * _sigmoid(b)


def conv_pw1_glu(x2d, g, shift, scale, w_pw1, b_pw1, tm, tiles_per_mod):
    m, d = x2d.shape
    r = shift.shape[1]
    dc = w_pw1.shape[1] // 2
    tn = 512 if dc % 512 == 0 else 256
    nj = dc // tn
    mod_spec = pl.BlockSpec((1, r, d), lambda i, j: (i // tiles_per_mod, 0, 0))
    return pl.pallas_call(
        _pw1_kernel,
        out_shape=jax.ShapeDtypeStruct((m, dc), F32),
        grid=(m // tm, nj),
        in_specs=[
            pl.BlockSpec((tm, d), lambda i, j: (i, 0)),
            pl.BlockSpec((1, d), lambda i, j: (0, 0)),
            mod_spec, mod_spec,
            pl.BlockSpec((d, tn), lambda i, j: (0, j)),
            pl.BlockSpec((d, tn), lambda i, j: (0, nj + j)),
            pl.BlockSpec((1, tn), lambda i, j: (0, j)),
            pl.BlockSpec((1, tn), lambda i, j: (0, nj + j)),
        ],
        out_specs=pl.BlockSpec((tm, tn), lambda i, j: (i, j)),
        scratch_shapes=[pltpu.VMEM((tm, d), BF16)],
        compiler_params=_cp(("arbitrary", "arbitrary")),
        name="conv_pw1_glu",
    )(x2d, g, shift, scale, w_pw1, w_pw1, b_pw1, b_pw1)


_TAIL = 32


def _ln_silu(y, g, b):
    yc = y - jnp.mean(y, axis=-1, keepdims=True)
    z = yc * lax.rsqrt(jnp.mean(yc * yc, axis=-1, keepdims=True) + EPS)
    z = z * g + b
    return z * _sigmoid(z)


_CONV_CH = 512


def _conv_prompt_kernel(u_ref, wdw_ref, bdw_ref, lg_ref, lb_ref, w2_ref, b2_ref, x_ref, g_ref, o_ref, buf, sh,
                        acc_scr, *, tt):
    ti = pl.program_id(1)

    @pl.when(ti == 0)
    def _():
        buf[0:_TAIL, :] = jnp.zeros((_TAIL, buf.shape[1]), F32)

    buf[_TAIL:_TAIL + tt, :] = u_ref[...]
    off = _TAIL - (CONV_W - 1)
    sub = 8
    span = tt + _TAIL - sub
    dc = buf.shape[1]
    ch = sh.shape[2]
    for c0 in range(0, dc, ch):
        cs = slice(c0, c0 + ch)
        for r in range(1, sub):
            sh[r, 0:span, :] = buf[r:r + span, cs]
        acc = None
        for k in range(CONV_W):
            a, r = divmod(off + k, sub)
            src = buf[sub * a:sub * a + tt, cs] if r == 0 else sh[r, sub * a:sub * a + tt, :]
            term = wdw_ref[k:k + 1, cs] * src
            acc = term if acc is None else acc + term
        acc_scr[:, cs] = acc
    tail = buf[tt:tt + _TAIL, :]
    buf[0:_TAIL, :] = tail
    z = _ln_silu(acc_scr[...] + bdw_ref[...], lg_ref[...], lb_ref[...]).astype(BF16)
    out = _mm(z, w2_ref[...]) + b2_ref[...]
    o_ref[...] = x_ref[...] + g_ref[0] * out


def conv_prompt(u, w_dw, b_dw, ln_g, ln_b, w_pw2, b_pw2, x2d, gate, b, t):
    m, d = x2d.shape
    dc = u.shape[1]
    tt = 256
    nt = t // tt
    kern = functools.partial(_conv_prompt_kernel, tt=tt)
    vec = lambda n: pl.BlockSpec((1, n), lambda i, j: (0, 0))
    return pl.pallas_call(
        kern,
        out_shape=jax.ShapeDtypeStruct((m, d), F32),
        grid=(b, nt),
        in_specs=[
            pl.BlockSpec((tt, dc), lambda i, j: (i * nt + j, 0)),
            pl.BlockSpec((_TAIL, dc), lambda i, j: (0, 0)),
            vec(dc), vec(dc), vec(dc),
            pl.BlockSpec((dc, d), lambda i, j: (0, 0)),
            vec(d),
            pl.BlockSpec((tt, d), lambda i, j: (i * nt + j, 0)),
            pl.BlockSpec((1, 1, d), lambda i, j: (i, 0, 0)),
        ],
        out_specs=pl.BlockSpec((tt, d), lambda i, j: (i * nt + j, 0)),
        scratch_shapes=[pltpu.VMEM((_TAIL + tt, dc), F32), pltpu.VMEM((8, _TAIL + tt, min(_CONV_CH, dc)), F32),
                        pltpu.VMEM((tt, dc), F32)],
        compiler_params=_cp(("arbitrary", "arbitrary")),
        name="conv_prompt",
    )(u, w_dw, b_dw, ln_g, ln_b, w_pw2, b_pw2, x2d, gate)


def _prep_params(p):
    d = p["attn_w_in"].shape[1]
    nsa_q = NSA_G * NSA_HPG * HD
    nsa_kv = 2 * NSA_G * HD
    gate_w = 3 * NSA_G * NSA_HPG
    c_gate = nsa_q + 3 * nsa_kv
    w_in = p["attn_w_in"][0]
    w_main = jnp.concatenate([w_in[:, :c_gate], w_in[:, c_gate + gate_w:]], axis=1).astype(BF16)
    w_gate = jnp.pad(w_in[:, c_gate:c_gate + gate_w], ((0, 0), (0, LANES - gate_w))).astype(BF16)
    w1 = p["nsa_cmp_w1"][0]
    wab = jnp.concatenate([w1[:, :CMP_STRIDE].reshape(2, CMP_STRIDE * HD, HD),
                           w1[:, CMP_STRIDE:].reshape(2, CMP_STRIDE * HD, HD)], axis=2).astype(BF16)
    pe = p["nsa_cmp_pe"][0].reshape(2, 2, CMP_STRIDE * HD)
    pe2 = jnp.pad(pe, ((0, 0), (0, 14), (0, 0))).astype(BF16)
    w2 = p["nsa_cmp_w2"][0].astype(BF16)
    moe = []
    for layer in range(p["moe_wg"].shape[0]):
        wr = jnp.zeros((d, LANES), F32)
        wr = wr.at[:, :N_EXP].set(p["moe_we"][layer]).at[:, _GROUP_LANE0:_GROUP_LANE0 + N_GROUPS].set(
            p["moe_wg"][layer])
        br = jnp.zeros((1, LANES), F32)
        br = br.at[0, :N_EXP].set(p["moe_be"][layer]).at[0, _GROUP_LANE0:_GROUP_LANE0 + N_GROUPS].set(
            p["moe_bg"][layer])
        hi, lo = _split(wr)
        moe.append((hi, lo, br, p["moe_w_gate"], p["moe_w_up"], p["moe_w_down"]))
    w_dw = jnp.pad(p["conv_w_dw"][0], ((0, _TAIL - CONV_W), (0, 0)))
    return dict(
        w_main=w_main, w_gate=w_gate, wab=wab, pe2=pe2, w2=w2,
        w_out=p["attn_w_out"][0].astype(BF16), moe=moe,
        w_pw1=p["conv_w_pw1"][0].astype(BF16), b_pw1=p["conv_b_pw1"][0][None, :],
        w_dw=w_dw, b_dw=p["conv_b_dw"][0][None, :], ln_g=p["conv_ln_g"][0][None, :],
        ln_b=p["conv_ln_b"][0][None, :], w_pw2=p["conv_w_pw2"][0].astype(BF16),
        b_pw2=p["conv_b_pw2"][0][None, :],
        norm_mix_g=p["norm_mix_g"], norm_ffn_g=p["norm_ffn_g"], final_g=p["final_norm_g"][None, :],
    )


def _trunk_prompt(x, mods, pp):
    b, t, d = x.shape
    m = b * t
    x2d = x.reshape(m, d)
    sh1, sc1, g1, sh2, sc2, g2 = mods[0]
    tm = 512
    qn, kvc, kvs, kvw, qm, kvm, gate, kvm_bf = attn_in_proj(
        x2d, pp["norm_mix_g"][0:1], sh1, sc1, pp["w_main"], pp["w_gate"], tm, t // tm)
    ck, cv = compress_prompt(kvc, b, t, pp["wab"], pp["pe2"], pp["w2"])
    o_nsa = nsa_prompt(qn, ck, cv, kvs, kvw, gate, b, t)
    o_moba = moba_prompt(qm, kvm_bf, moba_kmean(kvm, b, t), b, t)
    tm2 = 256
    x1 = attn_out_proj(o_nsa, o_moba, pp["w_out"], x2d, g1, tm2, t // tm2)
    x2 = hier_moe_block(x1, pp["norm_ffn_g"][0:1], sh2, sc2, g2, pp["moe"][0], 0, tm2, t // tm2,
                        pp["final_g"], False)
    sh1, sc1, g1, sh2, sc2, g2 = mods[1]
    u = conv_pw1_glu(x2, pp["norm_mix_g"][1:2], sh1, sc1, pp["w_pw1"], pp["b_pw1"], tm, t // tm)
    x3 = conv_prompt(u, pp["w_dw"], pp["b_dw"], pp["ln_g"], pp["ln_b"], pp["w_pw2"], pp["b_pw2"], x2, g1, b, t)
    y = hier_moe_block(x3, pp["norm_ffn_g"][1:2], sh2, sc2, g2, pp["moe"][1], 1, tm2, t // tm2,
                       pp["final_g"], True)
    wlen = min(WINDOW, t)
    state = (
        kvc.reshape(1, b, t, 2, NSA_G, HD), kvs.reshape(1, b, t, 2, NSA_G, HD),
        kvw.reshape(b, t, 2, NSA_G, HD)[None, :, t - wlen:], kvm.reshape(1, b, t, 2, MOBA_H, HD),
        u.reshape(b, t, -1)[None, :, t - (CONV_W - 1):],
    )
    return y.reshape(b, t, d), state


TPAD = 8


def _page_specs(npg, n_pages, block, col_block):
    nd = len(block)

    def spec(i):
        def imap(b, j, pt):
            return (pt[b * n_pages + j * npg + i],) + (0,) * (nd - 2) + (col_block,)
        return pl.BlockSpec(block, imap)

    return [spec(i) for i in range(npg)]


def _compress_sample_kernel(pt_ref, *refs, npg, nsteps):
    pages = refs[:npg]
    wab_ref, pe_ref, w2_ref, ck_ref, cv_ref, ab_scr = refs[npg:]
    j = pl.program_id(1)
    rows = npg * (PAGE // CMP_STRIDE)
    r0 = pl.multiple_of(j * rows, rows)
    gpp = PAGE // CMP_STRIDE
    rpt = 2 * NSA_G
    for c in range(2):
        for g in range(NSA_G):
            cg = c * NSA_G + g
            xs = jnp.concatenate(
                [jnp.concatenate([pg[pl.ds(r * rpt + cg, gpp, stride=CMP_STRIDE * rpt), :]
                                  for r in range(CMP_STRIDE)], axis=1) for pg in pages], axis=0).astype(BF16)
            ab_scr[cg, pl.ds(r0, rows), :] = _mm(xs, wab_ref[c])

    @pl.when(j == nsteps - 1)
    def _():
        for c, ref in ((0, ck_ref), (1, cv_ref)):
            for g in range(NSA_G):
                ref[0, g] = _compress_finish(ab_scr[c * NSA_G + g], c, pe_ref, wab_ref, w2_ref).astype(BF16)


def compress_sample(pt, cache, bs, n_pages, wab, pe2, w2):
    npg = 32
    nsteps = n_pages // npg
    gpp = PAGE // CMP_STRIDE
    ng = n_pages * gpp
    x = cache.reshape(-1, HD)
    kern = functools.partial(_compress_sample_kernel, npg=npg, nsteps=nsteps)
    out = jax.ShapeDtypeStruct((bs, NSA_G, ng, HD), BF16)
    ospec = pl.BlockSpec((1, NSA_G, ng, HD), lambda b, j, pt: (b, 0, 0, 0))
    const = lambda a: pl.BlockSpec(a.shape, lambda b, j, pt: (0,) * a.ndim)
    gs = pltpu.PrefetchScalarGridSpec(
        num_scalar_prefetch=1,
        grid=(bs, nsteps),
        in_specs=_page_specs(npg, n_pages, (PAGE * 2 * NSA_G, HD), 0) + [const(wab), const(pe2), const(w2)],
        out_specs=[ospec, ospec],
        scratch_shapes=[pltpu.VMEM((2 * NSA_G, ng, 2 * HD), F32)],
    )
    return pl.pallas_call(
        kern, out_shape=[out, out], grid_spec=gs,
        compiler_params=_cp(("arbitrary", "arbitrary")), name="nsa_compress_sample",
    )(pt, *([x] * npg), wab, pe2, w2)


def _nsa_sample_kernel(pt_ref, q_ref, ck_ref, cv_ref, ksn_ref, kwn_ref, win_ref, gate_ref, c2s_ref, *refs,
                       npg, nsteps, past, ts, n_cmp, n_slc):
    pages = refs[:npg]
    o_ref, selx_scr, m_scr, l_scr, acc_scr, part_scr = refs[npg:]
    j = pl.program_id(1)
    qr = NSA_HPG * TPAD
    keys = npg * PAGE
    rpt = 2 * NSA_G
    wbuf = win_ref.shape[0] // rpt
    t_row = past + (lax.broadcasted_iota(I32, (qr, 1), 0) & (TPAD - 1))
    lane = lax.broadcasted_iota(I32, (1, LANES), 1)

    @pl.when(j == 0)
    def _():
        t8 = past + lax.broadcasted_iota(I32, (TPAD, 1), 0)
        for g in range(NSA_G):
            q = (q_ref[0, g] * SCALE).astype(BF16)
            gt = gate_ref[0, g]
            ncg = ck_ref.shape[2]
            lane_c = lax.broadcasted_iota(I32, (1, ncg), 1)
            valid = (lane_c * CMP_STRIDE + (CMP_BLOCK - 1) <= t_row) & (lane_c < n_cmp)
            p = _masked_softmax(_nt(q, ck_ref[0, g]), valid)
            o_cmp = _mm(p.astype(BF16), cv_ref[0, g])
            psum = p[0:TPAD] + p[TPAD:2 * TPAD] + p[2 * TPAD:3 * TPAD] + p[3 * TPAD:4 * TPAD]
            phi, plo = _split(psum)
            imp = _mm(phi, c2s_ref[...]) + _mm(plo, c2s_ref[...])
            nl = c2s_ref.shape[1]
            lane_s = lax.broadcasted_iota(I32, (1, nl), 1)
            qb = t8 >> 6
            forced = (lane_s == 0) | (lane_s == qb) | (lane_s == qb - 1)
            score = jnp.where(forced, FORCED, imp)
            score = jnp.where(lane_s * SLC_BLOCK <= t8, score, -1.0)
            score = jnp.where(lane_s < n_slc, score, -2.0)
            sel = _select_blocks(score, lane_s, n_slc, min(SLC_TOPN, n_slc)).astype(BF16)
            bps = keys // SLC_BLOCK
            for jj in range(nsteps):
                blk = lax.broadcasted_iota(I32, (nl, keys), 0)
                key = lax.broadcasted_iota(I32, (nl, keys), 1)
                ee = (blk == jj * bps + (key >> 6)).astype(BF16)
                selx_scr[g, jj] = _mm(sel, ee)
            own = jnp.sum(jnp.where(lane_s == (past >> 6), sel.astype(F32), 0.0), axis=-1, keepdims=True)
            own4 = jnp.concatenate([own] * NSA_HPG, axis=0) > 0.5
            zpad = jnp.zeros((LANES - TPAD, HD), F32)
            kw = jnp.concatenate([win_ref[pl.ds(g, wbuf, stride=rpt), :],
                                  kwn_ref[0, :, g * HD:(g + 1) * HD], zpad], axis=0).astype(BF16)
            vw = jnp.concatenate([win_ref[pl.ds(NSA_G + g, wbuf, stride=rpt), :],
                                  kwn_ref[0, :, (NSA_G + g) * HD:(NSA_G + g + 1) * HD], zpad], axis=0).astype(BF16)
            idx = lax.broadcasted_iota(I32, (1, wbuf + LANES), 1)
            kpos = past - wbuf + idx
            okw = (idx < wbuf + ts) & (kpos <= t_row) & (kpos > t_row - WINDOW)
            o_win = _mm(_masked_softmax(_nt(q, kw), okw).astype(BF16), vw)
            part_scr[g] = gt[:, 0:1] * o_cmp + gt[:, 2:3] * o_win
            kn = jnp.concatenate([ksn_ref[0, :, g * HD:(g + 1) * HD], zpad], axis=0).astype(BF16)
            vn = jnp.concatenate([ksn_ref[0, :, (NSA_G + g) * HD:(NSA_G + g + 1) * HD], zpad], axis=0).astype(BF16)
            ok0 = (past + lane <= t_row) & (lane < ts) & own4
            s0 = jnp.where(ok0, _nt(q, kn), NEG)
            m0 = jnp.max(s0, axis=-1, keepdims=True)
            p0 = jnp.where(ok0, jnp.exp(s0 - m0), 0.0)
            m_scr[g] = m0
            l_scr[g] = jnp.sum(p0, axis=-1, keepdims=True)
            acc_scr[g] = _mm(p0.astype(BF16), vn)

    for g in range(NSA_G):
        q = (q_ref[0, g] * SCALE).astype(BF16)
        k = jnp.concatenate([pg[pl.ds(g, PAGE, stride=rpt), :] for pg in pages], axis=0).astype(BF16)
        v = jnp.concatenate([pg[pl.ds(NSA_G + g, PAGE, stride=rpt), :] for pg in pages], axis=0).astype(BF16)
        mk = selx_scr[g, j]
        ok = jnp.concatenate([mk] * NSA_HPG, axis=0) > 0.5
        s = jnp.where(ok, _nt(q, k), NEG)
        m_old = m_scr[g]
        m_new = jnp.maximum(m_old, jnp.max(s, axis=-1, keepdims=True))
        alpha = jnp.exp(m_old - m_new)
        pk = jnp.where(ok, jnp.exp(s - m_new), 0.0)
        m_scr[g] = m_new
        l_scr[g] = alpha * l_scr[g] + jnp.sum(pk, axis=-1, keepdims=True)
        acc_scr[g] = alpha * acc_scr[g] + _mm(pk.astype(BF16), v)

    @pl.when(j == nsteps - 1)
    def _():
        for g in range(NSA_G):
            o_slc = acc_scr[g] / jnp.maximum(l_scr[g], 1e-30)
            o_ref[0, g] = (part_scr[g] + gate_ref[0, g][:, 1:2] * o_slc).astype(BF16)


def nsa_sample(pt, q_g, ck, cv, ksn, kwn, win, gate_g, cache, bs, n_pages, ts):
    npg = 8
    nsteps = n_pages // npg
    past = n_pages * PAGE
    n_cmp = (past + ts - CMP_BLOCK) // CMP_STRIDE + 1
    n_slc = -(-(past + ts) // SLC_BLOCK)
    ncg = ck.shape[2]
    nl = -(-n_slc // LANES) * LANES
    c2s = np.zeros((ncg, nl), np.float32)
    c2s[:n_cmp, :n_slc] = _cmp_to_slc(n_cmp, n_slc)
    c2s = jnp.asarray(c2s, BF16)
    qr = NSA_HPG * TPAD
    keys = npg * PAGE
    rpt = 2 * NSA_G
    x = cache.reshape(-1, HD)
    kern = functools.partial(_nsa_sample_kernel, npg=npg, nsteps=nsteps, past=past, ts=ts, n_cmp=n_cmp,
                             n_slc=n_slc)

    def per_b(a):
        return pl.BlockSpec((1,) + a.shape[1:], lambda b, j, pt: (b,) + (0,) * (a.ndim - 1))

    win_rows = win.shape[0] // bs
    gs = pltpu.PrefetchScalarGridSpec(
        num_scalar_prefetch=1,
        grid=(bs, nsteps),
        in_specs=[per_b(q_g), per_b(ck), per_b(cv), per_b(ksn), per_b(kwn),
                  pl.BlockSpec((win_rows, HD), lambda b, j, pt: (b, 0)), per_b(gate_g),
                  pl.BlockSpec(c2s.shape, lambda b, j, pt: (0, 0))]
        + _page_specs(npg, n_pages, (PAGE * rpt, HD), 0),
        out_specs=pl.BlockSpec((1, NSA_G, qr, HD), lambda b, j, pt: (b, 0, 0, 0)),
        scratch_shapes=[pltpu.VMEM((NSA_G, nsteps, TPAD, keys), F32), pltpu.VMEM((NSA_G, qr, 1), F32),
                        pltpu.VMEM((NSA_G, qr, 1), F32), pltpu.VMEM((NSA_G, qr, HD), F32),
                        pltpu.VMEM((NSA_G, qr, HD), F32)],
    )
    return pl.pallas_call(
        kern, out_shape=jax.ShapeDtypeStruct((bs, NSA_G, qr, HD), BF16), grid_spec=gs,
        compiler_params=_cp(("arbitrary", "arbitrary")), name="nsa_sample",
    )(pt, q_g, ck, cv, ksn, kwn, win, gate_g, c2s, *([x] * npg))


def _moba_gate_kernel(pt_ref, q_ref, *refs, npg, nsteps, nb_past):
    pages = refs[:npg]
    sel_ref, km_scr = refs[npg:]
    j = pl.program_id(1)
    ppb = MOBA_BLOCK // PAGE
    bps = npg // ppb

    @pl.when(j == 0)
    def _():
        km_scr[...] = jnp.zeros(km_scr.shape, F32)

    for i in range(bps):
        ssum = jnp.sum(pages[ppb * i][...], axis=0)[0]
        for pp_ in range(1, ppb):
            ssum = ssum + jnp.sum(pages[ppb * i + pp_][...], axis=0)[0]
        ssum = ssum * (1.0 / MOBA_BLOCK)
        for h in range(MOBA_H):
            km_scr[h, pl.ds(j * bps + i, 1), :] = ssum[h:h + 1, :]

    @pl.when(j == nsteps - 1)
    def _():
        lane = lax.broadcasted_iota(I32, (1, LANES), 1)
        for h in range(MOBA_H):
            qhi, qlo = _split(q_ref[0, h])
            khi, klo = _split(km_scr[h])
            gate = _nt(qhi, khi) + _nt(qhi, klo) + _nt(qlo, khi)
            gate = jnp.where(lane < nb_past, gate, NEG)
            sel = _select_blocks(gate, lane, nb_past + 1, min(MOBA_TOPK, nb_past + 1)) & (lane < nb_past)
            sel_ref[0, h] = sel.astype(F32)


def moba_gate_sample(pt, q_h, cache, bs, n_pages):
    npg = 8
    nsteps = n_pages // npg
    hk = MOBA_H * HD
    x = cache.reshape(-1, 2, MOBA_H, HD)
    kern = functools.partial(_moba_gate_kernel, npg=npg, nsteps=nsteps, nb_past=n_pages * PAGE // MOBA_BLOCK)
    gs = pltpu.PrefetchScalarGridSpec(
        num_scalar_prefetch=1,
        grid=(bs, nsteps),
        in_specs=[pl.BlockSpec((1, MOBA_H, TPAD, HD), lambda b, j, pt: (b, 0, 0, 0))]
        + _page_specs(npg, n_pages, (PAGE, 1, MOBA_H, HD), 0),
        out_specs=pl.BlockSpec((1, MOBA_H, TPAD, LANES), lambda b, j, pt: (b, 0, 0, 0)),
        scratch_shapes=[pltpu.VMEM((MOBA_H, LANES, HD), F32)],
    )
    return pl.pallas_call(
        kern, out_shape=jax.ShapeDtypeStruct((bs, MOBA_H, TPAD, LANES), F32), grid_spec=gs,
        compiler_params=_cp(("arbitrary", "arbitrary")), name="moba_gate_sample",
    )(pt, q_h, *([x] * npg))


def _moba_sample_kernel(pt_ref, qbd_ref, sel_ref, kvn_ref, *refs, npg, nsteps, ts):
    pages = refs[:npg]
    o_ref, m_scr, l_scr, acc_scr = refs[npg:]
    j = pl.program_id(1)
    hk = MOBA_H * HD
    qr = MOBA_H * TPAD
    qbd = qbd_ref[0]
    lane = lax.broadcasted_iota(I32, (1, LANES), 1)

    def diag(o_all):
        return jnp.concatenate([o_all[h * TPAD:(h + 1) * TPAD, h * HD:(h + 1) * HD] for h in range(MOBA_H)],
                               axis=0)

    @pl.when(j == 0)
    def _():
        zpad = jnp.zeros((LANES - TPAD, hk), F32)
        kn = jnp.concatenate([kvn_ref[0, :, :hk], zpad], axis=0).astype(BF16)
        vn = jnp.concatenate([kvn_ref[0, :, hk:], zpad], axis=0).astype(BF16)
        t8 = lax.broadcasted_iota(I32, (qr, 1), 0) & (TPAD - 1)
        ok0 = (lane <= t8) & (lane < ts)
        s0 = jnp.where(ok0, _nt(qbd, kn), NEG)
        m0 = jnp.max(s0, axis=-1, keepdims=True)
        p0 = jnp.where(ok0, jnp.exp(s0 - m0), 0.0)
        m_scr[...] = m0
        l_scr[...] = jnp.sum(p0, axis=-1, keepdims=True)
        acc_scr[...] = diag(_mm(p0.astype(BF16), vn))

    rpt = 2 * MOBA_H

    def heads(pg, c):
        return jnp.concatenate([pg[pl.ds(c * MOBA_H + h, PAGE, stride=rpt), :] for h in range(MOBA_H)], axis=1)

    k = jnp.concatenate([heads(pg, 0) for pg in pages], axis=0).astype(BF16)
    v = jnp.concatenate([heads(pg, 1) for pg in pages], axis=0).astype(BF16)
    s = _nt(qbd, k)
    sel = sel_ref[0]
    bps = npg * PAGE // MOBA_BLOCK
    cols = []
    for bb in range(bps):
        on = jnp.sum(jnp.where(lane == j * bps + bb, sel, 0.0), axis=-1, keepdims=True)
        cols.append(jnp.broadcast_to(on, (qr, MOBA_BLOCK)))
    ok = jnp.concatenate(cols, axis=1) > 0.5
    s = jnp.where(ok, s, NEG)
    m_old = m_scr[...]
    m_new = jnp.maximum(m_old, jnp.max(s, axis=-1, keepdims=True))
    alpha = jnp.exp(m_old - m_new)
    pk = jnp.where(ok, jnp.exp(s - m_new), 0.0)
    m_scr[...] = m_new
    l_scr[...] = alpha * l_scr[...] + jnp.sum(pk, axis=-1, keepdims=True)
    acc_scr[...] = alpha * acc_scr[...] + diag(_mm(pk.astype(BF16), v))

    @pl.when(j == nsteps - 1)
    def _():
        o_ref[0] = (acc_scr[...] / l_scr[...]).astype(BF16)


def moba_sample(pt, q_bd, sel, kvn, cache, bs, n_pages, ts):
    npg = 8
    nsteps = n_pages // npg
    hk = MOBA_H * HD
    qr = MOBA_H * TPAD
    x = cache.reshape(-1, HD)
    kern = functools.partial(_moba_sample_kernel, npg=npg, nsteps=nsteps, ts=ts)

    def per_b(a):
        return pl.BlockSpec((1,) + a.shape[1:], lambda b, j, pt: (b,) + (0,) * (a.ndim - 1))

    gs = pltpu.PrefetchScalarGridSpec(
        num_scalar_prefetch=1,
        grid=(bs, nsteps),
        in_specs=[per_b(q_bd), per_b(sel), per_b(kvn)]
        + _page_specs(npg, n_pages, (PAGE * 2 * MOBA_H, HD), 0),
        out_specs=pl.BlockSpec((1, qr, HD), lambda b, j, pt: (b, 0, 0)),
        scratch_shapes=[pltpu.VMEM((qr, 1), F32), pltpu.VMEM((qr, 1), F32), pltpu.VMEM((qr, HD), F32)],
    )
    return pl.pallas_call(
        kern, out_shape=jax.ShapeDtypeStruct((bs, qr, HD), BF16), grid_spec=gs,
        compiler_params=_cp(("arbitrary", "arbitrary")), name="moba_sample",
    )(pt, q_bd, sel, kvn, *([x] * npg))


def _conv_sample_kernel(st_ref, u_ref, wdw_ref, bdw_ref, lg_ref, lb_ref, w2_ref, b2_ref, x_ref, g_ref, o_ref,
                        *, ts):
    nst = CONV_W - 1
    bs = st_ref.shape[1]
    zs = []
    for t in range(ts):
        acc = None
        for k in range(CONV_W):
            r = t + k
            row = st_ref[r] if r < nst else u_ref[r - nst]
            term = wdw_ref[k:k + 1, :] * row
            acc = term if acc is None else acc + term
        zs.append(_ln_silu(acc + bdw_ref[...], lg_ref[...], lb_ref[...]))
    z = jnp.concatenate(zs, axis=0).astype(BF16)
    out = _mm(z, w2_ref[...]) + b2_ref[...]
    for t in range(ts):
        o_ref[t] = x_ref[t] + g_ref[t] * out[t * bs:(t + 1) * bs]


def conv_sample(st_t, u_t, w_dw, b_dw, ln_g, ln_b, w_pw2, b_pw2, x_t, g_t):
    ts = u_t.shape[0]
    kern = functools.partial(_conv_sample_kernel, ts=ts)
    full = lambda a: pl.BlockSpec(a.shape, lambda i: (0,) * a.ndim)
    args = (st_t, u_t, w_dw, b_dw, ln_g, ln_b, w_pw2, b_pw2, x_t, g_t)
    return pl.pallas_call(
        kern, out_shape=jax.ShapeDtypeStruct(x_t.shape, F32), grid=(1,),
        in_specs=[full(a) for a in args], out_specs=full(x_t),
        compiler_params=_cp(("arbitrary",)), name="conv_sample",
    )(*args)


def _trunk_sample(x, mods, pp, past):
    bs, ts, d = x.shape
    m = bs * ts
    n_pages = past["page_table"].shape[1]
    plen = n_pages * PAGE
    assert ts <= TPAD and ts < CMP_STRIDE and plen % MOBA_BLOCK == 0 and m % 8 == 0
    pt = past["page_table"].reshape(-1).astype(I32)
    x2d = x.reshape(m, d)
    sh1, sc1, g1, sh2, sc2, g2 = mods[0]
    qn, kvc, kvs, kvw, qm, kvm, gate, _ = attn_in_proj(
        x2d, pp["norm_mix_g"][0:1], sh1, sc1, pp["w_main"], pp["w_gate"], m, 1)

    def pad_t(a, axis):
        w = [(0, 0)] * a.ndim
        w[axis] = (0, TPAD - ts)
        return jnp.pad(a, w)

    ck, cv = compress_sample(pt, past["nsa_cmp"], bs, n_pages, pp["wab"], pp["pe2"], pp["w2"])
    q_g = pad_t(qn.reshape(bs, ts, NSA_G, NSA_HPG, HD).transpose(0, 2, 3, 1, 4), 3)
    q_g = q_g.reshape(bs, NSA_G, NSA_HPG * TPAD, HD)
    gate_g = pad_t(gate[:, :3 * NSA_G * NSA_HPG].reshape(bs, ts, NSA_G, NSA_HPG, 3).transpose(0, 2, 3, 1, 4), 3)
    gate_g = jnp.pad(gate_g.reshape(bs, NSA_G, NSA_HPG * TPAD, 3), ((0, 0), (0, 0), (0, 0), (0, LANES - 3)))
    ksn = pad_t(kvs.reshape(bs, ts, 4 * HD), 1)
    kwn = pad_t(kvw.reshape(bs, ts, 4 * HD), 1)
    win = past["nsa_win"].reshape(-1, HD)
    o_g = nsa_sample(pt, q_g, ck, cv, ksn, kwn, win, gate_g, past["nsa_slc"], bs, n_pages, ts)
    o_nsa = o_g.reshape(bs, NSA_G, NSA_HPG, TPAD, HD)[:, :, :, :ts].transpose(0, 3, 1, 2, 4).reshape(m, -1)

    q_h = pad_t(qm.reshape(bs, ts, MOBA_H, HD).transpose(0, 2, 1, 3), 2)
    sel = moba_gate_sample(pt, q_h, past["moba"], bs, n_pages)
    eye = jnp.eye(MOBA_H, dtype=F32)
    q_bd = ((q_h * SCALE)[:, :, :, None, :] * eye[None, :, None, :, None]).astype(BF16)
    q_bd = q_bd.reshape(bs, MOBA_H * TPAD, MOBA_H * HD)
    kvn = pad_t(kvm.reshape(bs, ts, 2 * MOBA_H * HD), 1)
    o_m = moba_sample(pt, q_bd, sel.reshape(bs, MOBA_H * TPAD, LANES), kvn, past["moba"], bs, n_pages, ts)
    o_moba = o_m.reshape(bs, MOBA_H, TPAD, HD)[:, :, :ts].transpose(0, 2, 1, 3).reshape(m, -1)

    x1 = attn_out_proj(o_nsa, o_moba, pp["w_out"], x2d, g1, m, 1)
    x2 = hier_moe_block(x1, pp["norm_ffn_g"][0:1], sh2, sc2, g2, pp["moe"][0], 0, m, 1, pp["final_g"], False)
    sh1, sc1, g1, sh2, sc2, g2 = mods[1]
    u = conv_pw1_glu(x2, pp["norm_mix_g"][1:2], sh1, sc1, pp["w_pw1"], pp["b_pw1"], m, 1)
    tb = lambda a: a.reshape(bs, ts, -1).transpose(1, 0, 2)
    st = past["conv"][0]
    x3_t = conv_sample(st.transpose(1, 0, 2), tb(u), pp["w_dw"], pp["b_dw"], pp["ln_g"], pp["ln_b"],
                       pp["w_pw2"], pp["b_pw2"], tb(x2), tb(g1[0]))
    x3 = x3_t.transpose(1, 0, 2).reshape(m, d)
    y = hier_moe_block(x3, pp["norm_ffn_g"][1:2], sh2, sc2, g2, pp["moe"][1], 1, m, 1, pp["final_g"], True)
    state = (
        kvc.reshape(1, bs, ts, 2, NSA_G, HD), kvs.reshape(1, bs, ts, 2, NSA_G, HD),
        jnp.concatenate([past["nsa_win"][0][:, ts:], kvw.reshape(bs, ts, 2, NSA_G, HD)], axis=1)[None],
        kvm.reshape(1, bs, ts, 2, MOBA_H, HD),
        jnp.concatenate([st[:, ts:], u.reshape(bs, ts, -1)], axis=1)[None],
    )
    return y.reshape(bs, ts, d), state


def _mods_from(m_all, rows, expand):
    out = []
    for layer in range(m_all.shape[0]):
        parts = jnp.split(m_all[layer, rows], 6, axis=-1)
        if expand:
            parts = [jnp.repeat(a, expand, axis=0)[None] for a in parts]
        else:
            parts = [a[:, None, :] for a in parts]
        out.append(parts)
    return out


def kernel(x_prompt, x_sample, cache_nsa_cmp_kv, cache_nsa_slc_kv, state_nsa_win_kv, cache_moba_kv, state_conv, page_table, c_prompt, c_sample, norm_mix_g, norm_ffn_g, ada_w, ada_b, attn_w_in, attn_w_out, nsa_cmp_pe, nsa_cmp_w1, nsa_cmp_w2, conv_w_pw1, conv_b_pw1, conv_w_dw, conv_b_dw, conv_ln_g, conv_ln_b, conv_w_pw2, conv_b_pw2, moe_wg, moe_bg, moe_we, moe_be, moe_w_gate, moe_w_up, moe_w_down, final_norm_g):
    p = dict(norm_mix_g=norm_mix_g, norm_ffn_g=norm_ffn_g, ada_w=ada_w, ada_b=ada_b, attn_w_in=attn_w_in,
             attn_w_out=attn_w_out, nsa_cmp_pe=nsa_cmp_pe, nsa_cmp_w1=nsa_cmp_w1, nsa_cmp_w2=nsa_cmp_w2,
             conv_w_pw1=conv_w_pw1, conv_b_pw1=conv_b_pw1, conv_w_dw=conv_w_dw, conv_b_dw=conv_b_dw,
             conv_ln_g=conv_ln_g, conv_ln_b=conv_ln_b, conv_w_pw2=conv_w_pw2, conv_b_pw2=conv_b_pw2,
             moe_wg=moe_wg, moe_bg=moe_bg, moe_we=moe_we, moe_be=moe_be, moe_w_gate=moe_w_gate,
             moe_w_up=moe_w_up, moe_w_down=moe_w_down, final_norm_g=final_norm_g)
    pp = _prep_params(p)
    bp = x_prompt.shape[0]
    bs, ts, d = x_sample.shape
    c_all = jnp.concatenate([c_prompt, c_sample], axis=0)
    pad = (-c_all.shape[0]) % 16
    c_all = jnp.pad(c_all, ((0, pad), (0, 0)))
    m_all = ada_params(c_all, ada_w, ada_b)
    mods_p = _mods_from(m_all, slice(0, bp), 0)
    y_p, (cmp_p, slc_p, win_p, moba_p, conv_p) = _trunk_prompt(x_prompt, mods_p, pp)
    mods_s = _mods_from(m_all, slice(bp, bp + bs), ts)
    past = dict(page_table=page_table, nsa_cmp=cache_nsa_cmp_kv, nsa_slc=cache_nsa_slc_kv,
                nsa_win=state_nsa_win_kv, moba=cache_moba_kv, conv=state_conv)
    y_s, (cmp_s, slc_s, win_s, moba_s, conv_s) = _trunk_sample(x_sample, mods_s, pp, past)
    return (y_p, y_s, cmp_p, cmp_s, slc_p, slc_s, win_p, win_s, moba_p, moba_s, conv_p, conv_s)
```

```python
import functools

import numpy as np
import jax
import jax.numpy as jnp
from jax import lax
from jax.experimental import pallas as pl
from jax.experimental.pallas import tpu as pltpu

F32 = jnp.float32
BF16 = jnp.bfloat16
I32 = jnp.int32

HD = 128
LANES = 128
SCALE = HD ** -0.5
NSA_G = 2
NSA_HPG = 4
CMP_BLOCK = 32
CMP_STRIDE = 16
SLC_BLOCK = 64
SLC_TOPN = 16
WINDOW = 512
FORCED = 1.0e4
MOBA_H = 8
MOBA_BLOCK = 256
MOBA_TOPK = 3
CONV_W = 31
N_GROUPS = 4
EPG = 8
N_EXP = N_GROUPS * EPG
PAGE = 128
EPS = 1e-6
NEG = -1e30
TE = 256
VMEM_LIMIT = 56 * 1024 * 1024


def _cp(sem, vmem=VMEM_LIMIT):
    return pltpu.CompilerParams(dimension_semantics=sem, vmem_limit_bytes=vmem)


def _nt(a, b):
    return lax.dot_general(a, b, (((1,), (1,)), ((), ())), preferred_element_type=F32)


def _mm(a, b):
    return jnp.dot(a, b, preferred_element_type=F32)


def _split(x):
    hi = x.astype(BF16)
    lo = (x - hi.astype(F32)).astype(BF16)
    return hi, lo


def _sigmoid(x):
    return 1.0 / (1.0 + jnp.exp(-x))


def _rms_mod(x, g, shift, scale):
    y = x * lax.rsqrt(jnp.mean(x * x, axis=-1, keepdims=True) + EPS)
    return (y * g) * (1.0 + scale) + shift


def _masked_softmax(s, valid):
    sm = jnp.where(valid, s, NEG)
    mx = jnp.max(sm, axis=-1, keepdims=True)
    e = jnp.where(valid, jnp.exp(sm - mx), 0.0)
    return e / jnp.maximum(jnp.sum(e, axis=-1, keepdims=True), 1e-30)


def _ada_kernel(c_ref, w_ref, b_ref, o_ref):
    c = c_ref[...]
    s = c * _sigmoid(c)
    shi, slo = _split(s)
    whi, wlo = _split(w_ref[0])
    o_ref[0] = _mm(shi, whi) + _mm(shi, wlo) + _mm(slo, whi) + b_ref[0]


def ada_params(c_all, ada_w, ada_b):
    depth, d, n6 = ada_w.shape
    r = c_all.shape[0]
    tn = 1024 if n6 % 1024 == 0 else 512
    return pl.pallas_call(
        _ada_kernel,
        out_shape=jax.ShapeDtypeStruct((depth, r, n6), F32),
        grid=(depth, n6 // tn),
        in_specs=[
            pl.BlockSpec((r, d), lambda l, j: (0, 0)),
            pl.BlockSpec((1, d, tn), lambda l, j: (l, 0, j)),
            pl.BlockSpec((1, 1, tn), lambda l, j: (l, 0, j)),
        ],
        out_specs=pl.BlockSpec((1, r, tn), lambda l, j: (l, 0, j)),
        compiler_params=_cp(("arbitrary", "arbitrary")),
        name="ada_params",
    )(c_all, ada_w, ada_b.reshape(depth, 1, n6))


_TN = 512
_SEGS = ((0, 2), (2, 1), (3, 1), (4, 1), (5, 2), (7, 4))
_N_MAIN_TILES = 11
_KV_ROWS = 2 * NSA_G


def _inproj_kernel(x_ref, g_ref, sh_ref, sc_ref, w_ref, wg_ref,
                   qn_ref, kvc_ref, kvs_ref, kvw_ref, qm_ref, kvm_ref, gate_ref, kvmb_ref, h_scr):
    j = pl.program_id(1)

    @pl.when(j == 0)
    def _():
        h = _rms_mod(x_ref[...], g_ref[...], sh_ref[0], sc_ref[0]).astype(BF16)
        h_scr[...] = h
        gate_ref[...] = _sigmoid(_mm(h, wg_ref[...]))

    z = _mm(h_scr[...], w_ref[...])
    outs = (qn_ref, kvc_ref, kvs_ref, kvw_ref, qm_ref, kvm_ref)
    for idx, (ref, (start, n)) in enumerate(zip(outs, _SEGS)):
        @pl.when((j >= start) & (j < start + n))
        def _(ref=ref, idx=idx):
            if idx in (1, 2, 3):
                rows = z.shape[0]
                for cg in range(_KV_ROWS):
                    ref[pl.ds(cg, rows, stride=_KV_ROWS), :] = z[:, cg * HD:(cg + 1) * HD]
            else:
                ref[...] = z
            if ref is kvm_ref:
                kvmb_ref[...] = z.astype(BF16)


def attn_in_proj(x2d, g, shift, scale, w_main, w_gate, tm, tiles_per_mod):
    m, d = x2d.shape
    r = shift.shape[1]

    def seg_spec(start, n):
        return pl.BlockSpec((tm, _TN), lambda i, j: (i, jnp.clip(j - start, 0, n - 1)))

    out_shape = [jax.ShapeDtypeStruct((m, n * _TN), F32) for (_, n) in _SEGS]
    out_specs = [seg_spec(s, n) for (s, n) in _SEGS]
    for k in (1, 2, 3):
        out_shape[k] = jax.ShapeDtypeStruct((m * _KV_ROWS, HD), F32)
        out_specs[k] = pl.BlockSpec((tm * _KV_ROWS, HD), lambda i, j: (i, 0))
    out_shape.append(jax.ShapeDtypeStruct((m, LANES), F32))
    out_specs.append(pl.BlockSpec((tm, LANES), lambda i, j: (i, 0)))
    out_shape.append(jax.ShapeDtypeStruct((m, _SEGS[-1][1] * _TN), BF16))
    out_specs.append(seg_spec(*_SEGS[-1]))
    mod_spec = pl.BlockSpec((1, r, d), lambda i, j: (i // tiles_per_mod, 0, 0))
    return pl.pallas_call(
        _inproj_kernel,
        out_shape=out_shape,
        grid=(m // tm, _N_MAIN_TILES),
        in_specs=[
            pl.BlockSpec((tm, d), lambda i, j: (i, 0)),
            pl.BlockSpec((1, d), lambda i, j: (0, 0)),
            mod_spec, mod_spec,
            pl.BlockSpec((d, _TN), lambda i, j: (0, j)),
            pl.BlockSpec((d, LANES), lambda i, j: (0, 0)),
        ],
        out_specs=out_specs,
        scratch_shapes=[pltpu.VMEM((tm, d), BF16)],
        compiler_params=_cp(("arbitrary", "arbitrary")),
        name="attn_in_proj",
    )(x2d, g, shift, scale, w_main, w_gate)


def _gelu_tanh(x):
    return 0.5 * x * (1.0 + jnp.tanh(0.7978845608028654 * (x + 0.044715 * x * x * x)))


def _compress_finish(ab, c, pe_ref, wab_ref, w2_ref):
    rows = ab.shape[0]
    pe2 = _mm(pe_ref[c], wab_ref[c])
    pe_term = pe2[0:1, :HD] + pe2[1:2, HD:]
    pre = ab[:, :HD] + pltpu.roll(ab[:, HD:], rows - 1, 0) + pe_term
    return _mm(_gelu_tanh(pre).astype(BF16), w2_ref[c])


def _compress_prompt_kernel(x_ref, wab_ref, pe_ref, w2_ref, ck_ref, cv_ref):
    ng = x_ref.shape[0] // (CMP_STRIDE * _KV_ROWS)
    for c, ref in ((0, ck_ref), (1, cv_ref)):
        for g in range(NSA_G):
            cg = c * NSA_G + g
            xs = jnp.concatenate([x_ref[pl.ds(r * _KV_ROWS + cg, ng, stride=CMP_STRIDE * _KV_ROWS), :]
                                  for r in range(CMP_STRIDE)], axis=1).astype(BF16)
            ab = _mm(xs, wab_ref[c])
            ref[0, g] = _compress_finish(ab, c, pe_ref, wab_ref, w2_ref).astype(BF16)


def compress_prompt(kvc, b, t, wab, pe2, w2):
    ng = t // CMP_STRIDE
    out = jax.ShapeDtypeStruct((b, NSA_G, ng, HD), BF16)
    ospec = pl.BlockSpec((1, NSA_G, ng, HD), lambda i: (i, 0, 0, 0))
    x = kvc
    return pl.pallas_call(
        _compress_prompt_kernel,
        out_shape=[out, out],
        grid=(b,),
        in_specs=[
            pl.BlockSpec((t * _KV_ROWS, HD), lambda i: (i, 0)),
            pl.BlockSpec(wab.shape, lambda i: (0, 0, 0)),
            pl.BlockSpec(pe2.shape, lambda i: (0, 0, 0)),
            pl.BlockSpec(w2.shape, lambda i: (0, 0, 0)),
        ],
        out_specs=[ospec, ospec],
        compiler_params=_cp(("arbitrary",)),
        name="nsa_compress_prompt",
    )(x, wab, pe2, w2)


def _select_blocks(score, lane, n_blocks, topn):
    cnt = jnp.zeros(score.shape, F32)
    for i in range(n_blocks):
        ci = score[:, i:i + 1]
        beats = (ci > score) | ((ci == score) & (lane > i))
        cnt = cnt + beats.astype(F32)
    return (cnt < topn) & (lane < n_blocks)


def _nsa_prompt_kernel(q_ref, ck_ref, cv_ref, kvs_ref, kvw_ref, gate_ref, c2st_ref, ee_ref,
                       o_ref, selx_ref, m_ref, l_ref, acc_ref, *, tq, tk, t_len, n_cmp, n_slc):
    g = pl.program_id(1)
    q0 = pl.program_id(2) * tq
    q = q_ref[...] * SCALE
    qh = [q[:, h * HD:(h + 1) * HD].astype(BF16) for h in range(NSA_HPG)]
    trow = q0 + lax.broadcasted_iota(I32, (tq, 1), 0)
    lane = lax.broadcasted_iota(I32, (1, LANES), 1)

    valid = (lane * CMP_STRIDE + (CMP_BLOCK - 1) <= trow) & (lane < n_cmp)
    p_h = [_masked_softmax(_nt(qh[h], ck_ref[0, 0]), valid) for h in range(NSA_HPG)]
    o_cmp = [_mm(p_h[h].astype(BF16), cv_ref[0, 0]) for h in range(NSA_HPG)]

    psum = p_h[0] + p_h[1] + p_h[2] + p_h[3]
    phi, plo = _split(psum)
    imp_t = _nt(c2st_ref[...], phi) + _nt(c2st_ref[...], plo)
    ns8 = -(-n_slc // 8) * 8
    blk = lax.broadcasted_iota(I32, (ns8, 1), 0)
    tl = q0 + lax.broadcasted_iota(I32, (1, tq), 1)
    qb = tl >> 6
    forced = (blk == 0) | (blk == qb) | (blk == qb - 1)
    score = jnp.where(forced, FORCED, imp_t[0:ns8])
    score = jnp.where(blk * SLC_BLOCK <= tl, score, -1.0)
    score = jnp.where(blk < n_slc, score, -2.0)
    cnt = jnp.zeros((ns8, tq), F32)
    for i in range(n_slc):
        ci = score[i:i + 1, :]
        cnt = cnt + ((ci > score) | ((ci == score) & (blk > i))).astype(F32)
    sel_t = ((cnt < min(SLC_TOPN, n_slc)) & (blk < n_slc)).astype(F32)
    sel = jnp.concatenate([sel_t, jnp.zeros((LANES - ns8, tq), F32)], axis=0).T
    selx_ref[...] = _mm(sel.astype(BF16), ee_ref[...])

    def slc_tile(kt, first):
        k = kvs_ref[pl.ds(kt * tk * _KV_ROWS + g, tk, stride=_KV_ROWS), :].astype(BF16)
        v = kvs_ref[pl.ds(kt * tk * _KV_ROWS + NSA_G + g, tk, stride=_KV_ROWS), :].astype(BF16)
        kpos = kt * tk + lax.broadcasted_iota(I32, (1, tk), 1)
        ok = (selx_ref[:, kt * tk:(kt + 1) * tk] > 0.5) & (kpos <= trow)
        for h in range(NSA_HPG):
            sk = jnp.where(ok, _nt(qh[h], k), NEG)
            mx = jnp.max(sk, axis=-1, keepdims=True)
            if first:
                pk = jnp.exp(sk - mx)
                m_ref[h] = mx
                l_ref[h] = jnp.sum(pk, axis=-1, keepdims=True)
                acc_ref[h] = _mm(pk.astype(BF16), v)
            else:
                m_old = m_ref[h]
                m_new = jnp.maximum(m_old, mx)
                alpha = jnp.exp(m_old - m_new)
                pk = jnp.exp(sk - m_new)
                m_ref[h] = m_new
                l_ref[h] = alpha * l_ref[h] + jnp.sum(pk, axis=-1, keepdims=True)
                acc_ref[h] = alpha * acc_ref[h] + _mm(pk.astype(BF16), v)

    slc_tile(0, True)
    for kt in range(1, t_len // tk):
        pl.when(kt * tk <= q0 + tq - 1)(functools.partial(slc_tile, kt, False))

    span = WINDOW + tq
    start = pl.multiple_of(jnp.maximum(q0 - WINDOW, 0), LANES)
    kw = kvw_ref[pl.ds(start * _KV_ROWS + g, span, stride=_KV_ROWS), :].astype(BF16)
    vw = kvw_ref[pl.ds(start * _KV_ROWS + NSA_G + g, span, stride=_KV_ROWS), :].astype(BF16)
    kpos = start + lax.broadcasted_iota(I32, (1, span), 1)
    okw = (kpos <= trow) & (kpos > trow - WINDOW)

    gt = gate_ref[...]
    for h in range(NSA_HPG):
        o_win = _mm(_masked_softmax(_nt(qh[h], kw), okw).astype(BF16), vw)
        o_slc = acc_ref[h] / l_ref[h]
        gc = [jnp.where(g == 0, gt[:, h * 3 + kk:h * 3 + kk + 1],
                        gt[:, (NSA_HPG + h) * 3 + kk:(NSA_HPG + h) * 3 + kk + 1]) for kk in range(3)]
        o_ref[:, h * HD:(h + 1) * HD] = (gc[0] * o_cmp[h] + gc[1] * o_slc + gc[2] * o_win).astype(BF16)


def _cmp_to_slc(n_cmp, n_slc):
    cs = np.arange(n_cmp) * CMP_STRIDE
    ss = np.arange(n_slc) * SLC_BLOCK
    shared = (np.minimum(cs[:, None] + CMP_BLOCK, ss[None, :] + SLC_BLOCK)
              - np.maximum(cs[:, None], ss[None, :]))
    return np.clip(shared, 0, None) / CMP_STRIDE


def nsa_prompt(qn, ck, cv, kvs, kvw, gate, b, t):
    tq, tk = 256, 512
    nq = t // tq
    n_cmp = (t - CMP_BLOCK) // CMP_STRIDE + 1
    n_slc = -(-t // SLC_BLOCK)
    c2s = np.zeros((LANES, LANES), np.float32)
    c2s[:n_slc, :n_cmp] = _cmp_to_slc(n_cmp, n_slc).T
    ee = (np.arange(LANES)[:, None] == (np.arange(t)[None, :] // SLC_BLOCK)).astype(np.float32)
    kern = functools.partial(_nsa_prompt_kernel, tq=tq, tk=tk, t_len=t, n_cmp=n_cmp, n_slc=n_slc)
    return pl.pallas_call(
        kern,
        out_shape=jax.ShapeDtypeStruct((b * t, NSA_G * NSA_HPG * HD), BF16),
        grid=(b, NSA_G, nq),
        in_specs=[
            pl.BlockSpec((tq, NSA_HPG * HD), lambda i, g, q: (i * nq + q, g)),
            pl.BlockSpec((1, 1, ck.shape[2], HD), lambda i, g, q: (i, g, 0, 0)),
            pl.BlockSpec((1, 1, cv.shape[2], HD), lambda i, g, q: (i, g, 0, 0)),
            pl.BlockSpec((t * _KV_ROWS, HD), lambda i, g, q: (i, 0)),
            pl.BlockSpec((t * _KV_ROWS, HD), lambda i, g, q: (i, 0)),
            pl.BlockSpec((tq, LANES), lambda i, g, q: (i * nq + q, 0)),
            pl.BlockSpec((LANES, LANES), lambda i, g, q: (0, 0)),
            pl.BlockSpec((LANES, t), lambda i, g, q: (0, 0)),
        ],
        out_specs=pl.BlockSpec((tq, NSA_HPG * HD), lambda i, g, q: (i * nq + q, g)),
        scratch_shapes=[pltpu.VMEM((tq, t), F32), pltpu.VMEM((NSA_HPG, tq, 1), F32),
                        pltpu.VMEM((NSA_HPG, tq, 1), F32), pltpu.VMEM((NSA_HPG, tq, HD), F32)],
        compiler_params=_cp(("arbitrary", "arbitrary", "arbitrary")),
        name="nsa_prompt",
    )(qn, ck, cv, kvs, kvw, gate, jnp.asarray(c2s, BF16), jnp.asarray(ee, BF16))


MOBA_HPS = 4
KM_ROWS = 8


def _moba_kmean_kernel(k_ref, o_ref):
    n = pl.program_id(1)

    @pl.when(n == 0)
    def _():
        o_ref[...] = jnp.zeros(o_ref.shape, F32)

    o_ref[0, pl.ds(n, 1), :] = jnp.sum(k_ref[...], axis=0, keepdims=True) * (1.0 / MOBA_BLOCK)


def moba_kmean(kvm, b, t):
    nb = t // MOBA_BLOCK
    hk = MOBA_H * HD
    assert nb <= KM_ROWS
    return pl.pallas_call(
        _moba_kmean_kernel,
        out_shape=jax.ShapeDtypeStruct((b, KM_ROWS, hk), F32),
        grid=(b, nb),
        in_specs=[pl.BlockSpec((MOBA_BLOCK, hk), lambda i, n: (i * nb + n, 0))],
        out_specs=pl.BlockSpec((1, KM_ROWS, hk), lambda i, n: (i, 0, 0)),
        compiler_params=_cp(("arbitrary", "arbitrary")),
        name="moba_kmean",
    )(kvm)


def _moba_prompt_kernel(q_ref, k_ref, v_ref, km_ref, o_ref, *, t_len):
    qi = pl.program_id(2)
    nb = t_len // MOBA_BLOCK
    tq = MOBA_BLOCK
    lane = lax.broadcasted_iota(I32, (1, LANES), 1)
    blk = lax.broadcasted_iota(I32, (KM_ROWS, 1), 0)
    row = lax.broadcasted_iota(I32, (tq, 1), 0)
    col = lax.broadcasted_iota(I32, (1, tq), 1)
    d0 = pl.multiple_of(qi * MOBA_BLOCK, MOBA_BLOCK)

    qs_l, sel_l, init = [], [], []
    for hh in range(MOBA_HPS):
        cs = slice(hh * HD, (hh + 1) * HD)
        qf = q_ref[:, cs]
        qhi, qlo = _split(qf)
        khi, klo = _split(km_ref[0, :, cs])
        gt = _nt(khi, qhi) + _nt(klo, qhi) + _nt(khi, qlo)
        gt = jnp.where(blk < qi, gt, NEG)
        cnt = jnp.zeros((KM_ROWS, tq), F32)
        for i in range(nb):
            ci = gt[i:i + 1, :]
            cnt = cnt + ((ci > gt) | ((ci == gt) & (blk > i))).astype(F32)
        selt = ((cnt < min(MOBA_TOPK, nb)) & (blk < qi)).astype(F32)
        sel_l.append(jnp.concatenate([selt, jnp.zeros((LANES - KM_ROWS, tq), F32)], axis=0).T)
        qs = (qf * SCALE).astype(BF16)
        qs_l.append(qs)
        s = jnp.where(col <= row, _nt(qs, k_ref[pl.ds(d0, MOBA_BLOCK), cs]), NEG)
        m0 = jnp.max(s, axis=-1, keepdims=True)
        p0 = jnp.exp(s - m0)
        init.append((m0, jnp.sum(p0, axis=-1, keepdims=True),
                     _mm(p0.astype(BF16), v_ref[pl.ds(d0, MOBA_BLOCK), cs])))

    pair = 2 * MOBA_BLOCK

    def body(kp, carry):
        k0 = pl.multiple_of(kp * pair, pair)
        out = []
        for hh in range(MOBA_HPS):
            cs = slice(hh * HD, (hh + 1) * HD)
            m_old, l_old, acc = carry[hh]
            on = [jnp.broadcast_to(jnp.sum(jnp.where(lane == 2 * kp + e, sel_l[hh], 0.0), axis=-1, keepdims=True),
                                   (tq, MOBA_BLOCK)) for e in range(2)]
            ok = jnp.concatenate(on, axis=1) > 0.5
            sk = jnp.where(ok, _nt(qs_l[hh], k_ref[pl.ds(k0, pair), cs]), NEG)
            m_new = jnp.maximum(m_old, jnp.max(sk, axis=-1, keepdims=True))
            alpha = jnp.exp(m_old - m_new)
            pk = jnp.exp(sk - m_new)
            out.append((m_new, alpha * l_old + jnp.sum(pk, axis=-1, keepdims=True),
                        alpha * acc + _mm(pk.astype(BF16), v_ref[pl.ds(k0, pair), cs])))
        return tuple(out)

    fin = lax.fori_loop(0, (qi + 1) >> 1, body, tuple(init))
    for hh in range(MOBA_HPS):
        _, l_fin, acc = fin[hh]
        o_ref[:, hh * HD:(hh + 1) * HD] = (acc / l_fin).astype(BF16)


def moba_prompt(qm, kvm_bf, kmean, b, t):
    nq = t // MOBA_BLOCK
    ng = MOBA_H // MOBA_HPS
    w = MOBA_HPS * HD
    kern = functools.partial(_moba_prompt_kernel, t_len=t)
    return pl.pallas_call(
        kern,
        out_shape=jax.ShapeDtypeStruct((b * t, MOBA_H * HD), BF16),
        grid=(b, ng, nq),
        in_specs=[
            pl.BlockSpec((MOBA_BLOCK, w), lambda i, h, q: (i * nq + q, h)),
            pl.BlockSpec((t, w), lambda i, h, q: (i, h)),
            pl.BlockSpec((t, w), lambda i, h, q: (i, ng + h)),
            pl.BlockSpec((1, KM_ROWS, w), lambda i, h, q: (i, 0, h)),
        ],
        out_specs=pl.BlockSpec((MOBA_BLOCK, w), lambda i, h, q: (i * nq + q, h)),
        compiler_params=_cp(("arbitrary", "arbitrary", "arbitrary")),
        name="moba_prompt",
    )(qm, kvm_bf, kvm_bf, kmean)


def _outproj_kernel(on_ref, om_ref, w1_ref, w2_ref, x_ref, g_ref, o_ref):
    acc = _mm(on_ref[...], w1_ref[...]) + _mm(om_ref[...], w2_ref[...])
    o_ref[...] = x_ref[...] + g_ref[0] * acc


def attn_out_proj(o_nsa, o_moba, w_out, x2d, gate, tm, tiles_per_mod):
    m, d = x2d.shape
    r = gate.shape[1]
    kn = o_nsa.shape[1]
    return pl.pallas_call(
        _outproj_kernel,
        out_shape=jax.ShapeDtypeStruct((m, d), F32),
        grid=(m // tm,),
        in_specs=[
            pl.BlockSpec((tm, kn), lambda i: (i, 0)),
            pl.BlockSpec((tm, kn), lambda i: (i, 0)),
            pl.BlockSpec((kn, d), lambda i: (0, 0)),
            pl.BlockSpec((kn, d), lambda i: (1, 0)),
            pl.BlockSpec((tm, d), lambda i: (i, 0)),
            pl.BlockSpec((1, r, d), lambda i: (i // tiles_per_mod, 0, 0)),
        ],
        out_specs=pl.BlockSpec((tm, d), lambda i: (i, 0)),
        compiler_params=_cp(("arbitrary",)),
        name="attn_out_proj",
    )(o_nsa, o_moba, w_out, w_out, x2d, gate)


_GROUP_LANE0 = 64
U32 = jnp.uint32
_HI16 = 0xFFFF0000


def _pack_bf16_pairs(xb):
    half = xb.shape[1] // 2
    bits = lax.bitcast_convert_type(xb.astype(F32), U32)
    return (bits[:, half:] & jnp.uint32(_HI16)) | (bits[:, :half] >> 16)


def _unpack_bf16_pairs(xp):
    lo = lax.bitcast_convert_type(xp << 16, F32)
    hi = lax.bitcast_convert_type(xp & jnp.uint32(_HI16), F32)
    return jnp.concatenate([lo, hi], axis=1)


def _router_kernel(x_ref, g_ref, sh_ref, sc_ref, wh_ref, wl_ref, b_ref, h_ref, meta_ref, cnt_ref, carry_ref,
                   *, tm):
    i = pl.program_id(0)

    @pl.when(i == 0)
    def _():
        carry_ref[...] = jnp.zeros(carry_ref.shape, F32)

    h = _rms_mod(x_ref[...], g_ref[...], sh_ref[0], sc_ref[0])
    hhi, hlo = _split(h)
    h_ref[...] = _pack_bf16_pairs(hhi)
    lg = _mm(hhi, wh_ref[...]) + _mm(hhi, wl_ref[...]) + _mm(hlo, wh_ref[...]) + b_ref[...]
    lane = lax.broadcasted_iota(I32, (1, LANES), 1)
    lanef = lane.astype(F32)
    big = 1.0e9

    isg = (lane >= _GROUP_LANE0) & (lane < _GROUP_LANE0 + N_GROUPS)
    mxg = jnp.max(jnp.where(isg, lg, NEG), axis=-1, keepdims=True)
    grp = jnp.min(jnp.where(isg & (lg == mxg), lanef - _GROUP_LANE0, big), axis=-1, keepdims=True)
    pg = 1.0 / jnp.sum(jnp.where(isg, jnp.exp(lg - mxg), 0.0), axis=-1, keepdims=True)

    ing = (lane < N_EXP) & ((lane >> 3).astype(F32) == grp)
    l1 = jnp.max(jnp.where(ing, lg, NEG), axis=-1, keepdims=True)
    i1 = jnp.min(jnp.where(ing & (lg == l1), lanef, big), axis=-1, keepdims=True)
    ing2 = ing & (lanef != i1)
    l2 = jnp.max(jnp.where(ing2, lg, NEG), axis=-1, keepdims=True)
    i2 = jnp.min(jnp.where(ing2 & (lg == l2), lanef, big), axis=-1, keepdims=True)
    e21 = jnp.exp(l2 - l1)
    w1 = pg / (1.0 + e21)
    w2 = pg * e21 / (1.0 + e21)

    oh1 = lanef == i1
    oh2 = lanef == i2
    oh = (oh1 | oh2).astype(F32)
    r_i = lax.broadcasted_iota(I32, (tm, tm), 0)
    c_i = lax.broadcasted_iota(I32, (tm, tm), 1)
    lower = (c_i < r_i).astype(BF16)
    pref = _mm(lower, oh.astype(BF16)) + carry_ref[0:1, :]
    r1 = jnp.sum(jnp.where(oh1, pref, 0.0), axis=-1, keepdims=True)
    r2 = jnp.sum(jnp.where(oh2, pref, 0.0), axis=-1, keepdims=True)
    carry_ref[0:1, :] = carry_ref[0:1, :] + jnp.sum(oh, axis=0, keepdims=True)
    cnt_ref[...] = carry_ref[...]

    meta = jnp.where(lane == 0, i1, 0.0) + jnp.where(lane == 1, i2, 0.0) + jnp.where(lane == 2, r1, 0.0) \
        + jnp.where(lane == 3, r2, 0.0) + jnp.where(lane == 4, w1, 0.0) + jnp.where(lane == 5, w2, 0.0)
    meta_ref[...] = meta


def moe_router(x2d, g, shift, scale, wr_hi, wr_lo, br, tm, tiles_per_mod):
    m, d = x2d.shape
    r = shift.shape[1]
    mod_spec = pl.BlockSpec((1, r, d), lambda i: (i // tiles_per_mod, 0, 0))
    kern = functools.partial(_router_kernel, tm=tm)
    return pl.pallas_call(
        kern,
        out_shape=[jax.ShapeDtypeStruct((m, d // 2), U32), jax.ShapeDtypeStruct((m, LANES), F32),
                   jax.ShapeDtypeStruct((8, LANES), F32)],
        grid=(m // tm,),
        in_specs=[
            pl.BlockSpec((tm, d), lambda i: (i, 0)),
            pl.BlockSpec((1, d), lambda i: (0, 0)),
            mod_spec, mod_spec,
            pl.BlockSpec((d, LANES), lambda i: (0, 0)),
            pl.BlockSpec((d, LANES), lambda i: (0, 0)),
            pl.BlockSpec((1, LANES), lambda i: (0, 0)),
        ],
        out_specs=[pl.BlockSpec((tm, d // 2), lambda i: (i, 0)), pl.BlockSpec((tm, LANES), lambda i: (i, 0)),
                   pl.BlockSpec((8, LANES), lambda i: (0, 0))],
        scratch_shapes=[pltpu.VMEM((8, LANES), F32)],
        compiler_params=_cp(("arbitrary",)),
        name="moe_router",
    )(x2d, g, shift, scale, wr_hi, wr_lo, br)


def _scatter_kernel(seg_ref, has_ref, nu_ref, dest_ref, h_ref, xs_ref, zbuf, hbuf, sem, zsem,
                    *, tm, n_tiles, n_steps):
    i = pl.program_id(0)

    def zero_copy(row0):
        return pltpu.make_async_copy(zbuf, xs_ref.at[pl.ds(pl.multiple_of(row0, TE), TE), :], zsem)

    @pl.when(i == 0)
    def _():
        zbuf[...] = jnp.zeros(zbuf.shape, zbuf.dtype)
        for e in range(N_EXP):
            pl.when(has_ref[e] > 0)(lambda e=e: zero_copy(seg_ref[e]).start())

        def tail_start(tl, c):
            zero_copy(tl * TE).start()
            return c

        def tail_wait(tl, c):
            zero_copy(tl * TE).wait()
            return c

        lax.fori_loop(nu_ref[0], n_tiles, tail_start, 0)
        for e in range(N_EXP):
            pl.when(has_ref[e] > 0)(lambda e=e: zero_copy(seg_ref[e]).wait())
        lax.fori_loop(nu_ref[0], n_tiles, tail_wait, 0)

    slot = i % 2
    hbuf[slot] = h_ref[...]

    def row_copy(s, r, k):
        return pltpu.make_async_copy(hbuf.at[s, pl.ds(r, 1), :], xs_ref.at[pl.ds(dest_ref[2 * r + k], 1), :],
                                     sem.at[s])

    def start(r, c):
        row_copy(slot, r, 0).start(priority=0)
        row_copy(slot, r, 1).start(priority=1)
        return c

    def wait_slot(s):
        def wait(r, c):
            row_copy(s, r, 0).wait()
            row_copy(s, r, 1).wait()
            return c
        lax.fori_loop(0, tm, wait, 0, unroll=8)

    lax.fori_loop(0, tm, start, 0, unroll=8)
    pl.when(i > 0)(functools.partial(wait_slot, 1 - slot))
    pl.when(i == n_steps - 1)(functools.partial(wait_slot, slot))


def moe_scatter(h, dest_flat, seg_last, seg_has, n_used, n_rows, tm):
    m, d = h.shape
    kern = functools.partial(_scatter_kernel, tm=tm, n_tiles=n_rows // TE, n_steps=m // tm)
    gs = pltpu.PrefetchScalarGridSpec(
        num_scalar_prefetch=3,
        grid=(m // tm,),
        in_specs=[
            pl.BlockSpec((2 * tm,), lambda i, *_: (i,), memory_space=pltpu.SMEM),
            pl.BlockSpec((tm, d), lambda i, *_: (i, 0)),
        ],
        out_specs=pl.BlockSpec(memory_space=pl.ANY),
        scratch_shapes=[pltpu.VMEM((TE, d), h.dtype), pltpu.VMEM((2, tm, d), h.dtype),
                        pltpu.SemaphoreType.DMA((2,)), pltpu.SemaphoreType.DMA(())],
    )
    return pl.pallas_call(
        kern,
        out_shape=jax.ShapeDtypeStruct((n_rows, d), h.dtype),
        grid_spec=gs,
        compiler_params=_cp(("arbitrary",)),
        name="moe_scatter",
    )(seg_last, seg_has, n_used, dest_flat, h)


def _expert_kernel(te_ref, nu_ref, xs_ref, wg_ref, wu_ref, wd_ref, ys_ref):
    i = pl.program_id(0)

    @pl.when(i < nu_ref[0])
    def _():
        xb = _unpack_bf16_pairs(xs_ref[...]).astype(BF16)
        a = _mm(xb, wg_ref[0, 0].astype(BF16))
        u = _mm(xb, wu_ref[0, 0].astype(BF16))
        hid = (a * _sigmoid(a) * u).astype(BF16)
        ys_ref[...] = _pack_bf16_pairs(_mm(hid, wd_ref[0, 0].astype(BF16)).astype(BF16))

    @pl.when(i >= nu_ref[0])
    def _():
        ys_ref[...] = jnp.zeros(ys_ref.shape, U32)


def moe_experts(xs, tile_expert, n_used, w_gate, w_up, w_down, layer):
    n_rows, dh = xs.shape
    d = 2 * dh
    f = w_gate.shape[-1]
    nt = n_rows // TE
    gs = pltpu.PrefetchScalarGridSpec(
        num_scalar_prefetch=2,
        grid=(nt,),
        in_specs=[
            pl.BlockSpec((TE, dh), lambda i, te, nu: (jnp.minimum(i, nu[0] - 1), 0)),
            pl.BlockSpec((1, 1, d, f), lambda i, te, nu: (layer, te[i], 0, 0)),
            pl.BlockSpec((1, 1, d, f), lambda i, te, nu: (layer, te[i], 0, 0)),
            pl.BlockSpec((1, 1, f, d), lambda i, te, nu: (layer, te[i], 0, 0)),
        ],
        out_specs=pl.BlockSpec((TE, dh), lambda i, te, nu: (i, 0)),
    )
    return pl.pallas_call(
        _expert_kernel,
        out_shape=jax.ShapeDtypeStruct((n_rows, dh), U32),
        grid_spec=gs,
        compiler_params=_cp(("arbitrary",)),
        name="moe_experts",
    )(tile_expert, n_used, xs, w_gate, w_up, w_down)


def _combine_kernel(dcur_ref, dnext_ref, x_ref, g_ref, meta_ref, fg_ref, ys_ref, o_ref, rows, sem,
                    *, tm, final, n_steps):
    i = pl.program_id(0)
    slot = i % 2

    def row_copy(dref, s, r, k):
        return pltpu.make_async_copy(ys_ref.at[pl.ds(dref[2 * r + k], 1), :],
                                     rows.at[s, k, pl.ds(r, 1), :], sem.at[s])

    def gather(dref, s):
        def start(r, c):
            row_copy(dref, s, r, 0).start(priority=0)
            row_copy(dref, s, r, 1).start(priority=1)
            return c
        lax.fori_loop(0, tm, start, 0, unroll=8)

    pl.when(i == 0)(functools.partial(gather, dcur_ref, 0))
    pl.when(i + 1 < n_steps)(functools.partial(gather, dnext_ref, 1 - slot))

    def wait(r, c):
        row_copy(dcur_ref, slot, r, 0).wait()
        row_copy(dcur_ref, slot, r, 1).wait()
        return c

    lax.fori_loop(0, tm, wait, 0, unroll=8)
    meta = meta_ref[...]
    y = meta[:, 4:5] * _unpack_bf16_pairs(rows[slot, 0]) + meta[:, 5:6] * _unpack_bf16_pairs(rows[slot, 1])
    x = x_ref[...] + g_ref[0] * y
    if final:
        x = (x * lax.rsqrt(jnp.mean(x * x, axis=-1, keepdims=True) + EPS)) * fg_ref[...]
    o_ref[...] = x


def moe_combine(x2d, gate, meta, final_g, ys, dest_flat, tm, tiles_per_mod, final):
    m, d = x2d.shape
    r = gate.shape[1]
    n_steps = m // tm
    kern = functools.partial(_combine_kernel, tm=tm, final=final, n_steps=n_steps)
    return pl.pallas_call(
        kern,
        out_shape=jax.ShapeDtypeStruct((m, d), F32),
        grid=(n_steps,),
        in_specs=[
            pl.BlockSpec((2 * tm,), lambda i: (i,), memory_space=pltpu.SMEM),
            pl.BlockSpec((2 * tm,), lambda i: (jnp.minimum(i + 1, n_steps - 1),), memory_space=pltpu.SMEM),
            pl.BlockSpec((tm, d), lambda i: (i, 0)),
            pl.BlockSpec((1, r, d), lambda i: (i // tiles_per_mod, 0, 0)),
            pl.BlockSpec((tm, LANES), lambda i: (i, 0)),
            pl.BlockSpec((1, d), lambda i: (0, 0)),
            pl.BlockSpec(memory_space=pl.ANY),
        ],
        out_specs=pl.BlockSpec((tm, d), lambda i: (i, 0)),
        scratch_shapes=[pltpu.VMEM((2, 2, tm, d // 2), U32), pltpu.SemaphoreType.DMA((2,))],
        compiler_params=_cp(("arbitrary",)),
        name="moe_combine",
    )(dest_flat, dest_flat, x2d, gate, meta, final_g, ys)


def hier_moe_block(x2d, g_ffn, shift, scale, gate, p_layer, layer, tm, tiles_per_mod, final_g, final):
    m, d = x2d.shape
    wr_hi, wr_lo, br, w_gate, w_up, w_down = p_layer
    h, meta, cnt = moe_router(x2d, g_ffn, shift, scale, wr_hi, wr_lo, br, tm, tiles_per_mod)
    counts = cnt[0, :N_EXP].astype(I32)
    padded = ((counts + TE - 1) // TE) * TE
    ends = jnp.cumsum(padded)
    offs = ends - padded
    eid = meta[:, 0:2].astype(I32)
    off_sel = jnp.sum(jnp.where(eid[:, :, None] == jnp.arange(N_EXP, dtype=I32)[None, None, :],
                                offs[None, None, :], 0), axis=-1)
    dest = (off_sel + meta[:, 2:4].astype(I32)).reshape(-1)
    n_tiles = (2 * m) // TE + N_EXP
    n_used = (ends[-1] // TE).astype(I32).reshape(1)
    tile_start = jnp.arange(n_tiles, dtype=I32) * TE
    tile_clamped = jnp.minimum(tile_start, ends[-1] - 1)
    tile_expert = jnp.minimum(jnp.sum((tile_clamped[:, None] >= ends[None, :]).astype(I32), axis=1), N_EXP - 1)
    seg_last = jnp.maximum(ends - TE, 0).astype(I32)
    seg_has = (counts > 0).astype(I32)
    xs = moe_scatter(h, dest, seg_last, seg_has, n_used, n_tiles * TE, tm)
    ys = moe_experts(xs, tile_expert, n_used, w_gate, w_up, w_down, layer)
    return moe_combine(x2d, gate, meta, final_g, ys, dest, tm, tiles_per_mod, final)


def _pw1_kernel(x_ref, g_ref, sh_ref, sc_ref, wa_ref, wb_ref, ba_ref, bb_ref, u_ref, h_scr):
    @pl.when(pl.program_id(1) == 0)
    def _():
        h_scr[...] = _rms_mod(x_ref[...], g_ref[...], sh_ref[0], sc_ref[0]).astype(BF16)

    h = h_scr[...]
    a = _mm(h, wa_ref[...]) + ba_ref[...]
    b = _mm(h, wb_ref[...]) + bb_ref[...]
    u_ref[...] = a * _sigmoid(b)


def conv_pw1_glu(x2d, g, shift, scale, w_pw1, b_pw1, tm, tiles_per_mod):
    m, d = x2d.shape
    r = shift.shape[1]
    dc = w_pw1.shape[1] // 2
    tn = 512 if dc % 512 == 0 else 256
    nj = dc // tn
    mod_spec = pl.BlockSpec((1, r, d), lambda i, j: (i // tiles_per_mod, 0, 0))
    return pl.pallas_call(
        _pw1_kernel,
        out_shape=jax.ShapeDtypeStruct((m, dc), F32),
        grid=(m // tm, nj),
        in_specs=[
            pl.BlockSpec((tm, d), lambda i, j: (i, 0)),
            pl.BlockSpec((1, d), lambda i, j: (0, 0)),
            mod_spec, mod_spec,
            pl.BlockSpec((d, tn), lambda i, j: (0, j)),
            pl.BlockSpec((d, tn), lambda i, j: (0, nj + j)),
            pl.BlockSpec((1, tn), lambda i, j: (0, j)),
            pl.BlockSpec((1, tn), lambda i, j: (0, nj + j)),
        ],
        out_specs=pl.BlockSpec((tm, tn), lambda i, j: (i, j)),
        scratch_shapes=[pltpu.VMEM((tm, d), BF16)],
        compiler_params=_cp(("arbitrary", "arbitrary")),
        name="conv_pw1_glu",
    )(x2d, g, shift, scale, w_pw1, w_pw1, b_pw1, b_pw1)


_TAIL = 32


def _ln_silu(y, g, b):
    yc = y - jnp.mean(y, axis=-1, keepdims=True)
    z = yc * lax.rsqrt(jnp.mean(yc * yc, axis=-1, keepdims=True) + EPS)
    z = z * g + b
    return z * _sigmoid(z)


_CONV_CH = 512


def _conv_prompt_kernel(u_ref, wdw_ref, bdw_ref, lg_ref, lb_ref, w2_ref, b2_ref, x_ref, g_ref, o_ref, buf, sh,
                        acc_scr, *, tt):
    ti = pl.program_id(1)

    @pl.when(ti == 0)
    def _():
        buf[0:_TAIL, :] = jnp.zeros((_TAIL, buf.shape[1]), F32)

    buf[_TAIL:_TAIL + tt, :] = u_ref[...]
    off = _TAIL - (CONV_W - 1)
    sub = 8
    span = tt + _TAIL - sub
    dc = buf.shape[1]
    ch = sh.shape[2]
    for c0 in range(0, dc, ch):
        cs = slice(c0, c0 + ch)
        for r in range(1, sub):
            sh[r, 0:span, :] = buf[r:r + span, cs]
        acc = None
        for k in range(CONV_W):
            a, r = divmod(off + k, sub)
            src = buf[sub * a:sub * a + tt, cs] if r == 0 else sh[r, sub * a:sub * a + tt, :]
            term = wdw_ref[k:k + 1, cs] * src
            acc = term if acc is None else acc + term
        acc_scr[:, cs] = acc
    tail = buf[tt:tt + _TAIL, :]
    buf[0:_TAIL, :] = tail
    z = _ln_silu(acc_scr[...] + bdw_ref[...], lg_ref[...], lb_ref[...]).astype(BF16)
    out = _mm(z, w2_ref[...]) + b2_ref[...]
    o_ref[...] = x_ref[...] + g_ref[0] * out


def conv_prompt(u, w_dw, b_dw, ln_g, ln_b, w_pw2, b_pw2, x2d, gate, b, t):
    m, d = x2d.shape
    dc = u.shape[1]
    tt = 256
    nt = t // tt
    kern = functools.partial(_conv_prompt_kernel, tt=tt)
    vec = lambda n: pl.BlockSpec((1, n), lambda i, j: (0, 0))
    return pl.pallas_call(
        kern,
        out_shape=jax.ShapeDtypeStruct((m, d), F32),
        grid=(b, nt),
        in_specs=[
            pl.BlockSpec((tt, dc), lambda i, j: (i * nt + j, 0)),
            pl.BlockSpec((_TAIL, dc), lambda i, j: (0, 0)),
            vec(dc), vec(dc), vec(dc),
            pl.BlockSpec((dc, d), lambda i, j: (0, 0)),
            vec(d),
            pl.BlockSpec((tt, d), lambda i, j: (i * nt + j, 0)),
            pl.BlockSpec((1, 1, d), lambda i, j: (i, 0, 0)),
        ],
        out_specs=pl.BlockSpec((tt, d), lambda i, j: (i * nt + j, 0)),
        scratch_shapes=[pltpu.VMEM((_TAIL + tt, dc), F32), pltpu.VMEM((8, _TAIL + tt, min(_CONV_CH, dc)), F32),
                        pltpu.VMEM((tt, dc), F32)],
        compiler_params=_cp(("arbitrary", "arbitrary")),
        name="conv_prompt",
    )(u, w_dw, b_dw, ln_g, ln_b, w_pw2, b_pw2, x2d, gate)


def _prep_params(p):
    d = p["attn_w_in"].shape[1]
    nsa_q = NSA_G * NSA_HPG * HD
    nsa_kv = 2 * NSA_G * HD
    gate_w = 3 * NSA_G * NSA_HPG
    c_gate = nsa_q + 3 * nsa_kv
    w_in = p["attn_w_in"][0]
    w_main = jnp.concatenate([w_in[:, :c_gate], w_in[:, c_gate + gate_w:]], axis=1).astype(BF16)
    w_gate = jnp.pad(w_in[:, c_gate:c_gate + gate_w], ((0, 0), (0, LANES - gate_w))).astype(BF16)
    w1 = p["nsa_cmp_w1"][0]
    wab = jnp.concatenate([w1[:, :CMP_STRIDE].reshape(2, CMP_STRIDE * HD, HD),
                           w1[:, CMP_STRIDE:].reshape(2, CMP_STRIDE * HD, HD)], axis=2).astype(BF16)
    pe = p["nsa_cmp_pe"][0].reshape(2, 2, CMP_STRIDE * HD)
    pe2 = jnp.pad(pe, ((0, 0), (0, 14), (0, 0))).astype(BF16)
    w2 = p["nsa_cmp_w2"][0].astype(BF16)
    moe = []
    for layer in range(p["moe_wg"].shape[0]):
        wr = jnp.zeros((d, LANES), F32)
        wr = wr.at[:, :N_EXP].set(p["moe_we"][layer]).at[:, _GROUP_LANE0:_GROUP_LANE0 + N_GROUPS].set(
            p["moe_wg"][layer])
        br = jnp.zeros((1, LANES), F32)
        br = br.at[0, :N_EXP].set(p["moe_be"][layer]).at[0, _GROUP_LANE0:_GROUP_LANE0 + N_GROUPS].set(
            p["moe_bg"][layer])
        hi, lo = _split(wr)
        moe.append((hi, lo, br, p["moe_w_gate"], p["moe_w_up"], p["moe_w_down"]))
    w_dw = jnp.pad(p["conv_w_dw"][0], ((0, _TAIL - CONV_W), (0, 0)))
    return dict(
        w_main=w_main, w_gate=w_gate, wab=wab, pe2=pe2, w2=w2,
        w_out=p["attn_w_out"][0].astype(BF16), moe=moe,
        w_pw1=p["conv_w_pw1"][0].astype(BF16), b_pw1=p["conv_b_pw1"][0][None, :],
        w_dw=w_dw, b_dw=p["conv_b_dw"][0][None, :], ln_g=p["conv_ln_g"][0][None, :],
        ln_b=p["conv_ln_b"][0][None, :], w_pw2=p["conv_w_pw2"][0].astype(BF16),
        b_pw2=p["conv_b_pw2"][0][None, :],
        norm_mix_g=p["norm_mix_g"], norm_ffn_g=p["norm_ffn_g"], final_g=p["final_norm_g"][None, :],
    )


def _trunk_prompt(x, mods, pp):
    b, t, d = x.shape
    m = b * t
    x2d = x.reshape(m, d)
    sh1, sc1, g1, sh2, sc2, g2 = mods[0]
    tm = 512
    qn, kvc, kvs, kvw, qm, kvm, gate, kvm_bf = attn_in_proj(
        x2d, pp["norm_mix_g"][0:1], sh1, sc1, pp["w_main"], pp["w_gate"], tm, t // tm)
    ck, cv = compress_prompt(kvc, b, t, pp["wab"], pp["pe2"], pp["w2"])
    o_nsa = nsa_prompt(qn, ck, cv, kvs, kvw, gate, b, t)
    o_moba = moba_prompt(qm, kvm_bf, moba_kmean(kvm, b, t), b, t)
    tm2 = 256
    x1 = attn_out_proj(o_nsa, o_moba, pp["w_out"], x2d, g1, tm2, t // tm2)
    x2 = hier_moe_block(x1, pp["norm_ffn_g"][0:1], sh2, sc2, g2, pp["moe"][0], 0, tm2, t // tm2,
                        pp["final_g"], False)
    sh1, sc1, g1, sh2, sc2, g2 = mods[1]
    u = conv_pw1_glu(x2, pp["norm_mix_g"][1:2], sh1, sc1, pp["w_pw1"], pp["b_pw1"], tm, t // tm)
    x3 = conv_prompt(u, pp["w_dw"], pp["b_dw"], pp["ln_g"], pp["ln_b"], pp["w_pw2"], pp["b_pw2"], x2, g1, b, t)
    y = hier_moe_block(x3, pp["norm_ffn_g"][1:2], sh2, sc2, g2, pp["moe"][1], 1, tm2, t // tm2,
                       pp["final_g"], True)
    wlen = min(WINDOW, t)
    state = (
        kvc.reshape(1, b, t, 2, NSA_G, HD), kvs.reshape(1, b, t, 2, NSA_G, HD),
        kvw.reshape(b, t, 2, NSA_G, HD)[None, :, t - wlen:], kvm.reshape(1, b, t, 2, MOBA_H, HD),
        u.reshape(b, t, -1)[None, :, t - (CONV_W - 1):],
    )
    return y.reshape(b, t, d), state


TPAD = 8


def _page_specs(npg, n_pages, block, col_block):
    nd = len(block)

    def spec(i):
        def imap(b, j, pt):
            return (pt[b * n_pages + j * npg + i],) + (0,) * (nd - 2) + (col_block,)
        return pl.BlockSpec(block, imap)

    return [spec(i) for i in range(npg)]


def _compress_sample_kernel(pt_ref, *refs, npg, nsteps):
    pages = refs[:npg]
    wab_ref, pe_ref, w2_ref, ck_ref, cv_ref, ab_scr = refs[npg:]
    j = pl.program_id(1)
    rows = npg * (PAGE // CMP_STRIDE)
    r0 = pl.multiple_of(j * rows, rows)
    gpp = PAGE // CMP_STRIDE
    rpt = 2 * NSA_G
    for c in range(2):
        for g in range(NSA_G):
            cg = c * NSA_G + g
            xs = jnp.concatenate(
                [jnp.concatenate([pg[pl.ds(r * rpt + cg, gpp, stride=CMP_STRIDE * rpt), :]
                                  for r in range(CMP_STRIDE)], axis=1) for pg in pages], axis=0).astype(BF16)
            ab_scr[cg, pl.ds(r0, rows), :] = _mm(xs, wab_ref[c])

    @pl.when(j == nsteps - 1)
    def _():
        for c, ref in ((0, ck_ref), (1, cv_ref)):
            for g in range(NSA_G):
                ref[0, g] = _compress_finish(ab_scr[c * NSA_G + g], c, pe_ref, wab_ref, w2_ref).astype(BF16)


def compress_sample(pt, cache, bs, n_pages, wab, pe2, w2):
    npg = 32
    nsteps = n_pages // npg
    gpp = PAGE // CMP_STRIDE
    ng = n_pages * gpp
    x = cache.reshape(-1, HD)
    kern = functools.partial(_compress_sample_kernel, npg=npg, nsteps=nsteps)
    out = jax.ShapeDtypeStruct((bs, NSA_G, ng, HD), BF16)
    ospec = pl.BlockSpec((1, NSA_G, ng, HD), lambda b, j, pt: (b, 0, 0, 0))
    const = lambda a: pl.BlockSpec(a.shape, lambda b, j, pt: (0,) * a.ndim)
    gs = pltpu.PrefetchScalarGridSpec(
        num_scalar_prefetch=1,
        grid=(bs, nsteps),
        in_specs=_page_specs(npg, n_pages, (PAGE * 2 * NSA_G, HD), 0) + [const(wab), const(pe2), const(w2)],
        out_specs=[ospec, ospec],
        scratch_shapes=[pltpu.VMEM((2 * NSA_G, ng, 2 * HD), F32)],
    )
    return pl.pallas_call(
        kern, out_shape=[out, out], grid_spec=gs,
        compiler_params=_cp(("arbitrary", "arbitrary")), name="nsa_compress_sample",
    )(pt, *([x] * npg), wab, pe2, w2)


def _nsa_sample_kernel(pt_ref, q_ref, ck_ref, cv_ref, ksn_ref, kwn_ref, win_ref, gate_ref, c2s_ref, *refs,
                       npg, nsteps, past, ts, n_cmp, n_slc):
    pages = refs[:npg]
    o_ref, selx_scr, m_scr, l_scr, acc_scr, part_scr = refs[npg:]
    j = pl.program_id(1)
    qr = NSA_HPG * TPAD
    keys = npg * PAGE
    rpt = 2 * NSA_G
    wbuf = win_ref.shape[0] // rpt
    t_row = past + (lax.broadcasted_iota(I32, (qr, 1), 0) & (TPAD - 1))
    lane = lax.broadcasted_iota(I32, (1, LANES), 1)

    @pl.when(j == 0)
    def _():
        t8 = past + lax.broadcasted_iota(I32, (TPAD, 1), 0)
        for g in range(NSA_G):
            q = (q_ref[0, g] * SCALE).astype(BF16)
            gt = gate_ref[0, g]
            ncg = ck_ref.shape[2]
            lane_c = lax.broadcasted_iota(I32, (1, ncg), 1)
            valid = (lane_c * CMP_STRIDE + (CMP_BLOCK - 1) <= t_row) & (lane_c < n_cmp)
            p = _masked_softmax(_nt(q, ck_ref[0, g]), valid)
            o_cmp = _mm(p.astype(BF16), cv_ref[0, g])
            psum = p[0:TPAD] + p[TPAD:2 * TPAD] + p[2 * TPAD:3 * TPAD] + p[3 * TPAD:4 * TPAD]
            phi, plo = _split(psum)
            imp = _mm(phi, c2s_ref[...]) + _mm(plo, c2s_ref[...])
            nl = c2s_ref.shape[1]
            lane_s = lax.broadcasted_iota(I32, (1, nl), 1)
            qb = t8 >> 6
            forced = (lane_s == 0) | (lane_s == qb) | (lane_s == qb - 1)
            score = jnp.where(forced, FORCED, imp)
            score = jnp.where(lane_s * SLC_BLOCK <= t8, score, -1.0)
            score = jnp.where(lane_s < n_slc, score, -2.0)
            sel = _select_blocks(score, lane_s, n_slc, min(SLC_TOPN, n_slc)).astype(BF16)
            bps = keys // SLC_BLOCK
            for jj in range(nsteps):
                blk = lax.broadcasted_iota(I32, (nl, keys), 0)
                key = lax.broadcasted_iota(I32, (nl, keys), 1)
                ee = (blk == jj * bps + (key >> 6)).astype(BF16)
                selx_scr[g, jj] = _mm(sel, ee)
            own = jnp.sum(jnp.where(lane_s == (past >> 6), sel.astype(F32), 0.0), axis=-1, keepdims=True)
            own4 = jnp.concatenate([own] * NSA_HPG, axis=0) > 0.5
            zpad = jnp.zeros((LANES - TPAD, HD), F32)
            kw = jnp.concatenate([win_ref[pl.ds(g, wbuf, stride=rpt), :],
                                  kwn_ref[0, :, g * HD:(g + 1) * HD], zpad], axis=0).astype(BF16)
            vw = jnp.concatenate([win_ref[pl.ds(NSA_G + g, wbuf, stride=rpt), :],
                                  kwn_ref[0, :, (NSA_G + g) * HD:(NSA_G + g + 1) * HD], zpad], axis=0).astype(BF16)
            idx = lax.broadcasted_iota(I32, (1, wbuf + LANES), 1)
            kpos = past - wbuf + idx
            okw = (idx < wbuf + ts) & (kpos <= t_row) & (kpos > t_row - WINDOW)
            o_win = _mm(_masked_softmax(_nt(q, kw), okw).astype(BF16), vw)
            part_scr[g] = gt[:, 0:1] * o_cmp + gt[:, 2:3] * o_win
            kn = jnp.concatenate([ksn_ref[0, :, g * HD:(g + 1) * HD], zpad], axis=0).astype(BF16)
            vn = jnp.concatenate([ksn_ref[0, :, (NSA_G + g) * HD:(NSA_G + g + 1) * HD], zpad], axis=0).astype(BF16)
            ok0 = (past + lane <= t_row) & (lane < ts) & own4
            s0 = jnp.where(ok0, _nt(q, kn), NEG)
            m0 = jnp.max(s0, axis=-1, keepdims=True)
            p0 = jnp.where(ok0, jnp.exp(s0 - m0), 0.0)
            m_scr[g] = m0
            l_scr[g] = jnp.sum(p0, axis=-1, keepdims=True)
            acc_scr[g] = _mm(p0.astype(BF16), vn)

    for g in range(NSA_G):
        q = (q_ref[0, g] * SCALE).astype(BF16)
        k = jnp.concatenate([pg[pl.ds(g, PAGE, stride=rpt), :] for pg in pages], axis=0).astype(BF16)
        v = jnp.concatenate([pg[pl.ds(NSA_G + g, PAGE, stride=rpt), :] for pg in pages], axis=0).astype(BF16)
        mk = selx_scr[g, j]
        ok = jnp.concatenate([mk] * NSA_HPG, axis=0) > 0.5
        s = jnp.where(ok, _nt(q, k), NEG)
        m_old = m_scr[g]
        m_new = jnp.maximum(m_old, jnp.max(s, axis=-1, keepdims=True))
        alpha = jnp.exp(m_old - m_new)
        pk = jnp.where(ok, jnp.exp(s - m_new), 0.0)
        m_scr[g] = m_new
        l_scr[g] = alpha * l_scr[g] + jnp.sum(pk, axis=-1, keepdims=True)
        acc_scr[g] = alpha * acc_scr[g] + _mm(pk.astype(BF16), v)

    @pl.when(j == nsteps - 1)
    def _():
        for g in range(NSA_G):
            o_slc = acc_scr[g] / jnp.maximum(l_scr[g], 1e-30)
            o_ref[0, g] = (part_scr[g] + gate_ref[0, g][:, 1:2] * o_slc).astype(BF16)


def nsa_sample(pt, q_g, ck, cv, ksn, kwn, win, gate_g, cache, bs, n_pages, ts):
    npg = 8
    nsteps = n_pages // npg
    past = n_pages * PAGE
    n_cmp = (past + ts - CMP_BLOCK) // CMP_STRIDE + 1
    n_slc = -(-(past + ts) // SLC_BLOCK)
    ncg = ck.shape[2]
    nl = -(-n_slc // LANES) * LANES
    c2s = np.zeros((ncg, nl), np.float32)
    c2s[:n_cmp, :n_slc] = _cmp_to_slc(n_cmp, n_slc)
    c2s = jnp.asarray(c2s, BF16)
    qr = NSA_HPG * TPAD
    keys = npg * PAGE
    rpt = 2 * NSA_G
    x = cache.reshape(-1, HD)
    kern = functools.partial(_nsa_sample_kernel, npg=npg, nsteps=nsteps, past=past, ts=ts, n_cmp=n_cmp,
                             n_slc=n_slc)

    def per_b(a):
        return pl.BlockSpec((1,) + a.shape[1:], lambda b, j, pt: (b,) + (0,) * (a.ndim - 1))

    win_rows = win.shape[0] // bs
    gs = pltpu.PrefetchScalarGridSpec(
        num_scalar_prefetch=1,
        grid=(bs, nsteps),
        in_specs=[per_b(q_g), per_b(ck), per_b(cv), per_b(ksn), per_b(kwn),
                  pl.BlockSpec((win_rows, HD), lambda b, j, pt: (b, 0)), per_b(gate_g),
                  pl.BlockSpec(c2s.shape, lambda b, j, pt: (0, 0))]
        + _page_specs(npg, n_pages, (PAGE * rpt, HD), 0),
        out_specs=pl.BlockSpec((1, NSA_G, qr, HD), lambda b, j, pt: (b, 0, 0, 0)),
        scratch_shapes=[pltpu.VMEM((NSA_G, nsteps, TPAD, keys), F32), pltpu.VMEM((NSA_G, qr, 1), F32),
                        pltpu.VMEM((NSA_G, qr, 1), F32), pltpu.VMEM((NSA_G, qr, HD), F32),
                        pltpu.VMEM((NSA_G, qr, HD), F32)],
    )
    return pl.pallas_call(
        kern, out_shape=jax.ShapeDtypeStruct((bs, NSA_G, qr, HD), BF16), grid_spec=gs,
        compiler_params=_cp(("arbitrary", "arbitrary")), name="nsa_sample",
    )(pt, q_g, ck, cv, ksn, kwn, win, gate_g, c2s, *([x] * npg))


def _moba_gate_kernel(pt_ref, q_ref, *refs, npg, nsteps, nb_past):
    pages = refs[:npg]
    sel_ref, km_scr = refs[npg:]
    j = pl.program_id(1)
    ppb = MOBA_BLOCK // PAGE
    bps = npg // ppb

    @pl.when(j == 0)
    def _():
        km_scr[...] = jnp.zeros(km_scr.shape, F32)

    for i in range(bps):
        ssum = jnp.sum(pages[ppb * i][...], axis=0)[0]
        for pp_ in range(1, ppb):
            ssum = ssum + jnp.sum(pages[ppb * i + pp_][...], axis=0)[0]
        ssum = ssum * (1.0 / MOBA_BLOCK)
        for h in range(MOBA_H):
            km_scr[h, pl.ds(j * bps + i, 1), :] = ssum[h:h + 1, :]

    @pl.when(j == nsteps - 1)
    def _():
        lane = lax.broadcasted_iota(I32, (1, LANES), 1)
        for h in range(MOBA_H):
            qhi, qlo = _split(q_ref[0, h])
            khi, klo = _split(km_scr[h])
            gate = _nt(qhi, khi) + _nt(qhi, klo) + _nt(qlo, khi)
            gate = jnp.where(lane < nb_past, gate, NEG)
            sel = _select_blocks(gate, lane, nb_past + 1, min(MOBA_TOPK, nb_past + 1)) & (lane < nb_past)
            sel_ref[0, h] = sel.astype(F32)


def moba_gate_sample(pt, q_h, cache, bs, n_pages):
    npg = 8
    nsteps = n_pages // npg
    hk = MOBA_H * HD
    x = cache.reshape(-1, 2, MOBA_H, HD)
    kern = functools.partial(_moba_gate_kernel, npg=npg, nsteps=nsteps, nb_past=n_pages * PAGE // MOBA_BLOCK)
    gs = pltpu.PrefetchScalarGridSpec(
        num_scalar_prefetch=1,
        grid=(bs, nsteps),
        in_specs=[pl.BlockSpec((1, MOBA_H, TPAD, HD), lambda b, j, pt: (b, 0, 0, 0))]
        + _page_specs(npg, n_pages, (PAGE, 1, MOBA_H, HD), 0),
        out_specs=pl.BlockSpec((1, MOBA_H, TPAD, LANES), lambda b, j, pt: (b, 0, 0, 0)),
        scratch_shapes=[pltpu.VMEM((MOBA_H, LANES, HD), F32)],
    )
    return pl.pallas_call(
        kern, out_shape=jax.ShapeDtypeStruct((bs, MOBA_H, TPAD, LANES), F32), grid_spec=gs,
        compiler_params=_cp(("arbitrary", "arbitrary")), name="moba_gate_sample",
    )(pt, q_h, *([x] * npg))


def _moba_sample_kernel(pt_ref, qbd_ref, sel_ref, kvn_ref, *refs, npg, nsteps, ts):
    pages = refs[:npg]
    o_ref, m_scr, l_scr, acc_scr = refs[npg:]
    j = pl.program_id(1)
    hk = MOBA_H * HD
    qr = MOBA_H * TPAD
    qbd = qbd_ref[0]
    lane = lax.broadcasted_iota(I32, (1, LANES), 1)

    def diag(o_all):
        return jnp.concatenate([o_all[h * TPAD:(h + 1) * TPAD, h * HD:(h + 1) * HD] for h in range(MOBA_H)],
                               axis=0)

    @pl.when(j == 0)
    def _():
        zpad = jnp.zeros((LANES - TPAD, hk), F32)
        kn = jnp.concatenate([kvn_ref[0, :, :hk], zpad], axis=0).astype(BF16)
        vn = jnp.concatenate([kvn_ref[0, :, hk:], zpad], axis=0).astype(BF16)
        t8 = lax.broadcasted_iota(I32, (qr, 1), 0) & (TPAD - 1)
        ok0 = (lane <= t8) & (lane < ts)
        s0 = jnp.where(ok0, _nt(qbd, kn), NEG)
        m0 = jnp.max(s0, axis=-1, keepdims=True)
        p0 = jnp.where(ok0, jnp.exp(s0 - m0), 0.0)
        m_scr[...] = m0
        l_scr[...] = jnp.sum(p0, axis=-1, keepdims=True)
        acc_scr[...] = diag(_mm(p0.astype(BF16), vn))

    rpt = 2 * MOBA_H

    def heads(pg, c):
        return jnp.concatenate([pg[pl.ds(c * MOBA_H + h, PAGE, stride=rpt), :] for h in range(MOBA_H)], axis=1)

    k = jnp.concatenate([heads(pg, 0) for pg in pages], axis=0).astype(BF16)
    v = jnp.concatenate([heads(pg, 1) for pg in pages], axis=0).astype(BF16)
    s = _nt(qbd, k)
    sel = sel_ref[0]
    bps = npg * PAGE // MOBA_BLOCK
    cols = []
    for bb in range(bps):
        on = jnp.sum(jnp.where(lane == j * bps + bb, sel, 0.0), axis=-1, keepdims=True)
        cols.append(jnp.broadcast_to(on, (qr, MOBA_BLOCK)))
    ok = jnp.concatenate(cols, axis=1) > 0.5
    s = jnp.where(ok, s, NEG)
    m_old = m_scr[...]
    m_new = jnp.maximum(m_old, jnp.max(s, axis=-1, keepdims=True))
    alpha = jnp.exp(m_old - m_new)
    pk = jnp.where(ok, jnp.exp(s - m_new), 0.0)
    m_scr[...] = m_new
    l_scr[...] = alpha * l_scr[...] + jnp.sum(pk, axis=-1, keepdims=True)
    acc_scr[...] = alpha * acc_scr[...] + diag(_mm(pk.astype(BF16), v))

    @pl.when(j == nsteps - 1)
    def _():
        o_ref[0] = (acc_scr[...] / l_scr[...]).astype(BF16)


def moba_sample(pt, q_bd, sel, kvn, cache, bs, n_pages, ts):
    npg = 8
    nsteps = n_pages // npg
    hk = MOBA_H * HD
    qr = MOBA_H * TPAD
    x = cache.reshape(-1, HD)
    kern = functools.partial(_moba_sample_kernel, npg=npg, nsteps=nsteps, ts=ts)

    def per_b(a):
        return pl.BlockSpec((1,) + a.shape[1:], lambda b, j, pt: (b,) + (0,) * (a.ndim - 1))

    gs = pltpu.PrefetchScalarGridSpec(
        num_scalar_prefetch=1,
        grid=(bs, nsteps),
        in_specs=[per_b(q_bd), per_b(sel), per_b(kvn)]
        + _page_specs(npg, n_pages, (PAGE * 2 * MOBA_H, HD), 0),
        out_specs=pl.BlockSpec((1, qr, HD), lambda b, j, pt: (b, 0, 0)),
        scratch_shapes=[pltpu.VMEM((qr, 1), F32), pltpu.VMEM((qr, 1), F32), pltpu.VMEM((qr, HD), F32)],
    )
    return pl.pallas_call(
        kern, out_shape=jax.ShapeDtypeStruct((bs, qr, HD), BF16), grid_spec=gs,
        compiler_params=_cp(("arbitrary", "arbitrary")), name="moba_sample",
    )(pt, q_bd, sel, kvn, *([x] * npg))


def _conv_sample_kernel(st_ref, u_ref, wdw_ref, bdw_ref, lg_ref, lb_ref, w2_ref, b2_ref, x_ref, g_ref, o_ref,
                        *, ts):
    nst = CONV_W - 1
    bs = st_ref.shape[1]
    zs = []
    for t in range(ts):
        acc = None
        for k in range(CONV_W):
            r = t + k
            row = st_ref[r] if r < nst else u_ref[r - nst]
            term = wdw_ref[k:k + 1, :] * row
            acc = term if acc is None else acc + term
        zs.append(_ln_silu(acc + bdw_ref[...], lg_ref[...], lb_ref[...]))
    z = jnp.concatenate(zs, axis=0).astype(BF16)
    out = _mm(z, w2_ref[...]) + b2_ref[...]
    for t in range(ts):
        o_ref[t] = x_ref[t] + g_ref[t] * out[t * bs:(t + 1) * bs]


def conv_sample(st_t, u_t, w_dw, b_dw, ln_g, ln_b, w_pw2, b_pw2, x_t, g_t):
    ts = u_t.shape[0]
    kern = functools.partial(_conv_sample_kernel, ts=ts)
    full = lambda a: pl.BlockSpec(a.shape, lambda i: (0,) * a.ndim)
    args = (st_t, u_t, w_dw, b_dw, ln_g, ln_b, w_pw2, b_pw2, x_t, g_t)
    return pl.pallas_call(
        kern, out_shape=jax.ShapeDtypeStruct(x_t.shape, F32), grid=(1,),
        in_specs=[full(a) for a in args], out_specs=full(x_t),
        compiler_params=_cp(("arbitrary",)), name="conv_sample",
    )(*args)


def _trunk_sample(x, mods, pp, past):
    bs, ts, d = x.shape
    m = bs * ts
    n_pages = past["page_table"].shape[1]
    plen = n_pages * PAGE
    assert ts <= TPAD and ts < CMP_STRIDE and plen % MOBA_BLOCK == 0 and m % 8 == 0
    pt = past["page_table"].reshape(-1).astype(I32)
    x2d = x.reshape(m, d)
    sh1, sc1, g1, sh2, sc2, g2 = mods[0]
    qn, kvc, kvs, kvw, qm, kvm, gate, _ = attn_in_proj(
        x2d, pp["norm_mix_g"][0:1], sh1, sc1, pp["w_main"], pp["w_gate"], m, 1)

    def pad_t(a, axis):
        w = [(0, 0)] * a.ndim
        w[axis] = (0, TPAD - ts)
        return jnp.pad(a, w)

    ck, cv = compress_sample(pt, past["nsa_cmp"], bs, n_pages, pp["wab"], pp["pe2"], pp["w2"])
    q_g = pad_t(qn.reshape(bs, ts, NSA_G, NSA_HPG, HD).transpose(0, 2, 3, 1, 4), 3)
    q_g = q_g.reshape(bs, NSA_G, NSA_HPG * TPAD, HD)
    gate_g = pad_t(gate[:, :3 * NSA_G * NSA_HPG].reshape(bs, ts, NSA_G, NSA_HPG, 3).transpose(0, 2, 3, 1, 4), 3)
    gate_g = jnp.pad(gate_g.reshape(bs, NSA_G, NSA_HPG * TPAD, 3), ((0, 0), (0, 0), (0, 0), (0, LANES - 3)))
    ksn = pad_t(kvs.reshape(bs, ts, 4 * HD), 1)
    kwn = pad_t(kvw.reshape(bs, ts, 4 * HD), 1)
    win = past["nsa_win"].reshape(-1, HD)
    o_g = nsa_sample(pt, q_g, ck, cv, ksn, kwn, win, gate_g, past["nsa_slc"], bs, n_pages, ts)
    o_nsa = o_g.reshape(bs, NSA_G, NSA_HPG, TPAD, HD)[:, :, :, :ts].transpose(0, 3, 1, 2, 4).reshape(m, -1)

    q_h = pad_t(qm.reshape(bs, ts, MOBA_H, HD).transpose(0, 2, 1, 3), 2)
    sel = moba_gate_sample(pt, q_h, past["moba"], bs, n_pages)
    eye = jnp.eye(MOBA_H, dtype=F32)
    q_bd = ((q_h * SCALE)[:, :, :, None, :] * eye[None, :, None, :, None]).astype(BF16)
    q_bd = q_bd.reshape(bs, MOBA_H * TPAD, MOBA_H * HD)
    kvn = pad_t(kvm.reshape(bs, ts, 2 * MOBA_H * HD), 1)
    o_m = moba_sample(pt, q_bd, sel.reshape(bs, MOBA_H * TPAD, LANES), kvn, past["moba"], bs, n_pages, ts)
    o_moba = o_m.reshape(bs, MOBA_H, TPAD, HD)[:, :, :ts].transpose(0, 2, 1, 3).reshape(m, -1)

    x1 = attn_out_proj(o_nsa, o_moba, pp["w_out"], x2d, g1, m, 1)
    x2 = hier_moe_block(x1, pp["norm_ffn_g"][0:1], sh2, sc2, g2, pp["moe"][0], 0, m, 1, pp["final_g"], False)
    sh1, sc1, g1, sh2, sc2, g2 = mods[1]
    u = conv_pw1_glu(x2, pp["norm_mix_g"][1:2], sh1, sc1, pp["w_pw1"], pp["b_pw1"], m, 1)
    tb = lambda a: a.reshape(bs, ts, -1).transpose(1, 0, 2)
    st = past["conv"][0]
    x3_t = conv_sample(st.transpose(1, 0, 2), tb(u), pp["w_dw"], pp["b_dw"], pp["ln_g"], pp["ln_b"],
                       pp["w_pw2"], pp["b_pw2"], tb(x2), tb(g1[0]))
    x3 = x3_t.transpose(1, 0, 2).reshape(m, d)
    y = hier_moe_block(x3, pp["norm_ffn_g"][1:2], sh2, sc2, g2, pp["moe"][1], 1, m, 1, pp["final_g"], True)
    state = (
        kvc.reshape(1, bs, ts, 2, NSA_G, HD), kvs.reshape(1, bs, ts, 2, NSA_G, HD),
        jnp.concatenate([past["nsa_win"][0][:, ts:], kvw.reshape(bs, ts, 2, NSA_G, HD)], axis=1)[None],
        kvm.reshape(1, bs, ts, 2, MOBA_H, HD),
        jnp.concatenate([st[:, ts:], u.reshape(bs, ts, -1)], axis=1)[None],
    )
    return y.reshape(bs, ts, d), state


def _mods_from(m_all, rows, expand):
    out = []
    for layer in range(m_all.shape[0]):
        parts = jnp.split(m_all[layer, rows], 6, axis=-1)
        if expand:
            parts = [jnp.repeat(a, expand, axis=0)[None] for a in parts]
        else:
            parts = [a[:, None, :] for a in parts]
        out.append(parts)
    return out


def kernel(x_prompt, x_sample, cache_nsa_cmp_kv, cache_nsa_slc_kv, state_nsa_win_kv, cache_moba_kv, state_conv, page_table, c_prompt, c_sample, norm_mix_g, norm_ffn_g, ada_w, ada_b, attn_w_in, attn_w_out, nsa_cmp_pe, nsa_cmp_w1, nsa_cmp_w2, conv_w_pw1, conv_b_pw1, conv_w_dw, conv_b_dw, conv_ln_g, conv_ln_b, conv_w_pw2, conv_b_pw2, moe_wg, moe_bg, moe_we, moe_be, moe_w_gate, moe_w_up, moe_w_down, final_norm_g):
    p = dict(norm_mix_g=norm_mix_g, norm_ffn_g=norm_ffn_g, ada_w=ada_w, ada_b=ada_b, attn_w_in=attn_w_in,
             attn_w_out=attn_w_out, nsa_cmp_pe=nsa_cmp_pe, nsa_cmp_w1=nsa_cmp_w1, nsa_cmp_w2=nsa_cmp_w2,
             conv_w_pw1=conv_w_pw1, conv_b_pw1=conv_b_pw1, conv_w_dw=conv_w_dw, conv_b_dw=conv_b_dw,
             conv_ln_g=conv_ln_g, conv_ln_b=conv_ln_b, conv_w_pw2=conv_w_pw2, conv_b_pw2=conv_b_pw2,
             moe_wg=moe_wg, moe_bg=moe_bg, moe_we=moe_we, moe_be=moe_be, moe_w_gate=moe_w_gate,
             moe_w_up=moe_w_up, moe_w_down=moe_w_down, final_norm_g=final_norm_g)
    pp = _prep_params(p)
    bp = x_prompt.shape[0]
    bs, ts, d = x_sample.shape
    c_all = jnp.concatenate([c_prompt, c_sample], axis=0)
    pad = (-c_all.shape[0]) % 16
    c_all = jnp.pad(c_all, ((0, pad), (0, 0)))
    m_all = ada_params(c_all, ada_w, ada_b)
    mods_p = _mods_from(m_all, slice(0, bp), 0)
    y_p, (cmp_p, slc_p, win_p, moba_p, conv_p) = _trunk_prompt(x_prompt, mods_p, pp)
    mods_s = _mods_from(m_all, slice(bp, bp + bs), ts)
    past = dict(page_table=page_table, nsa_cmp=cache_nsa_cmp_kv, nsa_slc=cache_nsa_slc_kv,
                nsa_win=state_nsa_win_kv, moba=cache_moba_kv, conv=state_conv)
    y_s, (cmp_s, slc_s, win_s, moba_s, conv_s) = _trunk_sample(x_sample, mods_s, pp, past)
    return (y_p, y_s, cmp_p, cmp_s, slc_p, slc_s, win_p, win_s, moba_p, moba_s, conv_p, conv_s)
```

```python
import functools

import numpy as np
import jax
import jax.numpy as jnp
from jax import lax
from jax.experimental import pallas as pl
from jax.experimental.pallas import tpu as pltpu

F32 = jnp.float32
BF16 = jnp.bfloat16
I32 = jnp.int32

HD = 128
LANES = 128
SCALE = HD ** -0.5
NSA_G = 2
NSA_HPG = 4
CMP_BLOCK = 32
CMP_STRIDE = 16
SLC_BLOCK = 64
SLC_TOPN = 16
WINDOW = 512
FORCED = 1.0e4
MOBA_H = 8
MOBA_BLOCK = 256
MOBA_TOPK = 3
CONV_W = 31
N_GROUPS = 4
EPG = 8
N_EXP = N_GROUPS * EPG
PAGE = 128
EPS = 1e-6
NEG = -1e30
TE = 256
VMEM_LIMIT = 56 * 1024 * 1024


def _cp(sem, vmem=VMEM_LIMIT):
    return pltpu.CompilerParams(dimension_semantics=sem, vmem_limit_bytes=vmem)


def _nt(a, b):
    return lax.dot_general(a, b, (((1,), (1,)), ((), ())), preferred_element_type=F32)


def _mm(a, b):
    return jnp.dot(a, b, preferred_element_type=F32)


def _split(x):
    hi = x.astype(BF16)
    lo = (x - hi.astype(F32)).astype(BF16)
    return hi, lo


def _sigmoid(x):
    return 1.0 / (1.0 + jnp.exp(-x))


def _rms_mod(x, g, shift, scale):
    y = x * lax.rsqrt(jnp.mean(x * x, axis=-1, keepdims=True) + EPS)
    return (y * g) * (1.0 + scale) + shift


def _masked_softmax(s, valid):
    sm = jnp.where(valid, s, NEG)
    mx = jnp.max(sm, axis=-1, keepdims=True)
    e = jnp.where(valid, jnp.exp(sm - mx), 0.0)
    return e / jnp.maximum(jnp.sum(e, axis=-1, keepdims=True), 1e-30)


def _ada_kernel(c_ref, w_ref, b_ref, o_ref):
    c = c_ref[...]
    s = c * _sigmoid(c)
    shi, slo = _split(s)
    whi, wlo = _split(w_ref[0])
    o_ref[0] = _mm(shi, whi) + _mm(shi, wlo) + _mm(slo, whi) + b_ref[0]


def ada_params(c_all, ada_w, ada_b):
    depth, d, n6 = ada_w.shape
    r = c_all.shape[0]
    tn = 1024 if n6 % 1024 == 0 else 512
    return pl.pallas_call(
        _ada_kernel,
        out_shape=jax.ShapeDtypeStruct((depth, r, n6), F32),
        grid=(depth, n6 // tn),
        in_specs=[
            pl.BlockSpec((r, d), lambda l, j: (0, 0)),
            pl.BlockSpec((1, d, tn), lambda l, j: (l, 0, j)),
            pl.BlockSpec((1, 1, tn), lambda l, j: (l, 0, j)),
        ],
        out_specs=pl.BlockSpec((1, r, tn), lambda l, j: (l, 0, j)),
        compiler_params=_cp(("arbitrary", "arbitrary")),
        name="ada_params",
    )(c_all, ada_w, ada_b.reshape(depth, 1, n6))


_TN = 512
_SEGS = ((0, 2), (2, 1), (3, 1), (4, 1), (5, 2), (7, 4))
_N_MAIN_TILES = 11
_KV_ROWS = 2 * NSA_G


def _inproj_kernel(x_ref, g_ref, sh_ref, sc_ref, w_ref, wg_ref,
                   qn_ref, kvc_ref, kvs_ref, kvw_ref, qm_ref, kvm_ref, gate_ref, kvmb_ref, h_scr):
    j = pl.program_id(1)

    @pl.when(j == 0)
    def _():
        h = _rms_mod(x_ref[...], g_ref[...], sh_ref[0], sc_ref[0]).astype(BF16)
        h_scr[...] = h
        gate_ref[...] = _sigmoid(_mm(h, wg_ref[...]))

    z = _mm(h_scr[...], w_ref[...])
    outs = (qn_ref, kvc_ref, kvs_ref, kvw_ref, qm_ref, kvm_ref)
    for idx, (ref, (start, n)) in enumerate(zip(outs, _SEGS)):
        @pl.when((j >= start) & (j < start + n))
        def _(ref=ref, idx=idx):
            if idx in (1, 2, 3):
                rows = z.shape[0]
                for cg in range(_KV_ROWS):
                    ref[pl.ds(cg, rows, stride=_KV_ROWS), :] = z[:, cg * HD:(cg + 1) * HD]
            else:
                ref[...] = z
            if ref is kvm_ref:
                kvmb_ref[...] = z.astype(BF16)


def attn_in_proj(x2d, g, shift, scale, w_main, w_gate, tm, tiles_per_mod):
    m, d = x2d.shape
    r = shift.shape[1]

    def seg_spec(start, n):
        return pl.BlockSpec((tm, _TN), lambda i, j: (i, jnp.clip(j - start, 0, n - 1)))

    out_shape = [jax.ShapeDtypeStruct((m, n * _TN), F32) for (_, n) in _SEGS]
    out_specs = [seg_spec(s, n) for (s, n) in _SEGS]
    for k in (1, 2, 3):
        out_shape[k] = jax.ShapeDtypeStruct((m * _KV_ROWS, HD), F32)
        out_specs[k] = pl.BlockSpec((tm * _KV_ROWS, HD), lambda i, j: (i, 0))
    out_shape.append(jax.ShapeDtypeStruct((m, LANES), F32))
    out_specs.append(pl.BlockSpec((tm, LANES), lambda i, j: (i, 0)))
    out_shape.append(jax.ShapeDtypeStruct((m, _SEGS[-1][1] * _TN), BF16))
    out_specs.append(seg_spec(*_SEGS[-1]))
    mod_spec = pl.BlockSpec((1, r, d), lambda i, j: (i // tiles_per_mod, 0, 0))
    return pl.pallas_call(
        _inproj_kernel,
        out_shape=out_shape,
        grid=(m // tm, _N_MAIN_TILES),
        in_specs=[
            pl.BlockSpec((tm, d), lambda i, j: (i, 0)),
            pl.BlockSpec((1, d), lambda i, j: (0, 0)),
            mod_spec, mod_spec,
            pl.BlockSpec((d, _TN), lambda i, j: (0, j)),
            pl.BlockSpec((d, LANES), lambda i, j: (0, 0)),
        ],
        out_specs=out_specs,
        scratch_shapes=[pltpu.VMEM((tm, d), BF16)],
        compiler_params=_cp(("arbitrary", "arbitrary")),
        name="attn_in_proj",
    )(x2d, g, shift, scale, w_main, w_gate)


def _gelu_tanh(x):
    return 0.5 * x * (1.0 + jnp.tanh(0.7978845608028654 * (x + 0.044715 * x * x * x)))


def _compress_finish(ab, c, pe_ref, wab_ref, w2_ref):
    rows = ab.shape[0]
    pe2 = _mm(pe_ref[c], wab_ref[c])
    pe_term = pe2[0:1, :HD] + pe2[1:2, HD:]
    pre = ab[:, :HD] + pltpu.roll(ab[:, HD:], rows - 1, 0) + pe_term
    return _mm(_gelu_tanh(pre).astype(BF16), w2_ref[c])


def _compress_prompt_kernel(x_ref, wab_ref, pe_ref, w2_ref, ck_ref, cv_ref):
    ng = x_ref.shape[0] // (CMP_STRIDE * _KV_ROWS)
    for c, ref in ((0, ck_ref), (1, cv_ref)):
        for g in range(NSA_G):
            cg = c * NSA_G + g
            xs = jnp.concatenate([x_ref[pl.ds(r * _KV_ROWS + cg, ng, stride=CMP_STRIDE * _KV_ROWS), :]
                                  for r in range(CMP_STRIDE)], axis=1).astype(BF16)
            ab = _mm(xs, wab_ref[c])
            ref[0, g] = _compress_finish(ab, c, pe_ref, wab_ref, w2_ref).astype(BF16)


def compress_prompt(kvc, b, t, wab, pe2, w2):
    ng = t // CMP_STRIDE
    out = jax.ShapeDtypeStruct((b, NSA_G, ng, HD), BF16)
    ospec = pl.BlockSpec((1, NSA_G, ng, HD), lambda i: (i, 0, 0, 0))
    x = kvc
    return pl.pallas_call(
        _compress_prompt_kernel,
        out_shape=[out, out],
        grid=(b,),
        in_specs=[
            pl.BlockSpec((t * _KV_ROWS, HD), lambda i: (i, 0)),
            pl.BlockSpec(wab.shape, lambda i: (0, 0, 0)),
            pl.BlockSpec(pe2.shape, lambda i: (0, 0, 0)),
            pl.BlockSpec(w2.shape, lambda i: (0, 0, 0)),
        ],
        out_specs=[ospec, ospec],
        compiler_params=_cp(("arbitrary",)),
        name="nsa_compress_prompt",
    )(x, wab, pe2, w2)


def _select_blocks(score, lane, n_blocks, topn):
    cnt = jnp.zeros(score.shape, F32)
    for i in range(n_blocks):
        ci = score[:, i:i + 1]
        beats = (ci > score) | ((ci == score) & (lane > i))
        cnt = cnt + beats.astype(F32)
    return (cnt < topn) & (lane < n_blocks)


def _nsa_prompt_kernel(q_ref, ck_ref, cv_ref, kvs_ref, kvw_ref, gate_ref, c2st_ref, ee_ref,
                       o_ref, selx_ref, m_ref, l_ref, acc_ref, *, tq, tk, t_len, n_cmp, n_slc):
    g = pl.program_id(1)
    q0 = pl.program_id(2) * tq
    q = q_ref[...] * SCALE
    qh = [q[:, h * HD:(h + 1) * HD].astype(BF16) for h in range(NSA_HPG)]
    trow = q0 + lax.broadcasted_iota(I32, (tq, 1), 0)
    lane = lax.broadcasted_iota(I32, (1, LANES), 1)

    valid = (lane * CMP_STRIDE + (CMP_BLOCK - 1) <= trow) & (lane < n_cmp)
    p_h = [_masked_softmax(_nt(qh[h], ck_ref[0, 0]), valid) for h in range(NSA_HPG)]
    o_cmp = [_mm(p_h[h].astype(BF16), cv_ref[0, 0]) for h in range(NSA_HPG)]

    psum = p_h[0] + p_h[1] + p_h[2] + p_h[3]
    phi, plo = _split(psum)
    imp_t = _nt(c2st_ref[...], phi) + _nt(c2st_ref[...], plo)
    ns8 = -(-n_slc // 8) * 8
    blk = lax.broadcasted_iota(I32, (ns8, 1), 0)
    tl = q0 + lax.broadcasted_iota(I32, (1, tq), 1)
    qb = tl >> 6
    forced = (blk == 0) | (blk == qb) | (blk == qb - 1)
    score = jnp.where(forced, FORCED, imp_t[0:ns8])
    score = jnp.where(blk * SLC_BLOCK <= tl, score, -1.0)
    score = jnp.where(blk < n_slc, score, -2.0)
    cnt = jnp.zeros((ns8, tq), F32)
    for i in range(n_slc):
        ci = score[i:i + 1, :]
        cnt = cnt + ((ci > score) | ((ci == score) & (blk > i))).astype(F32)
    sel_t = ((cnt < min(SLC_TOPN, n_slc)) & (blk < n_slc)).astype(F32)
    sel = jnp.concatenate([sel_t, jnp.zeros((LANES - ns8, tq), F32)], axis=0).T
    selx_ref[...] = _mm(sel.astype(BF16), ee_ref[...])

    def slc_tile(kt, first):
        k = kvs_ref[pl.ds(kt * tk * _KV_ROWS + g, tk, stride=_KV_ROWS), :].astype(BF16)
        v = kvs_ref[pl.ds(kt * tk * _KV_ROWS + NSA_G + g, tk, stride=_KV_ROWS), :].astype(BF16)
        kpos = kt * tk + lax.broadcasted_iota(I32, (1, tk), 1)
        ok = (selx_ref[:, kt * tk:(kt + 1) * tk] > 0.5) & (kpos <= trow)
        for h in range(NSA_HPG):
            sk = jnp.where(ok, _nt(qh[h], k), NEG)
            mx = jnp.max(sk, axis=-1, keepdims=True)
            if first:
                pk = jnp.exp(sk - mx)
                m_ref[h] = mx
                l_ref[h] = jnp.sum(pk, axis=-1, keepdims=True)
                acc_ref[h] = _mm(pk.astype(BF16), v)
            else:
                m_old = m_ref[h]
                m_new = jnp.maximum(m_old, mx)
                alpha = jnp.exp(m_old - m_new)
                pk = jnp.exp(sk - m_new)
                m_ref[h] = m_new
                l_ref[h] = alpha * l_ref[h] + jnp.sum(pk, axis=-1, keepdims=True)
                acc_ref[h] = alpha * acc_ref[h] + _mm(pk.astype(BF16), v)

    slc_tile(0, True)
    for kt in range(1, t_len // tk):
        pl.when(kt * tk <= q0 + tq - 1)(functools.partial(slc_tile, kt, False))

    span = WINDOW + tq
    start = pl.multiple_of(jnp.maximum(q0 - WINDOW, 0), LANES)
    kw = kvw_ref[pl.ds(start * _KV_ROWS + g, span, stride=_KV_ROWS), :].astype(BF16)
    vw = kvw_ref[pl.ds(start * _KV_ROWS + NSA_G + g, span, stride=_KV_ROWS), :].astype(BF16)
    kpos = start + lax.broadcasted_iota(I32, (1, span), 1)
    okw = (kpos <= trow) & (kpos > trow - WINDOW)

    gt = gate_ref[...]
    for h in range(NSA_HPG):
        o_win = _mm(_masked_softmax(_nt(qh[h], kw), okw).astype(BF16), vw)
        o_slc = acc_ref[h] / l_ref[h]
        gc = [jnp.where(g == 0, gt[:, h * 3 + kk:h * 3 + kk + 1],
                        gt[:, (NSA_HPG + h) * 3 + kk:(NSA_HPG + h) * 3 + kk + 1]) for kk in range(3)]
        o_ref[:, h * HD:(h + 1) * HD] = (gc[0] * o_cmp[h] + gc[1] * o_slc + gc[2] * o_win).astype(BF16)


def _cmp_to_slc(n_cmp, n_slc):
    cs = np.arange(n_cmp) * CMP_STRIDE
    ss = np.arange(n_slc) * SLC_BLOCK
    shared = (np.minimum(cs[:, None] + CMP_BLOCK, ss[None, :] + SLC_BLOCK)
              - np.maximum(cs[:, None], ss[None, :]))
    return np.clip(shared, 0, None) / CMP_STRIDE


def nsa_prompt(qn, ck, cv, kvs, kvw, gate, b, t):
    tq, tk = 256, 512
    nq = t // tq
    n_cmp = (t - CMP_BLOCK) // CMP_STRIDE + 1
    n_slc = -(-t // SLC_BLOCK)
    c2s = np.zeros((LANES, LANES), np.float32)
    c2s[:n_slc, :n_cmp] = _cmp_to_slc(n_cmp, n_slc).T
    ee = (np.arange(LANES)[:, None] == (np.arange(t)[None, :] // SLC_BLOCK)).astype(np.float32)
    kern = functools.partial(_nsa_prompt_kernel, tq=tq, tk=tk, t_len=t, n_cmp=n_cmp, n_slc=n_slc)
    return pl.pallas_call(
        kern,
        out_shape=jax.ShapeDtypeStruct((b * t, NSA_G * NSA_HPG * HD), BF16),
        grid=(b, NSA_G, nq),
        in_specs=[
            pl.BlockSpec((tq, NSA_HPG * HD), lambda i, g, q: (i * nq + q, g)),
            pl.BlockSpec((1, 1, ck.shape[2], HD), lambda i, g, q: (i, g, 0, 0)),
            pl.BlockSpec((1, 1, cv.shape[2], HD), lambda i, g, q: (i, g, 0, 0)),
            pl.BlockSpec((t * _KV_ROWS, HD), lambda i, g, q: (i, 0)),
            pl.BlockSpec((t * _KV_ROWS, HD), lambda i, g, q: (i, 0)),
            pl.BlockSpec((tq, LANES), lambda i, g, q: (i * nq + q, 0)),
            pl.BlockSpec((LANES, LANES), lambda i, g, q: (0, 0)),
            pl.BlockSpec((LANES, t), lambda i, g, q: (0, 0)),
        ],
        out_specs=pl.BlockSpec((tq, NSA_HPG * HD), lambda i, g, q: (i * nq + q, g)),
        scratch_shapes=[pltpu.VMEM((tq, t), F32), pltpu.VMEM((NSA_HPG, tq, 1), F32),
                        pltpu.VMEM((NSA_HPG, tq, 1), F32), pltpu.VMEM((NSA_HPG, tq, HD), F32)],
        compiler_params=_cp(("arbitrary", "arbitrary", "arbitrary")),
        name="nsa_prompt",
    )(qn, ck, cv, kvs, kvw, gate, jnp.asarray(c2s, BF16), jnp.asarray(ee, BF16))


MOBA_HPS = 4
KM_ROWS = 8


def _moba_kmean_kernel(k_ref, o_ref):
    n = pl.program_id(1)

    @pl.when(n == 0)
    def _():
        o_ref[...] = jnp.zeros(o_ref.shape, F32)

    o_ref[0, pl.ds(n, 1), :] = jnp.sum(k_ref[...], axis=0, keepdims=True) * (1.0 / MOBA_BLOCK)


def moba_kmean(kvm, b, t):
    nb = t // MOBA_BLOCK
    hk = MOBA_H * HD
    assert nb <= KM_ROWS
    return pl.pallas_call(
        _moba_kmean_kernel,
        out_shape=jax.ShapeDtypeStruct((b, KM_ROWS, hk), F32),
        grid=(b, nb),
        in_specs=[pl.BlockSpec((MOBA_BLOCK, hk), lambda i, n: (i * nb + n, 0))],
        out_specs=pl.BlockSpec((1, KM_ROWS, hk), lambda i, n: (i, 0, 0)),
        compiler_params=_cp(("arbitrary", "arbitrary")),
        name="moba_kmean",
    )(kvm)


def _moba_prompt_kernel(q_ref, k_ref, v_ref, km_ref, o_ref, *, t_len):
    qi = pl.program_id(2)
    nb = t_len // MOBA_BLOCK
    tq = MOBA_BLOCK
    lane = lax.broadcasted_iota(I32, (1, LANES), 1)
    blk = lax.broadcasted_iota(I32, (KM_ROWS, 1), 0)
    row = lax.broadcasted_iota(I32, (tq, 1), 0)
    col = lax.broadcasted_iota(I32, (1, tq), 1)
    d0 = pl.multiple_of(qi * MOBA_BLOCK, MOBA_BLOCK)

    qs_l, sel_l, init = [], [], []
    for hh in range(MOBA_HPS):
        cs = slice(hh * HD, (hh + 1) * HD)
        qf = q_ref[:, cs]
        qhi, qlo = _split(qf)
        khi, klo = _split(km_ref[0, :, cs])
        gt = _nt(khi, qhi) + _nt(klo, qhi) + _nt(khi, qlo)
        gt = jnp.where(blk < qi, gt, NEG)
        cnt = jnp.zeros((KM_ROWS, tq), F32)
        for i in range(nb):
            ci = gt[i:i + 1, :]
            cnt = cnt + ((ci > gt) | ((ci == gt) & (blk > i))).astype(F32)
        selt = ((cnt < min(MOBA_TOPK, nb)) & (blk < qi)).astype(F32)
        sel_l.append(jnp.concatenate([selt, jnp.zeros((LANES - KM_ROWS, tq), F32)], axis=0).T)
        qs = (qf * SCALE).astype(BF16)
        qs_l.append(qs)
        s = jnp.where(col <= row, _nt(qs, k_ref[pl.ds(d0, MOBA_BLOCK), cs]), NEG)
        m0 = jnp.max(s, axis=-1, keepdims=True)
        p0 = jnp.exp(s - m0)
        init.append((m0, jnp.sum(p0, axis=-1, keepdims=True),
                     _mm(p0.astype(BF16), v_ref[pl.ds(d0, MOBA_BLOCK), cs])))

    pair = 2 * MOBA_BLOCK

    def body(kp, carry):
        k0 = pl.multiple_of(kp * pair, pair)
        out = []
        for hh in range(MOBA_HPS):
            cs = slice(hh * HD, (hh + 1) * HD)
            m_old, l_old, acc = carry[hh]
            on = [jnp.broadcast_to(jnp.sum(jnp.where(lane == 2 * kp + e, sel_l[hh], 0.0), axis=-1, keepdims=True),
                                   (tq, MOBA_BLOCK)) for e in range(2)]
            ok = jnp.concatenate(on, axis=1) > 0.5
            sk = jnp.where(ok, _nt(qs_l[hh], k_ref[pl.ds(k0, pair), cs]), NEG)
            m_new = jnp.maximum(m_old, jnp.max(sk, axis=-1, keepdims=True))
            alpha = jnp.exp(m_old - m_new)
            pk = jnp.exp(sk - m_new)
            out.append((m_new, alpha * l_old + jnp.sum(pk, axis=-1, keepdims=True),
                        alpha * acc + _mm(pk.astype(BF16), v_ref[pl.ds(k0, pair), cs])))
        return tuple(out)

    fin = lax.fori_loop(0, (qi + 1) >> 1, body, tuple(init))
    for hh in range(MOBA_HPS):
        _, l_fin, acc = fin[hh]
        o_ref[:, hh * HD:(hh + 1) * HD] = (acc / l_fin).astype(BF16)


def moba_prompt(qm, kvm_bf, kmean, b, t):
    nq = t // MOBA_BLOCK
    ng = MOBA_H // MOBA_HPS
    w = MOBA_HPS * HD
    kern = functools.partial(_moba_prompt_kernel, t_len=t)
    return pl.pallas_call(
        kern,
        out_shape=jax.ShapeDtypeStruct((b * t, MOBA_H * HD), BF16),
        grid=(b, ng, nq),
        in_specs=[
            pl.BlockSpec((MOBA_BLOCK, w), lambda i, h, q: (i * nq + q, h)),
            pl.BlockSpec((t, w), lambda i, h, q: (i, h)),
            pl.BlockSpec((t, w), lambda i, h, q: (i, ng + h)),
            pl.BlockSpec((1, KM_ROWS, w), lambda i, h, q: (i, 0, h)),
        ],
        out_specs=pl.BlockSpec((MOBA_BLOCK, w), lambda i, h, q: (i * nq + q, h)),
        compiler_params=_cp(("arbitrary", "arbitrary", "arbitrary")),
        name="moba_prompt",
    )(qm, kvm_bf, kvm_bf, kmean)


def _outproj_kernel(on_ref, om_ref, w1_ref, w2_ref, x_ref, g_ref, o_ref):
    acc = _mm(on_ref[...], w1_ref[...]) + _mm(om_ref[...], w2_ref[...])
    o_ref[...] = x_ref[...] + g_ref[0] * acc


def attn_out_proj(o_nsa, o_moba, w_out, x2d, gate, tm, tiles_per_mod):
    m, d = x2d.shape
    r = gate.shape[1]
    kn = o_nsa.shape[1]
    return pl.pallas_call(
        _outproj_kernel,
        out_shape=jax.ShapeDtypeStruct((m, d), F32),
        grid=(m // tm,),
        in_specs=[
            pl.BlockSpec((tm, kn), lambda i: (i, 0)),
            pl.BlockSpec((tm, kn), lambda i: (i, 0)),
            pl.BlockSpec((kn, d), lambda i: (0, 0)),
            pl.BlockSpec((kn, d), lambda i: (1, 0)),
            pl.BlockSpec((tm, d), lambda i: (i, 0)),
            pl.BlockSpec((1, r, d), lambda i: (i // tiles_per_mod, 0, 0)),
        ],
        out_specs=pl.BlockSpec((tm, d), lambda i: (i, 0)),
        compiler_params=_cp(("arbitrary",)),
        name="attn_out_proj",
    )(o_nsa, o_moba, w_out, w_out, x2d, gate)


_GROUP_LANE0 = 64
U32 = jnp.uint32
_HI16 = 0xFFFF0000


def _pack_bf16_pairs(xb):
    half = xb.shape[1] // 2
    bits = lax.bitcast_convert_type(xb.astype(F32), U32)
    return (bits[:, half:] & jnp.uint32(_HI16)) | (bits[:, :half] >> 16)


def _unpack_bf16_pairs(xp):
    lo = lax.bitcast_convert_type(xp << 16, F32)
    hi = lax.bitcast_convert_type(xp & jnp.uint32(_HI16), F32)
    return jnp.concatenate([lo, hi], axis=1)


def _router_kernel(x_ref, g_ref, sh_ref, sc_ref, wh_ref, wl_ref, b_ref, h_ref, meta_ref, cnt_ref, carry_ref,
                   *, tm):
    i = pl.program_id(0)

    @pl.when(i == 0)
    def _():
        carry_ref[...] = jnp.zeros(carry_ref.shape, F32)

    h = _rms_mod(x_ref[...], g_ref[...], sh_ref[0], sc_ref[0])
    hhi, hlo = _split(h)
    h_ref[...] = _pack_bf16_pairs(hhi)
    lg = _mm(hhi, wh_ref[...]) + _mm(hhi, wl_ref[...]) + _mm(hlo, wh_ref[...]) + b_ref[...]
    lane = lax.broadcasted_iota(I32, (1, LANES), 1)
    lanef = lane.astype(F32)
    big = 1.0e9

    isg = (lane >= _GROUP_LANE0) & (lane < _GROUP_LANE0 + N_GROUPS)
    mxg = jnp.max(jnp.where(isg, lg, NEG), axis=-1, keepdims=True)
    grp = jnp.min(jnp.where(isg & (lg == mxg), lanef - _GROUP_LANE0, big), axis=-1, keepdims=True)
    pg = 1.0 / jnp.sum(jnp.where(isg, jnp.exp(lg - mxg), 0.0), axis=-1, keepdims=True)

    ing = (lane < N_EXP) & ((lane >> 3).astype(F32) == grp)
    l1 = jnp.max(jnp.where(ing, lg, NEG), axis=-1, keepdims=True)
    i1 = jnp.min(jnp.where(ing & (lg == l1), lanef, big), axis=-1, keepdims=True)
    ing2 = ing & (lanef != i1)
    l2 = jnp.max(jnp.where(ing2, lg, NEG), axis=-1, keepdims=True)
    i2 = jnp.min(jnp.where(ing2 & (lg == l2), lanef, big), axis=-1, keepdims=True)
    e21 = jnp.exp(l2 - l1)
    w1 = pg / (1.0 + e21)
    w2 = pg * e21 / (1.0 + e21)

    oh1 = lanef == i1
    oh2 = lanef == i2
    oh = (oh1 | oh2).astype(F32)
    r_i = lax.broadcasted_iota(I32, (tm, tm), 0)
    c_i = lax.broadcasted_iota(I32, (tm, tm), 1)
    lower = (c_i < r_i).astype(BF16)
    pref = _mm(lower, oh.astype(BF16)) + carry_ref[0:1, :]
    r1 = jnp.sum(jnp.where(oh1, pref, 0.0), axis=-1, keepdims=True)
    r2 = jnp.sum(jnp.where(oh2, pref, 0.0), axis=-1, keepdims=True)
    carry_ref[0:1, :] = carry_ref[0:1, :] + jnp.sum(oh, axis=0, keepdims=True)
    cnt_ref[...] = carry_ref[...]

    meta = jnp.where(lane == 0, i1, 0.0) + jnp.where(lane == 1, i2, 0.0) + jnp.where(lane == 2, r1, 0.0) \
        + jnp.where(lane == 3, r2, 0.0) + jnp.where(lane == 4, w1, 0.0) + jnp.where(lane == 5, w2, 0.0)
    meta_ref[...] = meta


def moe_router(x2d, g, shift, scale, wr_hi, wr_lo, br, tm, tiles_per_mod):
    m, d = x2d.shape
    r = shift.shape[1]
    mod_spec = pl.BlockSpec((1, r, d), lambda i: (i // tiles_per_mod, 0, 0))
    kern = functools.partial(_router_kernel, tm=tm)
    return pl.pallas_call(
        kern,
        out_shape=[jax.ShapeDtypeStruct((m, d // 2), U32), jax.ShapeDtypeStruct((m, LANES), F32),
                   jax.ShapeDtypeStruct((8, LANES), F32)],
        grid=(m // tm,),
        in_specs=[
            pl.BlockSpec((tm, d), lambda i: (i, 0)),
            pl.BlockSpec((1, d), lambda i: (0, 0)),
            mod_spec, mod_spec,
            pl.BlockSpec((d, LANES), lambda i: (0, 0)),
            pl.BlockSpec((d, LANES), lambda i: (0, 0)),
            pl.BlockSpec((1, LANES), lambda i: (0, 0)),
        ],
        out_specs=[pl.BlockSpec((tm, d // 2), lambda i: (i, 0)), pl.BlockSpec((tm, LANES), lambda i: (i, 0)),
                   pl.BlockSpec((8, LANES), lambda i: (0, 0))],
        scratch_shapes=[pltpu.VMEM((8, LANES), F32)],
        compiler_params=_cp(("arbitrary",)),
        name="moe_router",
    )(x2d, g, shift, scale, wr_hi, wr_lo, br)


def _scatter_kernel(seg_ref, has_ref, nu_ref, dest_ref, h_ref, xs_ref, zbuf, hbuf, sem, zsem,
                    *, tm, n_tiles, n_steps):
    i = pl.program_id(0)

    def zero_copy(row0):
        return pltpu.make_async_copy(zbuf, xs_ref.at[pl.ds(pl.multiple_of(row0, TE), TE), :], zsem)

    @pl.when(i == 0)
    def _():
        zbuf[...] = jnp.zeros(zbuf.shape, zbuf.dtype)
        for e in range(N_EXP):
            pl.when(has_ref[e] > 0)(lambda e=e: zero_copy(seg_ref[e]).start())

        def tail_start(tl, c):
            zero_copy(tl * TE).start()
            return c

        def tail_wait(tl, c):
            zero_copy(tl * TE).wait()
            return c

        lax.fori_loop(nu_ref[0], n_tiles, tail_start, 0)
        for e in range(N_EXP):
            pl.when(has_ref[e] > 0)(lambda e=e: zero_copy(seg_ref[e]).wait())
        lax.fori_loop(nu_ref[0], n_tiles, tail_wait, 0)

    slot = i % 2
    hbuf[slot] = h_ref[...]

    def row_copy(s, r, k):
        return pltpu.make_async_copy(hbuf.at[s, pl.ds(r, 1), :], xs_ref.at[pl.ds(dest_ref[2 * r + k], 1), :],
                                     sem.at[s])

    def start(r, c):
        row_copy(slot, r, 0).start(priority=0)
        row_copy(slot, r, 1).start(priority=1)
        return c

    def wait_slot(s):
        def wait(r, c):
            row_copy(s, r, 0).wait()
            row_copy(s, r, 1).wait()
            return c
        lax.fori_loop(0, tm, wait, 0, unroll=8)

    lax.fori_loop(0, tm, start, 0, unroll=8)
    pl.when(i > 0)(functools.partial(wait_slot, 1 - slot))
    pl.when(i == n_steps - 1)(functools.partial(wait_slot, slot))


def moe_scatter(h, dest_flat, seg_last, seg_has, n_used, n_rows, tm):
    m, d = h.shape
    kern = functools.partial(_scatter_kernel, tm=tm, n_tiles=n_rows // TE, n_steps=m // tm)
    gs = pltpu.PrefetchScalarGridSpec(
        num_scalar_prefetch=3,
        grid=(m // tm,),
        in_specs=[
            pl.BlockSpec((2 * tm,), lambda i, *_: (i,), memory_space=pltpu.SMEM),
            pl.BlockSpec((tm, d), lambda i, *_: (i, 0)),
        ],
        out_specs=pl.BlockSpec(memory_space=pl.ANY),
        scratch_shapes=[pltpu.VMEM((TE, d), h.dtype), pltpu.VMEM((2, tm, d), h.dtype),
                        pltpu.SemaphoreType.DMA((2,)), pltpu.SemaphoreType.DMA(())],
    )
    return pl.pallas_call(
        kern,
        out_shape=jax.ShapeDtypeStruct((n_rows, d), h.dtype),
        grid_spec=gs,
        compiler_params=_cp(("arbitrary",)),
        name="moe_scatter",
    )(seg_last, seg_has, n_used, dest_flat, h)


def _expert_kernel(te_ref, nu_ref, xs_ref, wg_ref, wu_ref, wd_ref, ys_ref):
    i = pl.program_id(0)

    @pl.when(i < nu_ref[0])
    def _():
        xb = _unpack_bf16_pairs(xs_ref[...]).astype(BF16)
        a = _mm(xb, wg_ref[0, 0].astype(BF16))
        u = _mm(xb, wu_ref[0, 0].astype(BF16))
        hid = (a * _sigmoid(a) * u).astype(BF16)
        ys_ref[...] = _pack_bf16_pairs(_mm(hid, wd_ref[0, 0].astype(BF16)).astype(BF16))

    @pl.when(i >= nu_ref[0])
    def _():
        ys_ref[...] = jnp.zeros(ys_ref.shape, U32)


def moe_experts(xs, tile_expert, n_used, w_gate, w_up, w_down, layer):
    n_rows, dh = xs.shape
    d = 2 * dh
    f = w_gate.shape[-1]
    nt = n_rows // TE
    gs = pltpu.PrefetchScalarGridSpec(
        num_scalar_prefetch=2,
        grid=(nt,),
        in_specs=[
            pl.BlockSpec((TE, dh), lambda i, te, nu: (jnp.minimum(i, nu[0] - 1), 0)),
            pl.BlockSpec((1, 1, d, f), lambda i, te, nu: (layer, te[i], 0, 0)),
            pl.BlockSpec((1, 1, d, f), lambda i, te, nu: (layer, te[i], 0, 0)),
            pl.BlockSpec((1, 1, f, d), lambda i, te, nu: (layer, te[i], 0, 0)),
        ],
        out_specs=pl.BlockSpec((TE, dh), lambda i, te, nu: (i, 0)),
    )
    return pl.pallas_call(
        _expert_kernel,
        out_shape=jax.ShapeDtypeStruct((n_rows, dh), U32),
        grid_spec=gs,
        compiler_params=_cp(("arbitrary",)),
        name="moe_experts",
    )(tile_expert, n_used, xs, w_gate, w_up, w_down)


def _combine_kernel(dcur_ref, dnext_ref, x_ref, g_ref, meta_ref, fg_ref, ys_ref, o_ref, rows, sem,
                    *, tm, final, n_steps):
    i = pl.program_id(0)
    slot = i % 2

    def row_copy(dref, s, r, k):
        return pltpu.make_async_copy(ys_ref.at[pl.ds(dref[2 * r + k], 1), :],
                                     rows.at[s, k, pl.ds(r, 1), :], sem.at[s])

    def gather(dref, s):
        def start(r, c):
            row_copy(dref, s, r, 0).start(priority=0)
            row_copy(dref, s, r, 1).start(priority=1)
            return c
        lax.fori_loop(0, tm, start, 0, unroll=8)

    pl.when(i == 0)(functools.partial(gather, dcur_ref, 0))
    pl.when(i + 1 < n_steps)(functools.partial(gather, dnext_ref, 1 - slot))

    def wait(r, c):
        row_copy(dcur_ref, slot, r, 0).wait()
        row_copy(dcur_ref, slot, r, 1).wait()
        return c

    lax.fori_loop(0, tm, wait, 0, unroll=8)
    meta = meta_ref[...]
    y = meta[:, 4:5] * _unpack_bf16_pairs(rows[slot, 0]) + meta[:, 5:6] * _unpack_bf16_pairs(rows[slot, 1])
    x = x_ref[...] + g_ref[0] * y
    if final:
        x = (x * lax.rsqrt(jnp.mean(x * x, axis=-1, keepdims=True) + EPS)) * fg_ref[...]
    o_ref[...] = x


def moe_combine(x2d, gate, meta, final_g, ys, dest_flat, tm, tiles_per_mod, final):
    m, d = x2d.shape
    r = gate.shape[1]
    n_steps = m // tm
    kern = functools.partial(_combine_kernel, tm=tm, final=final, n_steps=n_steps)
    return pl.pallas_call(
        kern,
        out_shape=jax.ShapeDtypeStruct((m, d), F32),
        grid=(n_steps,),
        in_specs=[
            pl.BlockSpec((2 * tm,), lambda i: (i,), memory_space=pltpu.SMEM),
            pl.BlockSpec((2 * tm,), lambda i: (jnp.minimum(i + 1, n_steps - 1),), memory_space=pltpu.SMEM),
            pl.BlockSpec((tm, d), lambda i: (i, 0)),
            pl.BlockSpec((1, r, d), lambda i: (i // tiles_per_mod, 0, 0)),
            pl.BlockSpec((tm, LANES), lambda i: (i, 0)),
            pl.BlockSpec((1, d), lambda i: (0, 0)),
            pl.BlockSpec(memory_space=pl.ANY),
        ],
        out_specs=pl.BlockSpec((tm, d), lambda i: (i, 0)),
        scratch_shapes=[pltpu.VMEM((2, 2, tm, d // 2), U32), pltpu.SemaphoreType.DMA((2,))],
        compiler_params=_cp(("arbitrary",)),
        name="moe_combine",
    )(dest_flat, dest_flat, x2d, gate, meta, final_g, ys)


def hier_moe_block(x2d, g_ffn, shift, scale, gate, p_layer, layer, tm, tiles_per_mod, final_g, final):
    m, d = x2d.shape
    wr_hi, wr_lo, br, w_gate, w_up, w_down = p_layer
    h, meta, cnt = moe_router(x2d, g_ffn, shift, scale, wr_hi, wr_lo, br, tm, tiles_per_mod)
    counts = cnt[0, :N_EXP].astype(I32)
    padded = ((counts + TE - 1) // TE) * TE
    ends = jnp.cumsum(padded)
    offs = ends - padded
    eid = meta[:, 0:2].astype(I32)
    off_sel = jnp.sum(jnp.where(eid[:, :, None] == jnp.arange(N_EXP, dtype=I32)[None, None, :],
                                offs[None, None, :], 0), axis=-1)
    dest = (off_sel + meta[:, 2:4].astype(I32)).reshape(-1)
    n_tiles = (2 * m) // TE + N_EXP
    n_used = (ends[-1] // TE).astype(I32).reshape(1)
    tile_start = jnp.arange(n_tiles, dtype=I32) * TE
    tile_clamped = jnp.minimum(tile_start, ends[-1] - 1)
    tile_expert = jnp.minimum(jnp.sum((tile_clamped[:, None] >= ends[None, :]).astype(I32), axis=1), N_EXP - 1)
    seg_last = jnp.maximum(ends - TE, 0).astype(I32)
    seg_has = (counts > 0).astype(I32)
    xs = moe_scatter(h, dest, seg_last, seg_has, n_used, n_tiles * TE, tm)
    ys = moe_experts(xs, tile_expert, n_used, w_gate, w_up, w_down, layer)
    return moe_combine(x2d, gate, meta, final_g, ys, dest, tm, tiles_per_mod, final)


def _pw1_kernel(x_ref, g_ref, sh_ref, sc_ref, wa_ref, wb_ref, ba_ref, bb_ref, u_ref, h_scr):
    @pl.when(pl.program_id(1) == 0)
    def _():
        h_scr[...] = _rms_mod(x_ref[...], g_ref[...], sh_ref[0], sc_ref[0]).astype(BF16)

    h = h_scr[...]
    a = _mm(h, wa_ref[...]) + ba_ref[...]
    b = _mm(h, wb_ref[...]) + bb_ref[...]
    u_ref[...] = a * _sigmoid(b)


def conv_pw1_glu(x2d, g, shift, scale, w_pw1, b_pw1, tm, tiles_per_mod):
    m, d = x2d.shape
    r = shift.shape[1]
    dc = w_pw1.shape[1] // 2
    tn = 1024 if dc % 1024 == 0 else 256
    nj = dc // tn
    mod_spec = pl.BlockSpec((1, r, d), lambda i, j: (i // tiles_per_mod, 0, 0))
    return pl.pallas_call(
        _pw1_kernel,
        out_shape=jax.ShapeDtypeStruct((m, dc), F32),
        grid=(m // tm, nj),
        in_specs=[
            pl.BlockSpec((tm, d), lambda i, j: (i, 0)),
            pl.BlockSpec((1, d), lambda i, j: (0, 0)),
            mod_spec, mod_spec,
            pl.BlockSpec((d, tn), lambda i, j: (0, j)),
            pl.BlockSpec((d, tn), lambda i, j: (0, nj + j)),
            pl.BlockSpec((1, tn), lambda i, j: (0, j)),
            pl.BlockSpec((1, tn), lambda i, j: (0, nj + j)),
        ],
        out_specs=pl.BlockSpec((tm, tn), lambda i, j: (i, j)),
        scratch_shapes=[pltpu.VMEM((tm, d), BF16)],
        compiler_params=_cp(("arbitrary", "arbitrary")),
        name="conv_pw1_glu",
    )(x2d, g, shift, scale, w_pw1, w_pw1, b_pw1, b_pw1)


_TAIL = 32


def _ln_silu(y, g, b):
    yc = y - jnp.mean(y, axis=-1, keepdims=True)
    z = yc * lax.rsqrt(jnp.mean(yc * yc, axis=-1, keepdims=True) + EPS)
    z = z * g + b
    return z * _sigmoid(z)


_CONV_CH = 512


def _conv_prompt_kernel(u_ref, wdw_ref, bdw_ref, lg_ref, lb_ref, w2_ref, b2_ref, x_ref, g_ref, o_ref, buf, sh,
                        acc_scr, *, tt):
    ti = pl.program_id(1)

    @pl.when(ti == 0)
    def _():
        buf[0:_TAIL, :] = jnp.zeros((_TAIL, buf.shape[1]), F32)

    buf[_TAIL:_TAIL + tt, :] = u_ref[...]
    off = _TAIL - (CONV_W - 1)
    sub = 8
    span = tt + _TAIL - sub
    dc = buf.shape[1]
    ch = sh.shape[2]
    for c0 in range(0, dc, ch):
        cs = slice(c0, c0 + ch)
        for r in range(1, sub):
            sh[r, 0:span, :] = buf[r:r + span, cs]
        acc = None
        for k in range(CONV_W):
            a, r = divmod(off + k, sub)
            src = buf[sub * a:sub * a + tt, cs] if r == 0 else sh[r, sub * a:sub * a + tt, :]
            term = wdw_ref[k:k + 1, cs] * src
            acc = term if acc is None else acc + term
        acc_scr[:, cs] = acc
    tail = buf[tt:tt + _TAIL, :]
    buf[0:_TAIL, :] = tail
    z = _ln_silu(acc_scr[...] + bdw_ref[...], lg_ref[...], lb_ref[...]).astype(BF16)
    out = _mm(z, w2_ref[...]) + b2_ref[...]
    o_ref[...] = x_ref[...] + g_ref[0] * out


def conv_prompt(u, w_dw, b_dw, ln_g, ln_b, w_pw2, b_pw2, x2d, gate, b, t):
    m, d = x2d.shape
    dc = u.shape[1]
    tt = 256
    nt = t // tt
    kern = functools.partial(_conv_prompt_kernel, tt=tt)
    vec = lambda n: pl.BlockSpec((1, n), lambda i, j: (0, 0))
    return pl.pallas_call(
        kern,
        out_shape=jax.ShapeDtypeStruct((m, d), F32),
        grid=(b, nt),
        in_specs=[
            pl.BlockSpec((tt, dc), lambda i, j: (i * nt + j, 0)),
            pl.BlockSpec((_TAIL, dc), lambda i, j: (0, 0)),
            vec(dc), vec(dc), vec(dc),
            pl.BlockSpec((dc, d), lambda i, j: (0, 0)),
            vec(d),
            pl.BlockSpec((tt, d), lambda i, j: (i * nt + j, 0)),
            pl.BlockSpec((1, 1, d), lambda i, j: (i, 0, 0)),
        ],
        out_specs=pl.BlockSpec((tt, d), lambda i, j: (i * nt + j, 0)),
        scratch_shapes=[pltpu.VMEM((_TAIL + tt, dc), F32), pltpu.VMEM((8, _TAIL + tt, min(_CONV_CH, dc)), F32),
                        pltpu.VMEM((tt, dc), F32)],
        compiler_params=_cp(("arbitrary", "arbitrary")),
        name="conv_prompt",
    )(u, w_dw, b_dw, ln_g, ln_b, w_pw2, b_pw2, x2d, gate)


def _prep_params(p):
    d = p["attn_w_in"].shape[1]
    nsa_q = NSA_G * NSA_HPG * HD
    nsa_kv = 2 * NSA_G * HD
    gate_w = 3 * NSA_G * NSA_HPG
    c_gate = nsa_q + 3 * nsa_kv
    w_in = p["attn_w_in"][0]
    w_main = jnp.concatenate([w_in[:, :c_gate], w_in[:, c_gate + gate_w:]], axis=1).astype(BF16)
    w_gate = jnp.pad(w_in[:, c_gate:c_gate + gate_w], ((0, 0), (0, LANES - gate_w))).astype(BF16)
    w1 = p["nsa_cmp_w1"][0]
    wab = jnp.concatenate([w1[:, :CMP_STRIDE].reshape(2, CMP_STRIDE * HD, HD),
                           w1[:, CMP_STRIDE:].reshape(2, CMP_STRIDE * HD, HD)], axis=2).astype(BF16)
    pe = p["nsa_cmp_pe"][0].reshape(2, 2, CMP_STRIDE * HD)
    pe2 = jnp.pad(pe, ((0, 0), (0, 14), (0, 0))).astype(BF16)
    w2 = p["nsa_cmp_w2"][0].astype(BF16)
    moe = []
    for layer in range(p["moe_wg"].shape[0]):
        wr = jnp.zeros((d, LANES), F32)
        wr = wr.at[:, :N_EXP].set(p["moe_we"][layer]).at[:, _GROUP_LANE0:_GROUP_LANE0 + N_GROUPS].set(
            p["moe_wg"][layer])
        br = jnp.zeros((1, LANES), F32)
        br = br.at[0, :N_EXP].set(p["moe_be"][layer]).at[0, _GROUP_LANE0:_GROUP_LANE0 + N_GROUPS].set(
            p["moe_bg"][layer])
        hi, lo = _split(wr)
        moe.append((hi, lo, br, p["moe_w_gate"], p["moe_w_up"], p["moe_w_down"]))
    w_dw = jnp.pad(p["conv_w_dw"][0], ((0, _TAIL - CONV_W), (0, 0)))
    return dict(
        w_main=w_main, w_gate=w_gate, wab=wab, pe2=pe2, w2=w2,
        w_out=p["attn_w_out"][0].astype(BF16), moe=moe,
        w_pw1=p["conv_w_pw1"][0].astype(BF16), b_pw1=p["conv_b_pw1"][0][None, :],
        w_dw=w_dw, b_dw=p["conv_b_dw"][0][None, :], ln_g=p["conv_ln_g"][0][None, :],
        ln_b=p["conv_ln_b"][0][None, :], w_pw2=p["conv_w_pw2"][0].astype(BF16),
        b_pw2=p["conv_b_pw2"][0][None, :],
        norm_mix_g=p["norm_mix_g"], norm_ffn_g=p["norm_ffn_g"], final_g=p["final_norm_g"][None, :],
    )


def _trunk_prompt(x, mods, pp):
    b, t, d = x.shape
    m = b * t
    x2d = x.reshape(m, d)
    sh1, sc1, g1, sh2, sc2, g2 = mods[0]
    tm = 512
    qn, kvc, kvs, kvw, qm, kvm, gate, kvm_bf = attn_in_proj(
        x2d, pp["norm_mix_g"][0:1], sh1, sc1, pp["w_main"], pp["w_gate"], tm, t // tm)
    ck, cv = compress_prompt(kvc, b, t, pp["wab"], pp["pe2"], pp["w2"])
    o_nsa = nsa_prompt(qn, ck, cv, kvs, kvw, gate, b, t)
    o_moba = moba_prompt(qm, kvm_bf, moba_kmean(kvm, b, t), b, t)
    tm2 = 256
    x1 = attn_out_proj(o_nsa, o_moba, pp["w_out"], x2d, g1, tm2, t // tm2)
    x2 = hier_moe_block(x1, pp["norm_ffn_g"][0:1], sh2, sc2, g2, pp["moe"][0], 0, tm2, t // tm2,
                        pp["final_g"], False)
    sh1, sc1, g1, sh2, sc2, g2 = mods[1]
    u = conv_pw1_glu(x2, pp["norm_mix_g"][1:2], sh1, sc1, pp["w_pw1"], pp["b_pw1"], tm, t // tm)
    x3 = conv_prompt(u, pp["w_dw"], pp["b_dw"], pp["ln_g"], pp["ln_b"], pp["w_pw2"], pp["b_pw2"], x2, g1, b, t)
    y = hier_moe_block(x3, pp["norm_ffn_g"][1:2], sh2, sc2, g2, pp["moe"][1], 1, tm2, t // tm2,
                       pp["final_g"], True)
    wlen = min(WINDOW, t)
    state = (
        kvc.reshape(1, b, t, 2, NSA_G, HD), kvs.reshape(1, b, t, 2, NSA_G, HD),
        kvw.reshape(b, t, 2, NSA_G, HD)[None, :, t - wlen:], kvm.reshape(1, b, t, 2, MOBA_H, HD),
        u.reshape(b, t, -1)[None, :, t - (CONV_W - 1):],
    )
    return y.reshape(b, t, d), state


TPAD = 8


def _page_specs(npg, n_pages, block, col_block):
    nd = len(block)

    def spec(i):
        def imap(b, j, pt):
            return (pt[b * n_pages + j * npg + i],) + (0,) * (nd - 2) + (col_block,)
        return pl.BlockSpec(block, imap)

    return [spec(i) for i in range(npg)]


def _compress_sample_kernel(pt_ref, *refs, npg, nsteps):
    pages = refs[:npg]
    wab_ref, pe_ref, w2_ref, ck_ref, cv_ref, ab_scr = refs[npg:]
    j = pl.program_id(1)
    rows = npg * (PAGE // CMP_STRIDE)
    r0 = pl.multiple_of(j * rows, rows)
    gpp = PAGE // CMP_STRIDE
    rpt = 2 * NSA_G
    for c in range(2):
        for g in range(NSA_G):
            cg = c * NSA_G + g
            xs = jnp.concatenate(
                [jnp.concatenate([pg[pl.ds(r * rpt + cg, gpp, stride=CMP_STRIDE * rpt), :]
                                  for r in range(CMP_STRIDE)], axis=1) for pg in pages], axis=0).astype(BF16)
            ab_scr[cg, pl.ds(r0, rows), :] = _mm(xs, wab_ref[c])

    @pl.when(j == nsteps - 1)
    def _():
        for c, ref in ((0, ck_ref), (1, cv_ref)):
            for g in range(NSA_G):
                ref[0, g] = _compress_finish(ab_scr[c * NSA_G + g], c, pe_ref, wab_ref, w2_ref).astype(BF16)


def compress_sample(pt, cache, bs, n_pages, wab, pe2, w2):
    npg = 32
    nsteps = n_pages // npg
    gpp = PAGE // CMP_STRIDE
    ng = n_pages * gpp
    x = cache.reshape(-1, HD)
    kern = functools.partial(_compress_sample_kernel, npg=npg, nsteps=nsteps)
    out = jax.ShapeDtypeStruct((bs, NSA_G, ng, HD), BF16)
    ospec = pl.BlockSpec((1, NSA_G, ng, HD), lambda b, j, pt: (b, 0, 0, 0))
    const = lambda a: pl.BlockSpec(a.shape, lambda b, j, pt: (0,) * a.ndim)
    gs = pltpu.PrefetchScalarGridSpec(
        num_scalar_prefetch=1,
        grid=(bs, nsteps),
        in_specs=_page_specs(npg, n_pages, (PAGE * 2 * NSA_G, HD), 0) + [const(wab), const(pe2), const(w2)],
        out_specs=[ospec, ospec],
        scratch_shapes=[pltpu.VMEM((2 * NSA_G, ng, 2 * HD), F32)],
    )
    return pl.pallas_call(
        kern, out_shape=[out, out], grid_spec=gs,
        compiler_params=_cp(("arbitrary", "arbitrary")), name="nsa_compress_sample",
    )(pt, *([x] * npg), wab, pe2, w2)


def _nsa_sample_kernel(pt_ref, q_ref, ck_ref, cv_ref, ksn_ref, kwn_ref, win_ref, gate_ref, c2s_ref, *refs,
                       npg, nsteps, past, ts, n_cmp, n_slc):
    pages = refs[:npg]
    o_ref, selx_scr, m_scr, l_scr, acc_scr, part_scr = refs[npg:]
    j = pl.program_id(1)
    qr = NSA_HPG * TPAD
    keys = npg * PAGE
    rpt = 2 * NSA_G
    wbuf = win_ref.shape[0] // rpt
    t_row = past + (lax.broadcasted_iota(I32, (qr, 1), 0) & (TPAD - 1))
    lane = lax.broadcasted_iota(I32, (1, LANES), 1)

    @pl.when(j == 0)
    def _():
        t8 = past + lax.broadcasted_iota(I32, (TPAD, 1), 0)
        for g in range(NSA_G):
            q = (q_ref[0, g] * SCALE).astype(BF16)
            gt = gate_ref[0, g]
            ncg = ck_ref.shape[2]
            lane_c = lax.broadcasted_iota(I32, (1, ncg), 1)
            valid = (lane_c * CMP_STRIDE + (CMP_BLOCK - 1) <= t_row) & (lane_c < n_cmp)
            p = _masked_softmax(_nt(q, ck_ref[0, g]), valid)
            o_cmp = _mm(p.astype(BF16), cv_ref[0, g])
            psum = p[0:TPAD] + p[TPAD:2 * TPAD] + p[2 * TPAD:3 * TPAD] + p[3 * TPAD:4 * TPAD]
            phi, plo = _split(psum)
            imp = _mm(phi, c2s_ref[...]) + _mm(plo, c2s_ref[...])
            nl = c2s_ref.shape[1]
            lane_s = lax.broadcasted_iota(I32, (1, nl), 1)
            qb = t8 >> 6
            forced = (lane_s == 0) | (lane_s == qb) | (lane_s == qb - 1)
            score = jnp.where(forced, FORCED, imp)
            score = jnp.where(lane_s * SLC_BLOCK <= t8, score, -1.0)
            score = jnp.where(lane_s < n_slc, score, -2.0)
            sel = _select_blocks(score, lane_s, n_slc, min(SLC_TOPN, n_slc)).astype(BF16)
            bps = keys // SLC_BLOCK
            for jj in range(nsteps):
                blk = lax.broadcasted_iota(I32, (nl, keys), 0)
                key = lax.broadcasted_iota(I32, (nl, keys), 1)
                ee = (blk == jj * bps + (key >> 6)).astype(BF16)
                selx_scr[g, jj] = _mm(sel, ee)
            own = jnp.sum(jnp.where(lane_s == (past >> 6), sel.astype(F32), 0.0), axis=-1, keepdims=True)
            own4 = jnp.concatenate([own] * NSA_HPG, axis=0) > 0.5
            zpad = jnp.zeros((LANES - TPAD, HD), F32)
            kw = jnp.concatenate([win_ref[pl.ds(g, wbuf, stride=rpt), :],
                                  kwn_ref[0, :, g * HD:(g + 1) * HD], zpad], axis=0).astype(BF16)
            vw = jnp.concatenate([win_ref[pl.ds(NSA_G + g, wbuf, stride=rpt), :],
                                  kwn_ref[0, :, (NSA_G + g) * HD:(NSA_G + g + 1) * HD], zpad], axis=0).astype(BF16)
            idx = lax.broadcasted_iota(I32, (1, wbuf + LANES), 1)
            kpos = past - wbuf + idx
            okw = (idx < wbuf + ts) & (kpos <= t_row) & (kpos > t_row - WINDOW)
            o_win = _mm(_masked_softmax(_nt(q, kw), okw).astype(BF16), vw)
            part_scr[g] = gt[:, 0:1] * o_cmp + gt[:, 2:3] * o_win
            kn = jnp.concatenate([ksn_ref[0, :, g * HD:(g + 1) * HD], zpad], axis=0).astype(BF16)
            vn = jnp.concatenate([ksn_ref[0, :, (NSA_G + g) * HD:(NSA_G + g + 1) * HD], zpad], axis=0).astype(BF16)
            ok0 = (past + lane <= t_row) & (lane < ts) & own4
            s0 = jnp.where(ok0, _nt(q, kn), NEG)
            m0 = jnp.max(s0, axis=-1, keepdims=True)
            p0 = jnp.where(ok0, jnp.exp(s0 - m0), 0.0)
            m_scr[g] = m0
            l_scr[g] = jnp.sum(p0, axis=-1, keepdims=True)
            acc_scr[g] = _mm(p0.astype(BF16), vn)

    for g in range(NSA_G):
        q = (q_ref[0, g] * SCALE).astype(BF16)
        k = jnp.concatenate([pg[pl.ds(g, PAGE, stride=rpt), :] for pg in pages], axis=0).astype(BF16)
        v = jnp.concatenate([pg[pl.ds(NSA_G + g, PAGE, stride=rpt), :] for pg in pages], axis=0).astype(BF16)
        mk = selx_scr[g, j]
        ok = jnp.concatenate([mk] * NSA_HPG, axis=0) > 0.5
        s = jnp.where(ok, _nt(q, k), NEG)
        m_old = m_scr[g]
        m_new = jnp.maximum(m_old, jnp.max(s, axis=-1, keepdims=True))
        alpha = jnp.exp(m_old - m_new)
        pk = jnp.where(ok, jnp.exp(s - m_new), 0.0)
        m_scr[g] = m_new
        l_scr[g] = alpha * l_scr[g] + jnp.sum(pk, axis=-1, keepdims=True)
        acc_scr[g] = alpha * acc_scr[g] + _mm(pk.astype(BF16), v)

    @pl.when(j == nsteps - 1)
    def _():
        for g in range(NSA_G):
            o_slc = acc_scr[g] / jnp.maximum(l_scr[g], 1e-30)
            o_ref[0, g] = (part_scr[g] + gate_ref[0, g][:, 1:2] * o_slc).astype(BF16)


def nsa_sample(pt, q_g, ck, cv, ksn, kwn, win, gate_g, cache, bs, n_pages, ts):
    npg = 16
    nsteps = n_pages // npg
    past = n_pages * PAGE
    n_cmp = (past + ts - CMP_BLOCK) // CMP_STRIDE + 1
    n_slc = -(-(past + ts) // SLC_BLOCK)
    ncg = ck.shape[2]
    nl = -(-n_slc // LANES) * LANES
    c2s = np.zeros((ncg, nl), np.float32)
    c2s[:n_cmp, :n_slc] = _cmp_to_slc(n_cmp, n_slc)
    c2s = jnp.asarray(c2s, BF16)
    qr = NSA_HPG * TPAD
    keys = npg * PAGE
    rpt = 2 * NSA_G
    x = cache.reshape(-1, HD)
    kern = functools.partial(_nsa_sample_kernel, npg=npg, nsteps=nsteps, past=past, ts=ts, n_cmp=n_cmp,
                             n_slc=n_slc)

    def per_b(a):
        return pl.BlockSpec((1,) + a.shape[1:], lambda b, j, pt: (b,) + (0,) * (a.ndim - 1))

    win_rows = win.shape[0] // bs
    gs = pltpu.PrefetchScalarGridSpec(
        num_scalar_prefetch=1,
        grid=(bs, nsteps),
        in_specs=[per_b(q_g), per_b(ck), per_b(cv), per_b(ksn), per_b(kwn),
                  pl.BlockSpec((win_rows, HD), lambda b, j, pt: (b, 0)), per_b(gate_g),
                  pl.BlockSpec(c2s.shape, lambda b, j, pt: (0, 0))]
        + _page_specs(npg, n_pages, (PAGE * rpt, HD), 0),
        out_specs=pl.BlockSpec((1, NSA_G, qr, HD), lambda b, j, pt: (b, 0, 0, 0)),
        scratch_shapes=[pltpu.VMEM((NSA_G, nsteps, TPAD, keys), F32), pltpu.VMEM((NSA_G, qr, 1), F32),
                        pltpu.VMEM((NSA_G, qr, 1), F32), pltpu.VMEM((NSA_G, qr, HD), F32),
                        pltpu.VMEM((NSA_G, qr, HD), F32)],
    )
    return pl.pallas_call(
        kern, out_shape=jax.ShapeDtypeStruct((bs, NSA_G, qr, HD), BF16), grid_spec=gs,
        compiler_params=_cp(("arbitrary", "arbitrary")), name="nsa_sample",
    )(pt, q_g, ck, cv, ksn, kwn, win, gate_g, c2s, *([x] * npg))


def _moba_gate_kernel(pt_ref, q_ref, *refs, npg, nsteps, nb_past):
    pages = refs[:npg]
    sel_ref, km_scr = refs[npg:]
    j = pl.program_id(1)
    ppb = MOBA_BLOCK // PAGE
    bps = npg // ppb

    @pl.when(j == 0)
    def _():
        km_scr[...] = jnp.zeros(km_scr.shape, F32)

    for i in range(bps):
        ssum = jnp.sum(pages[ppb * i][...], axis=0)[0]
        for pp_ in range(1, ppb):
            ssum = ssum + jnp.sum(pages[ppb * i + pp_][...], axis=0)[0]
        ssum = ssum * (1.0 / MOBA_BLOCK)
        for h in range(MOBA_H):
            km_scr[h, pl.ds(j * bps + i, 1), :] = ssum[h:h + 1, :]

    @pl.when(j == nsteps - 1)
    def _():
        lane = lax.broadcasted_iota(I32, (1, LANES), 1)
        for h in range(MOBA_H):
            qhi, qlo = _split(q_ref[0, h])
            khi, klo = _split(km_scr[h])
            gate = _nt(qhi, khi) + _nt(qhi, klo) + _nt(qlo, khi)
            gate = jnp.where(lane < nb_past, gate, NEG)
            sel = _select_blocks(gate, lane, nb_past + 1, min(MOBA_TOPK, nb_past + 1)) & (lane < nb_past)
            sel_ref[0, h] = sel.astype(F32)


def moba_gate_sample(pt, q_h, cache, bs, n_pages):
    npg = 8
    nsteps = n_pages // npg
    hk = MOBA_H * HD
    x = cache.reshape(-1, 2, MOBA_H, HD)
    kern = functools.partial(_moba_gate_kernel, npg=npg, nsteps=nsteps, nb_past=n_pages * PAGE // MOBA_BLOCK)
    gs = pltpu.PrefetchScalarGridSpec(
        num_scalar_prefetch=1,
        grid=(bs, nsteps),
        in_specs=[pl.BlockSpec((1, MOBA_H, TPAD, HD), lambda b, j, pt: (b, 0, 0, 0))]
        + _page_specs(npg, n_pages, (PAGE, 1, MOBA_H, HD), 0),
        out_specs=pl.BlockSpec((1, MOBA_H, TPAD, LANES), lambda b, j, pt: (b, 0, 0, 0)),
        scratch_shapes=[pltpu.VMEM((MOBA_H, LANES, HD), F32)],
    )
    return pl.pallas_call(
        kern, out_shape=jax.ShapeDtypeStruct((bs, MOBA_H, TPAD, LANES), F32), grid_spec=gs,
        compiler_params=_cp(("arbitrary", "arbitrary")), name="moba_gate_sample",
    )(pt, q_h, *([x] * npg))


def _moba_sample_kernel(pt_ref, qbd_ref, sel_ref, kvn_ref, *refs, npg, nsteps, ts):
    pages = refs[:npg]
    o_ref, m_scr, l_scr, acc_scr = refs[npg:]
    j = pl.program_id(1)
    hk = MOBA_H * HD
    qr = MOBA_H * TPAD
    qbd = qbd_ref[0]
    lane = lax.broadcasted_iota(I32, (1, LANES), 1)

    def diag(o_all):
        return jnp.concatenate([o_all[h * TPAD:(h + 1) * TPAD, h * HD:(h + 1) * HD] for h in range(MOBA_H)],
                               axis=0)

    @pl.when(j == 0)
    def _():
        zpad = jnp.zeros((LANES - TPAD, hk), F32)
        kn = jnp.concatenate([kvn_ref[0, :, :hk], zpad], axis=0).astype(BF16)
        vn = jnp.concatenate([kvn_ref[0, :, hk:], zpad], axis=0).astype(BF16)
        t8 = lax.broadcasted_iota(I32, (qr, 1), 0) & (TPAD - 1)
        ok0 = (lane <= t8) & (lane < ts)
        s0 = jnp.where(ok0, _nt(qbd, kn), NEG)
        m0 = jnp.max(s0, axis=-1, keepdims=True)
        p0 = jnp.where(ok0, jnp.exp(s0 - m0), 0.0)
        m_scr[...] = m0
        l_scr[...] = jnp.sum(p0, axis=-1, keepdims=True)
        acc_scr[...] = diag(_mm(p0.astype(BF16), vn))

    rpt = 2 * MOBA_H

    def heads(pg, c):
        return jnp.concatenate([pg[pl.ds(c * MOBA_H + h, PAGE, stride=rpt), :] for h in range(MOBA_H)], axis=1)

    k = jnp.concatenate([heads(pg, 0) for pg in pages], axis=0).astype(BF16)
    v = jnp.concatenate([heads(pg, 1) for pg in pages], axis=0).astype(BF16)
    s = _nt(qbd, k)
    sel = sel_ref[0]
    bps = npg * PAGE // MOBA_BLOCK
    cols = []
    for bb in range(bps):
        on = jnp.sum(jnp.where(lane == j * bps + bb, sel, 0.0), axis=-1, keepdims=True)
        cols.append(jnp.broadcast_to(on, (qr, MOBA_BLOCK)))
    ok = jnp.concatenate(cols, axis=1) > 0.5
    s = jnp.where(ok, s, NEG)
    m_old = m_scr[...]
    m_new = jnp.maximum(m_old, jnp.max(s, axis=-1, keepdims=True))
    alpha = jnp.exp(m_old - m_new)
    pk = jnp.where(ok, jnp.exp(s - m_new), 0.0)
    m_scr[...] = m_new
    l_scr[...] = alpha * l_scr[...] + jnp.sum(pk, axis=-1, keepdims=True)
    acc_scr[...] = alpha * acc_scr[...] + diag(_mm(pk.astype(BF16), v))

    @pl.when(j == nsteps - 1)
    def _():
        o_ref[0] = (acc_scr[...] / l_scr[...]).astype(BF16)


def moba_sample(pt, q_bd, sel, kvn, cache, bs, n_pages, ts):
    npg = 8
    nsteps = n_pages // npg
    hk = MOBA_H * HD
    qr = MOBA_H * TPAD
    x = cache.reshape(-1, HD)
    kern = functools.partial(_moba_sample_kernel, npg=npg, nsteps=nsteps, ts=ts)

    def per_b(a):
        return pl.BlockSpec((1,) + a.shape[1:], lambda b, j, pt: (b,) + (0,) * (a.ndim - 1))

    gs = pltpu.PrefetchScalarGridSpec(
        num_scalar_prefetch=1,
        grid=(bs, nsteps),
        in_specs=[per_b(q_bd), per_b(sel), per_b(kvn)]
        + _page_specs(npg, n_pages, (PAGE * 2 * MOBA_H, HD), 0),
        out_specs=pl.BlockSpec((1, qr, HD), lambda b, j, pt: (b, 0, 0)),
        scratch_shapes=[pltpu.VMEM((qr, 1), F32), pltpu.VMEM((qr, 1), F32), pltpu.VMEM((qr, HD), F32)],
    )
    return pl.pallas_call(
        kern, out_shape=jax.ShapeDtypeStruct((bs, qr, HD), BF16), grid_spec=gs,
        compiler_params=_cp(("arbitrary", "arbitrary")), name="moba_sample",
    )(pt, q_bd, sel, kvn, *([x] * npg))


def _conv_sample_kernel(st_ref, u_ref, wdw_ref, bdw_ref, lg_ref, lb_ref, w2_ref, b2_ref, x_ref, g_ref, o_ref,
                        *, ts):
    nst = CONV_W - 1
    bs = st_ref.shape[1]
    zs = []
    for t in range(ts):
        acc = None
        for k in range(CONV_W):
            r = t + k
            row = st_ref[r] if r < nst else u_ref[r - nst]
            term = wdw_ref[k:k + 1, :] * row
            acc = term if acc is None else acc + term
        zs.append(_ln_silu(acc + bdw_ref[...], lg_ref[...], lb_ref[...]))
    z = jnp.concatenate(zs, axis=0).astype(BF16)
    out = _mm(z, w2_ref[...]) + b2_ref[...]
    for t in range(ts):
        o_ref[t] = x_ref[t] + g_ref[t] * out[t * bs:(t + 1) * bs]


def conv_sample(st_t, u_t, w_dw, b_dw, ln_g, ln_b, w_pw2, b_pw2, x_t, g_t):
    ts = u_t.shape[0]
    kern = functools.partial(_conv_sample_kernel, ts=ts)
    full = lambda a: pl.BlockSpec(a.shape, lambda i: (0,) * a.ndim)
    args = (st_t, u_t, w_dw, b_dw, ln_g, ln_b, w_pw2, b_pw2, x_t, g_t)
    return pl.pallas_call(
        kern, out_shape=jax.ShapeDtypeStruct(x_t.shape, F32), grid=(1,),
        in_specs=[full(a) for a in args], out_specs=full(x_t),
        compiler_params=_cp(("arbitrary",)), name="conv_sample",
    )(*args)


def _trunk_sample(x, mods, pp, past):
    bs, ts, d = x.shape
    m = bs * ts
    n_pages = past["page_table"].shape[1]
    plen = n_pages * PAGE
    assert ts <= TPAD and ts < CMP_STRIDE and plen % MOBA_BLOCK == 0 and m % 8 == 0
    pt = past["page_table"].reshape(-1).astype(I32)
    x2d = x.reshape(m, d)
    sh1, sc1, g1, sh2, sc2, g2 = mods[0]
    qn, kvc, kvs, kvw, qm, kvm, gate, _ = attn_in_proj(
        x2d, pp["norm_mix_g"][0:1], sh1, sc1, pp["w_main"], pp["w_gate"], m, 1)

    def pad_t(a, axis):
        w = [(0, 0)] * a.ndim
        w[axis] = (0, TPAD - ts)
        return jnp.pad(a, w)

    ck, cv = compress_sample(pt, past["nsa_cmp"], bs, n_pages, pp["wab"], pp["pe2"], pp["w2"])
    q_g = pad_t(qn.reshape(bs, ts, NSA_G, NSA_HPG, HD).transpose(0, 2, 3, 1, 4), 3)
    q_g = q_g.reshape(bs, NSA_G, NSA_HPG * TPAD, HD)
    gate_g = pad_t(gate[:, :3 * NSA_G * NSA_HPG].reshape(bs, ts, NSA_G, NSA_HPG, 3).transpose(0, 2, 3, 1, 4), 3)
    gate_g = jnp.pad(gate_g.reshape(bs, NSA_G, NSA_HPG * TPAD, 3), ((0, 0), (0, 0), (0, 0), (0, LANES - 3)))
    ksn = pad_t(kvs.reshape(bs, ts, 4 * HD), 1)
    kwn = pad_t(kvw.reshape(bs, ts, 4 * HD), 1)
    win = past["nsa_win"].reshape(-1, HD)
    o_g = nsa_sample(pt, q_g, ck, cv, ksn, kwn, win, gate_g, past["nsa_slc"], bs, n_pages, ts)
    o_nsa = o_g.reshape(bs, NSA_G, NSA_HPG, TPAD, HD)[:, :, :, :ts].transpose(0, 3, 1, 2, 4).reshape(m, -1)

    q_h = pad_t(qm.reshape(bs, ts, MOBA_H, HD).transpose(0, 2, 1, 3), 2)
    sel = moba_gate_sample(pt, q_h, past["moba"], bs, n_pages)
    eye = jnp.eye(MOBA_H, dtype=F32)
    q_bd = ((q_h * SCALE)[:, :, :, None, :] * eye[None, :, None, :, None]).astype(BF16)
    q_bd = q_bd.reshape(bs, MOBA_H * TPAD, MOBA_H * HD)
    kvn = pad_t(kvm.reshape(bs, ts, 2 * MOBA_H * HD), 1)
    o_m = moba_sample(pt, q_bd, sel.reshape(bs, MOBA_H * TPAD, LANES), kvn, past["moba"], bs, n_pages, ts)
    o_moba = o_m.reshape(bs, MOBA_H, TPAD, HD)[:, :, :ts].transpose(0, 2, 1, 3).reshape(m, -1)

    x1 = attn_out_proj(o_nsa, o_moba, pp["w_out"], x2d, g1, m, 1)
    x2 = hier_moe_block(x1, pp["norm_ffn_g"][0:1], sh2, sc2, g2, pp["moe"][0], 0, m, 1, pp["final_g"], False)
    sh1, sc1, g1, sh2, sc2, g2 = mods[1]
    u = conv_pw1_glu(x2, pp["norm_mix_g"][1:2], sh1, sc1, pp["w_pw1"], pp["b_pw1"], m, 1)
    tb = lambda a: a.reshape(bs, ts, -1).transpose(1, 0, 2)
    st = past["conv"][0]
    x3_t = conv_sample(st.transpose(1, 0, 2), tb(u), pp["w_dw"], pp["b_dw"], pp["ln_g"], pp["ln_b"],
                       pp["w_pw2"], pp["b_pw2"], tb(x2), tb(g1[0]))
    x3 = x3_t.transpose(1, 0, 2).reshape(m, d)
    y = hier_moe_block(x3, pp["norm_ffn_g"][1:2], sh2, sc2, g2, pp["moe"][1], 1, m, 1, pp["final_g"], True)
    state = (
        kvc.reshape(1, bs, ts, 2, NSA_G, HD), kvs.reshape(1, bs, ts, 2, NSA_G, HD),
        jnp.concatenate([past["nsa_win"][0][:, ts:], kvw.reshape(bs, ts, 2, NSA_G, HD)], axis=1)[None],
        kvm.reshape(1, bs, ts, 2, MOBA_H, HD),
        jnp.concatenate([st[:, ts:], u.reshape(bs, ts, -1)], axis=1)[None],
    )
    return y.reshape(bs, ts, d), state


def _mods_from(m_all, rows, expand):
    out = []
    for layer in range(m_all.shape[0]):
        parts = jnp.split(m_all[layer, rows], 6, axis=-1)
        if expand:
            parts = [jnp.repeat(a, expand, axis=0)[None] for a in parts]
        else:
            parts = [a[:, None, :] for a in parts]
        out.append(parts)
    return out


def kernel(x_prompt, x_sample, cache_nsa_cmp_kv, cache_nsa_slc_kv, state_nsa_win_kv, cache_moba_kv, state_conv, page_table, c_prompt, c_sample, norm_mix_g, norm_ffn_g, ada_w, ada_b, attn_w_in, attn_w_out, nsa_cmp_pe, nsa_cmp_w1, nsa_cmp_w2, conv_w_pw1, conv_b_pw1, conv_w_dw, conv_b_dw, conv_ln_g, conv_ln_b, conv_w_pw2, conv_b_pw2, moe_wg, moe_bg, moe_we, moe_be, moe_w_gate, moe_w_up, moe_w_down, final_norm_g):
    p = dict(norm_mix_g=norm_mix_g, norm_ffn_g=norm_ffn_g, ada_w=ada_w, ada_b=ada_b, attn_w_in=attn_w_in,
             attn_w_out=attn_w_out, nsa_cmp_pe=nsa_cmp_pe, nsa_cmp_w1=nsa_cmp_w1, nsa_cmp_w2=nsa_cmp_w2,
             conv_w_pw1=conv_w_pw1, conv_b_pw1=conv_b_pw1, conv_w_dw=conv_w_dw, conv_b_dw=conv_b_dw,
             conv_ln_g=conv_ln_g, conv_ln_b=conv_ln_b, conv_w_pw2=conv_w_pw2, conv_b_pw2=conv_b_pw2,
             moe_wg=moe_wg, moe_bg=moe_bg, moe_we=moe_we, moe_be=moe_be, moe_w_gate=moe_w_gate,
             moe_w_up=moe_w_up, moe_w_down=moe_w_down, final_norm_g=final_norm_g)
    pp = _prep_params(p)
    bp = x_prompt.shape[0]
    bs, ts, d = x_sample.shape
    c_all = jnp.concatenate([c_prompt, c_sample], axis=0)
    pad = (-c_all.shape[0]) % 16
    c_all = jnp.pad(c_all, ((0, pad), (0, 0)))
    m_all = ada_params(c_all, ada_w, ada_b)
    mods_p = _mods_from(m_all, slice(0, bp), 0)
    y_p, (cmp_p, slc_p, win_p, moba_p, conv_p) = _trunk_prompt(x_prompt, mods_p, pp)
    mods_s = _mods_from(m_all, slice(bp, bp + bs), ts)
    past = dict(page_table=page_table, nsa_cmp=cache_nsa_cmp_kv, nsa_slc=cache_nsa_slc_kv,
                nsa_win=state_nsa_win_kv, moba=cache_moba_kv, conv=state_conv)
    y_s, (cmp_s, slc_s, win_s, moba_s, conv_s) = _trunk_sample(x_sample, mods_s, pp, past)
    return (y_p, y_s, cmp_p, cmp_s, slc_p, slc_s, win_p, win_s, moba_p, moba_s, conv_p, conv_s)
```
